```python
import jax, jax.numpy as jnp
from jax import lax
import numpy as np

D_MODEL = 1024
BATCH = 8
SEQ = 2048
DEPTH = 2
DEC_BATCH = 128
DEC_SEQ = 8
PAST_LEN = 16384
PAGE_SIZE = 128

GLA_HEADS = 4
GLA_KEY_DIM = D_MODEL // 2
GLA_VAL_DIM = D_MODEL
GLA_HEAD_K = GLA_KEY_DIM // GLA_HEADS
GLA_HEAD_V = GLA_VAL_DIM // GLA_HEADS
GLA_GATE_RANK = 16
GLA_GATE_NORMALIZER = 16.0
GLA_CHUNK = 64
POOL_WIDTH = D_MODEL // 2
POOL_WINDOWS = (2, 4, 8, 16)
POOL_GROUPS = 4
POOL_GROUP_DIM = POOL_WIDTH // POOL_GROUPS
POOL_HIST = 15
SG_WIDTH = D_MODEL // 2
SG_GROUPS = 4
SG_GROUP_DIM = SG_WIDTH // SG_GROUPS
SG_CHUNK = 128
N_BRANCHES = 3
D_FF = 4 * D_MODEL
EPS = 1e-6
SPLIT_SIZES = (GLA_KEY_DIM, GLA_KEY_DIM, GLA_VAL_DIM, GLA_VAL_DIM, GLA_GATE_RANK,
               POOL_WIDTH, SG_WIDTH, SG_WIDTH, N_BRANCHES * D_MODEL)
IN_COLS = 2 * GLA_KEY_DIM + 2 * GLA_VAL_DIM + GLA_GATE_RANK + POOL_WIDTH + 2 * SG_WIDTH + N_BRANCHES * D_MODEL

kernel_name = "hybrid_gla_pool_sgmlp_decode_step"


def rms_norm(x, w):
    xf = x.astype(jnp.float32)
    y = xf * lax.rsqrt(jnp.mean(xf * xf, axis=-1, keepdims=True) + EPS)
    return (y * w.astype(jnp.float32)).astype(x.dtype)


def gla_mix(q, k, v, g_log, s0, chunk):
    n, t, _ = q.shape
    nc = t // chunk

    def heads(a, d):
        return a.astype(jnp.float32).reshape(n, nc, chunk, GLA_HEADS, d).transpose(1, 0, 3, 2, 4)

    qs = heads(q, GLA_HEAD_K) * (GLA_HEAD_K ** -0.5)
    ks = heads(k, GLA_HEAD_K)
    vs = heads(v, GLA_HEAD_V)
    gs = heads(g_log, GLA_HEAD_K)
    tri = jnp.tril(jnp.ones((chunk, chunk), dtype=bool))[:, :, None]

    def step(s, inp):
        qc, kc, vc, gc = inp
        b = jnp.cumsum(gc, axis=2)
        inter = jnp.einsum('nhtd,nhde->nhte', qc * jnp.exp(b), s)
        diff = b[:, :, :, None, :] - b[:, :, None, :, :]
        decay = jnp.exp(jnp.where(tri, diff, -jnp.inf))
        att = jnp.einsum('nhtd,nhsd,nhtsd->nhts', qc, kc, decay)
        o = inter + jnp.einsum('nhts,nhse->nhte', att, vc)
        b_last = b[:, :, -1:, :]
        s_new = jnp.exp(b_last[:, :, 0, :])[..., None] * s + jnp.einsum(
            'nhsd,nhse->nhde', kc * jnp.exp(b_last - b), vc)
        return s_new, o

    s_fin, o = lax.scan(step, s0.astype(jnp.float32), (qs, ks, vs, gs))
    o = o.transpose(1, 0, 3, 2, 4).reshape(n, t, GLA_HEADS, GLA_HEAD_V)
    return o, s_fin


def pool_mix(xp, hist, n_prev, w_mix, scale):
    n, t, _ = xp.shape
    xf = xp.astype(jnp.float32)
    full = jnp.concatenate([hist.astype(jnp.float32), xf], axis=1)
    cs = jnp.concatenate([jnp.zeros((n, 1, POOL_WIDTH), jnp.float32), jnp.cumsum(full, axis=1)], axis=1)
    end = cs[:, POOL_HIST + 1:]
    pos = jnp.arange(t)
    means = []
    for gi, w in enumerate(POOL_WINDOWS):
        lo, hi = gi * POOL_GROUP_DIM, (gi + 1) * POOL_GROUP_DIM
        start = cs[:, POOL_HIST + 1 - w: POOL_HIST + 1 - w + t, lo:hi]
        cnt = jnp.minimum(w, pos + 1 + n_prev).astype(jnp.float32)[None, :, None]
        means.append((end[:, :, lo:hi] - start) / cnt)
    pooled = (jnp.concatenate(means, axis=-1) - xf).reshape(n, t, POOL_GROUPS, POOL_GROUP_DIM)
    mixed = jnp.einsum('ntgc,gcd->ntgd', pooled, w_mix.astype(jnp.float32)).reshape(n, t, POOL_WIDTH)
    out = mixed * scale.astype(jnp.float32)
    return out.astype(xp.dtype), full[:, -POOL_HIST:]


def spatial_gate(u, v, w_s, b_s):
    n, t, _ = u.shape
    length = min(t, SG_CHUNK)
    nc = t // length
    w = w_s[:, :length, :length] * jnp.tril(jnp.ones((length, length), w_s.dtype))
    vg = v.reshape(n, nc, length, SG_GROUPS, SG_GROUP_DIM)
    mixed = jnp.einsum('gts,ncsgd->nctgd', w, vg) + b_s[:, :length].T[None, None, :, :, None]
    return u * mixed.reshape(n, t, SG_WIDTH)


def mixer_block(h, norm_mix, w_in, w_gk2, b_gk, gla_norm, w_pool_mix, pool_scale,
                w_spatial, b_spatial, w_br_a, w_br_b, w_br_c, w_out,
                s_gla, pool_hist, n_prev, gla_chunk):
    n, t, _ = h.shape
    hn = rms_norm(h, norm_mix)
    proj = hn @ w_in
    offs = [int(o) for o in np.cumsum(SPLIT_SIZES)[:-1]]
    q, k, v, g_out, g_lr, xp, u, vv, gates = jnp.split(proj, offs, axis=-1)
    g_log = jax.nn.log_sigmoid((g_lr @ w_gk2 + b_gk).astype(jnp.float32)) / GLA_GATE_NORMALIZER
    o, s_new = gla_mix(q, k, v, g_log, s_gla, gla_chunk)
    o = rms_norm(o, gla_norm) * jax.nn.silu(g_out.astype(jnp.float32).reshape(n, t, GLA_HEADS, GLA_HEAD_V))
    branch_a = o.reshape(n, t, GLA_VAL_DIM).astype(h.dtype) @ w_br_a
    pool_out, new_hist = pool_mix(xp, pool_hist, n_prev, w_pool_mix, pool_scale)
    branch_b = pool_out @ w_br_b
    u = jax.nn.gelu(u)
    vv = jax.nn.gelu(vv)
    branch_c = spatial_gate(u, vv, w_spatial, b_spatial) @ w_br_c
    ga, gb, gc = jnp.split(jax.nn.sigmoid(gates.astype(jnp.float32)), N_BRANCHES, axis=-1)
    merged = (ga * branch_a + gb * branch_b + gc * branch_c).astype(h.dtype)
    return h + merged @ w_out, s_new, new_hist, vv


def ffn_block(h, norm_ffn, w_ff1, w_ff2):
    hn = rms_norm(h, norm_ffn)
    a = jax.nn.relu(hn @ w_ff1)
    return h + (a * a) @ w_ff2


def setup_inputs(seed: int = 0) -> dict:
    key = jax.random.key(seed)
    ks = jax.random.split(key, 24)
    f32 = jnp.float32

    def nrm(k, shape, scale):
        return jax.random.normal(k, shape, f32) * scale

    return {
        "x_prompt": nrm(ks[0], (BATCH, SEQ, D_MODEL), 1.0),
        "x_sample": nrm(ks[1], (DEC_BATCH, DEC_SEQ, D_MODEL), 1.0),
        "state_gla": nrm(ks[2], (DEPTH, DEC_BATCH, GLA_HEADS, GLA_HEAD_K, GLA_HEAD_V), 1.0),
        "state_pool": nrm(ks[3], (DEPTH, DEC_BATCH, POOL_HIST, POOL_WIDTH), 1.0),
        "norm_mix": 1.0 + nrm(ks[4], (DEPTH, D_MODEL), 0.02),
        "w_in": nrm(ks[5], (DEPTH, D_MODEL, IN_COLS), D_MODEL ** -0.5),
        "w_gk2": nrm(ks[6], (DEPTH, GLA_GATE_RANK, GLA_KEY_DIM), GLA_GATE_RANK ** -0.5),
        "b_gk": nrm(ks[7], (DEPTH, GLA_KEY_DIM), 0.01),
        "gla_norm": 1.0 + nrm(ks[8], (DEPTH, GLA_HEAD_V), 0.02),
        "w_pool_mix": nrm(ks[9], (DEPTH, POOL_GROUPS, POOL_GROUP_DIM, POOL_GROUP_DIM), POOL_GROUP_DIM ** -0.5),
        "pool_scale": 1.0 + nrm(ks[10], (DEPTH, POOL_WIDTH), 0.02),
        "w_spatial": nrm(ks[11], (DEPTH, SG_GROUPS, SG_CHUNK, SG_CHUNK), SG_CHUNK ** -0.5),
        "b_spatial": 1.0 + nrm(ks[12], (DEPTH, SG_GROUPS, SG_CHUNK), 0.02),
        "w_br_a": nrm(ks[13], (DEPTH, GLA_VAL_DIM, D_MODEL), GLA_VAL_DIM ** -0.5),
        "w_br_b": nrm(ks[14], (DEPTH, POOL_WIDTH, D_MODEL), POOL_WIDTH ** -0.5),
        "w_br_c": nrm(ks[15], (DEPTH, SG_WIDTH, D_MODEL), SG_WIDTH ** -0.5),
        "w_out": nrm(ks[16], (DEPTH, D_MODEL, D_MODEL), D_MODEL ** -0.5),
        "norm_ffn": 1.0 + nrm(ks[17], (DEPTH, D_MODEL), 0.02),
        "w_ff1": nrm(ks[18], (DEPTH, D_MODEL, D_FF), D_MODEL ** -0.5),
        "w_ff2": nrm(ks[19], (DEPTH, D_FF, D_MODEL), D_FF ** -0.5),
        "norm_final": 1.0 + nrm(ks[20], (D_MODEL,), 0.02),
    }


def reference(x_prompt, x_sample, state_gla, state_pool, norm_mix, w_in, w_gk2, b_gk, gla_norm,
              w_pool_mix, pool_scale, w_spatial, b_spatial, w_br_a, w_br_b, w_br_c, w_out,
              norm_ffn, w_ff1, w_ff2, norm_final):
    hp, hs = x_prompt, x_sample
    gla_p, gla_s, pool_p, pool_s, sgv_s = [], [], [], [], []
    prompt_chunk = min(GLA_CHUNK, hp.shape[1])
    sample_chunk = hs.shape[1]
    sample_prev = min(PAST_LEN, POOL_HIST)
    for l in range(DEPTH):
        layer_w = (norm_mix[l], w_in[l], w_gk2[l], b_gk[l], gla_norm[l], w_pool_mix[l], pool_scale[l],
                   w_spatial[l], b_spatial[l], w_br_a[l], w_br_b[l], w_br_c[l], w_out[l])
        s0 = jnp.zeros((hp.shape[0], GLA_HEADS, GLA_HEAD_K, GLA_HEAD_V), jnp.float32)
        hist0 = jnp.zeros((hp.shape[0], POOL_HIST, POOL_WIDTH), jnp.float32)
        hp, sp, histp, _ = mixer_block(hp, *layer_w, s0, hist0, 0, prompt_chunk)
        hp = ffn_block(hp, norm_ffn[l], w_ff1[l], w_ff2[l])
        hs, ss, hists, vs = mixer_block(hs, *layer_w, state_gla[l], state_pool[l], sample_prev, sample_chunk)
        hs = ffn_block(hs, norm_ffn[l], w_ff1[l], w_ff2[l])
        gla_p.append(sp.astype(x_prompt.dtype))
        gla_s.append(ss.astype(state_gla.dtype))
        pool_p.append(histp.astype(x_prompt.dtype))
        pool_s.append(hists.astype(state_pool.dtype))
        sgv_s.append(vs)
    y_prompt = rms_norm(hp, norm_final)
    y_sample = rms_norm(hs, norm_final)
    return (y_prompt, y_sample, jnp.stack(gla_p), jnp.stack(gla_s), jnp.stack(pool_p),
            jnp.stack(pool_s), jnp.stack(sgv_s))
```

```python
import functools

import jax
import jax.numpy as jnp
from jax import lax
from jax.experimental import pallas as pl
from jax.experimental.pallas import tpu as pltpu

F32 = jnp.float32
BF16 = jnp.bfloat16

D_MODEL = 1024
GLA_HEADS = 4
GLA_KEY_DIM = 512
GLA_VAL_DIM = 1024
GLA_HEAD_K = 128
GLA_HEAD_V = 256
GLA_GATE_RANK = 16
GLA_GATE_NORMALIZER = 16.0
POOL_WIDTH = 512
POOL_WINDOWS = (2, 4, 8, 16)
POOL_GROUPS = 4
POOL_GROUP_DIM = 128
POOL_HIST = 15
SG_WIDTH = 512
SG_GROUPS = 4
SG_GROUP_DIM = 128
SG_CHUNK = 128
N_BRANCHES = 3
D_FF = 4096
EPS = 1e-6

SUBLANES = 8
LANES = 128
VMEM_LIMIT_BYTES = 56 * 1024 * 1024

C_QK = (0, 1024)
C_VG = (1024, 3072)
C_XP = (3072, 3584)
C_U = (3584, 4096)
C_VV = (4096, 4608)
C_GATES = (4608, 7680)
C_GLR = (7680, 7808)
IN_COLS_R = 7808

GLA_CHUNK = 128
GLA_BLOCK = SUBLANES
SAMPLE_SEQS_PER_STEP = 16


def _resident(shape):
    nd = len(shape)
    return pl.BlockSpec(shape, lambda *_: (0,) * nd, pipeline_mode=pl.Buffered(1))


def _rms(x, w):
    return x * lax.rsqrt(jnp.mean(x * x, axis=-1, keepdims=True) + EPS) * w


def _dot(a, b):
    return jnp.dot(a, b, preferred_element_type=F32)


def _dot_nt(a, b):
    return lax.dot_general(a, b, (((1,), (1,)), ((), ())), preferred_element_type=F32)


def _inproj_body(h_ref, nw_ref, w_ref, qk_ref, vg_ref, xp_ref, u_ref, vv_ref, gt_ref, glr_ref):
    xn = _rms(h_ref[...], nw_ref[...]).astype(BF16)

    def proj(lo, hi):
        return _dot(xn, w_ref[:, lo:hi])

    qk_ref[...] = proj(*C_QK).astype(BF16)
    for j in range(2):
        lo = C_VG[0] + j * 1024
        vg_ref[:, j * 1024:(j + 1) * 1024] = proj(lo, lo + 1024).astype(BF16)
    xp_ref[...] = proj(*C_XP)
    u_ref[...] = jax.nn.gelu(proj(*C_U)).astype(u_ref.dtype)
    vv_ref[...] = jax.nn.gelu(proj(*C_VV)).astype(vv_ref.dtype)
    for j in range(N_BRANCHES):
        lo = C_GATES[0] + j * D_MODEL
        gt_ref[:, j * D_MODEL:(j + 1) * D_MODEL] = jax.nn.sigmoid(proj(lo, lo + D_MODEL)).astype(BF16)
    glr_ref[...] = proj(*C_GLR)


def _inproj(h, norm_w, w_r, *, tm, vv_dtype):
    m = h.shape[0]
    row = lambda w: pl.BlockSpec((tm, w), lambda i: (i, 0))
    out_shape = (
        jax.ShapeDtypeStruct((m, 1024), BF16),
        jax.ShapeDtypeStruct((m, 2048), BF16),
        jax.ShapeDtypeStruct((m, POOL_WIDTH), F32),
        jax.ShapeDtypeStruct((m, SG_WIDTH), BF16),
        jax.ShapeDtypeStruct((m, SG_WIDTH), vv_dtype),
        jax.ShapeDtypeStruct((m, N_BRANCHES * D_MODEL), BF16),
        jax.ShapeDtypeStruct((m, LANES), F32),
    )
    return pl.pallas_call(
        _inproj_body,
        grid=(m // tm,),
        in_specs=[row(D_MODEL), _resident((1, D_MODEL)), _resident((D_MODEL, IN_COLS_R))],
        out_specs=[row(1024), row(2048), row(POOL_WIDTH), row(SG_WIDTH), row(SG_WIDTH),
                   row(N_BRANCHES * D_MODEL), row(LANES)],
        out_shape=out_shape,
        compiler_params=pltpu.CompilerParams(
            dimension_semantics=("arbitrary",), vmem_limit_bytes=VMEM_LIMIT_BYTES),
        name="inproj",
    )(h, norm_w, w_r)


def _bcast_row(x, blk, r):
    c, w = x.shape
    x3 = x.reshape(c // blk, blk, w)
    return jnp.broadcast_to(x3[:, r:r + 1, :], x3.shape).reshape(c, w)


def _row_iota(c, w):
    return lax.broadcasted_iota(jnp.int32, (c, w), 0)


def _seg_cumsum(g, seg):
    c, w = g.shape
    tpos = _row_iota(c, w) % seg
    sh = 1
    while sh < seg:
        g = g + jnp.where(tpos >= sh, pltpu.roll(g, sh, 0), 0.0)
        sh *= 2
    return g


def _gla_log_decay(glr, wgk, bgk):
    x = _dot(glr.astype(BF16), wgk) + bgk
    return jax.nn.log_sigmoid(x) / GLA_GATE_NORMALIZER


def _gla_block_diag(q, k, v, b):
    c = q.shape[0]
    tin_k = _row_iota(c, GLA_HEAD_K) % GLA_BLOCK
    o = jnp.zeros((c, GLA_HEAD_V), F32)
    for r in range(GLA_BLOCK):
        br = _bcast_row(b, GLA_BLOCK, r)
        kr = _bcast_row(k, GLA_BLOCK, r)
        vr = _bcast_row(v, GLA_BLOCK, r)
        e = jnp.exp(jnp.where(tin_k >= r, b - br, -jnp.inf))
        a = jnp.sum(q * kr * e, axis=-1, keepdims=True)
        o = o + a * vr
    return o


def _gla_levels(q, k, b, c):
    t_k = _row_iota(c, GLA_HEAD_K)
    t_c = _row_iota(c, c)
    s_c = lax.broadcasted_iota(jnp.int32, (c, c), 1)
    att = jnp.zeros((c, c), F32)
    lvl = 2 * GLA_BLOCK
    while lvl <= c:
        half = lvl // 2
        second = (t_k % lvl) >= half
        bmid = _bcast_row(b, lvl, half - 1)
        e = jnp.exp(jnp.where(second, b - bmid, bmid - b))
        ql = jnp.where(second, q * e, 0.0).astype(BF16)
        kl = jnp.where(second, 0.0, k * e).astype(BF16)
        att_l = _dot_nt(ql, kl)
        if lvl < c:
            att_l = jnp.where((t_c // lvl) == (s_c // lvl), att_l, 0.0)
        att = att + att_l
        lvl *= 2
    return att


def _gla_out(o, gn, g_out):
    return (_rms(o, gn) * jax.nn.silu(g_out)).astype(BF16)


def _gla_prompt_body(qk_ref, vg_ref, glr_ref, wgk_ref, bgk_ref, gn_ref, og_ref, sfin_ref, s_scr):
    ci = pl.program_id(1)
    c = GLA_CHUNK

    @pl.when(ci == 0)
    def _():
        s_scr[...] = jnp.zeros_like(s_scr)

    g = _gla_log_decay(glr_ref[...], wgk_ref[...], bgk_ref[...])
    b_all = _seg_cumsum(g, c)
    gn = gn_ref[...]
    scale = GLA_HEAD_K ** -0.5
    for h in range(GLA_HEADS):
        ks, kv = h * GLA_HEAD_K, h * GLA_HEAD_V
        q = qk_ref[:, ks:ks + GLA_HEAD_K].astype(F32) * scale
        k = qk_ref[:, GLA_KEY_DIM + ks:GLA_KEY_DIM + ks + GLA_HEAD_K].astype(F32)
        v_bf = vg_ref[:, kv:kv + GLA_HEAD_V]
        v = v_bf.astype(F32)
        g_out = vg_ref[:, GLA_VAL_DIM + kv:GLA_VAL_DIM + kv + GLA_HEAD_V].astype(F32)
        b = b_all[:, ks:ks + GLA_HEAD_K]
        s = s_scr[h]

        o = _dot((q * jnp.exp(b)).astype(BF16), s.astype(BF16))
        o = o + _gla_block_diag(q, k, v, b)
        att = _gla_levels(q, k, b, c)
        o = o + _dot(att.astype(BF16), v_bf)
        og_ref[:, kv:kv + GLA_HEAD_V] = _gla_out(o, gn, g_out)

        b_t = b.T
        b_last = b_t[:, c - 1:c]
        kd_t = (k.T * jnp.exp(b_last - b_t)).astype(BF16)
        s_scr[h] = jnp.exp(b_last) * s + _dot(kd_t, v_bf)

    @pl.when(ci == pl.num_programs(1) - 1)
    def _():
        sfin_ref[0] = s_scr[...]


def _gla_prompt(qk, vg, glr, wgk, bgk, gn, *, n_seq, seq_len):
    c = GLA_CHUNK
    nc = seq_len // c
    row = lambda w: pl.BlockSpec((c, w), lambda n, i: (n * nc + i, 0))
    return pl.pallas_call(
        _gla_prompt_body,
        grid=(n_seq, nc),
        in_specs=[row(1024), row(2048), row(LANES), _resident((LANES, GLA_KEY_DIM)),
                  _resident((1, GLA_KEY_DIM)), _resident((1, GLA_HEAD_V))],
        out_specs=[row(GLA_VAL_DIM),
                   pl.BlockSpec((1, GLA_HEADS, GLA_HEAD_K, GLA_HEAD_V), lambda n, i: (n, 0, 0, 0))],
        out_shape=(jax.ShapeDtypeStruct((n_seq * seq_len, GLA_VAL_DIM), BF16),
                   jax.ShapeDtypeStruct((n_seq, GLA_HEADS, GLA_HEAD_K, GLA_HEAD_V), F32)),
        scratch_shapes=[pltpu.VMEM((GLA_HEADS, GLA_HEAD_K, GLA_HEAD_V), F32)],
        compiler_params=pltpu.CompilerParams(
            dimension_semantics=("arbitrary", "arbitrary"), vmem_limit_bytes=VMEM_LIMIT_BYTES),
        name="gla_prompt",
    )(qk, vg, glr, wgk, bgk, gn)


def _gla_sample_body(qk_ref, vg_ref, glr_ref, s0_ref, wgk_ref, bgk_ref, gn_ref, og_ref, snew_ref,
                     *, seq_len):
    nb = SAMPLE_SEQS_PER_STEP
    c = nb * seq_len
    g = _gla_log_decay(glr_ref[...], wgk_ref[...], bgk_ref[...])
    b_all = _seg_cumsum(g, seq_len)
    gn = gn_ref[...]
    scale = GLA_HEAD_K ** -0.5
    lane_seq = lax.broadcasted_iota(jnp.int32, (GLA_HEAD_K, c), 1) // seq_len
    for h in range(GLA_HEADS):
        ks, kv = h * GLA_HEAD_K, h * GLA_HEAD_V
        q = qk_ref[:, ks:ks + GLA_HEAD_K].astype(F32) * scale
        k = qk_ref[:, GLA_KEY_DIM + ks:GLA_KEY_DIM + ks + GLA_HEAD_K].astype(F32)
        v_bf = vg_ref[:, kv:kv + GLA_HEAD_V]
        v = v_bf.astype(F32)
        g_out = vg_ref[:, GLA_VAL_DIM + kv:GLA_VAL_DIM + kv + GLA_HEAD_V].astype(F32)
        b = b_all[:, ks:ks + GLA_HEAD_K]

        qi = (q * jnp.exp(b)).astype(BF16)
        inter = [
            _dot(qi[n * seq_len:(n + 1) * seq_len, :], s0_ref[n, h].astype(BF16)) for n in range(nb)
        ]
        o = jnp.concatenate(inter, axis=0) + _gla_block_diag(q, k, v, b)
        og_ref[:, kv:kv + GLA_HEAD_V] = _gla_out(o, gn, g_out)

        b_last_rows = _bcast_row(b, seq_len, seq_len - 1)
        kd_t = (k * jnp.exp(b_last_rows - b)).T
        b_t = b.T
        for n in range(nb):
            col = (n + 1) * seq_len - 1
            decay = jnp.exp(b_t[:, col:col + 1])
            kd_n = jnp.where(lane_seq == n, kd_t, 0.0).astype(BF16)
            snew_ref[n, h] = decay * s0_ref[n, h] + _dot(kd_n, v_bf)


def _gla_sample(qk, vg, glr, s0, wgk, bgk, gn, *, seq_len):
    assert seq_len == GLA_BLOCK
    nb = SAMPLE_SEQS_PER_STEP
    n_seq = s0.shape[0]
    c = nb * seq_len
    row = lambda w: pl.BlockSpec((c, w), lambda i: (i, 0))
    st = pl.BlockSpec((nb, GLA_HEADS, GLA_HEAD_K, GLA_HEAD_V), lambda i: (i, 0, 0, 0))
    return pl.pallas_call(
        functools.partial(_gla_sample_body, seq_len=seq_len),
        grid=(n_seq // nb,),
        in_specs=[row(1024), row(2048), row(LANES), st, _resident((LANES, GLA_KEY_DIM)),
                  _resident((1, GLA_KEY_DIM)), _resident((1, GLA_HEAD_V))],
        out_specs=[row(GLA_VAL_DIM), st],
        out_shape=(jax.ShapeDtypeStruct((n_seq * seq_len, GLA_VAL_DIM), BF16),
                   jax.ShapeDtypeStruct(s0.shape, F32)),
        compiler_params=pltpu.CompilerParams(
            dimension_semantics=("arbitrary",), vmem_limit_bytes=VMEM_LIMIT_BYTES),
        name="gla_sample",
    )(qk, vg, glr, s0, wgk, bgk, gn)


def _window_sums(full):
    outs = []
    s = full
    w = 1
    for gi, win in enumerate(POOL_WINDOWS):
        while w < win:
            s = s + pltpu.roll(s, w, 0)
            w *= 2
        outs.append(s[:, gi * POOL_GROUP_DIM:(gi + 1) * POOL_GROUP_DIM])
    return jnp.concatenate(outs, axis=1)


def _pool_counts(pos, n_prev):
    grp = lax.broadcasted_iota(jnp.int32, pos.shape, pos.ndim - 1) // POOL_GROUP_DIM
    win = jnp.zeros(pos.shape, jnp.int32)
    for gi, w in enumerate(POOL_WINDOWS):
        win = jnp.where(grp == gi, w, win)
    return jnp.minimum(win, pos + 1 + n_prev).astype(F32)


def _pool_prompt_body(x_ref, prev_ref, o_ref, *, tiles_per_seq):
    i = pl.program_id(0)
    tp = x_ref.shape[0]
    hist_rows = prev_ref.shape[0]
    first = (i % tiles_per_seq) == 0
    x = x_ref[...]
    prev = jnp.where(first, 0.0, prev_ref[...])
    sums = _window_sums(jnp.concatenate([prev, x], axis=0))[hist_rows:, :]
    pos = _row_iota(tp, POOL_WIDTH) + (i % tiles_per_seq) * tp
    o_ref[...] = (sums / _pool_counts(pos, 0) - x).astype(BF16)


def _pool_prompt(xp, *, seq_len, tp):
    m = xp.shape[0]
    hist_rows = 2 * SUBLANES
    ratio = tp // hist_rows
    return pl.pallas_call(
        functools.partial(_pool_prompt_body, tiles_per_seq=seq_len // tp),
        grid=(m // tp,),
        in_specs=[pl.BlockSpec((tp, POOL_WIDTH), lambda i: (i, 0)),
                  pl.BlockSpec((hist_rows, POOL_WIDTH), lambda i: (jnp.maximum(i * ratio - 1, 0), 0))],
        out_specs=pl.BlockSpec((tp, POOL_WIDTH), lambda i: (i, 0)),
        out_shape=jax.ShapeDtypeStruct((m, POOL_WIDTH), BF16),
        compiler_params=pltpu.CompilerParams(dimension_semantics=("arbitrary",)),
        name="pool_prompt",
    )(xp, xp)


def _pool_sample_body(x_ref, hist_ref, o_ref, nh_ref, *, n_prev):
    nb, t, _ = x_ref.shape
    hr = hist_ref.shape[1]
    x = x_ref[...]
    full = jnp.concatenate([hist_ref[...], x], axis=1)
    sums = _window_sums(full.reshape(nb * (hr + t), POOL_WIDTH)).reshape(nb, hr + t, POOL_WIDTH)
    pos = lax.broadcasted_iota(jnp.int32, (nb, t, POOL_WIDTH), 1)
    o_ref[...] = (sums[:, hr:, :] / _pool_counts(pos, n_prev) - x).astype(BF16)
    nh_ref[...] = full[:, t:, :]


def _pool_sample(xp3, hist16, *, n_prev, nb):
    n, t, _ = xp3.shape
    hr = hist16.shape[1]
    blk = lambda r: pl.BlockSpec((nb, r, POOL_WIDTH), lambda i: (i, 0, 0))
    return pl.pallas_call(
        functools.partial(_pool_sample_body, n_prev=n_prev),
        grid=(n // nb,),
        in_specs=[blk(t), blk(hr)],
        out_specs=[blk(t), blk(hr)],
        out_shape=(jax.ShapeDtypeStruct((n, t, POOL_WIDTH), BF16),
                   jax.ShapeDtypeStruct((n, hr, POOL_WIDTH), F32)),
        compiler_params=pltpu.CompilerParams(dimension_semantics=("arbitrary",)),
        name="pool_sample",
    )(xp3, hist16)


def _merge_ffn_body(h_ref, og_ref, pl_ref, u_ref, vv_ref, gt_ref,
                    wmix_ref, pscale_ref, wsg_ref, bsg_ref, wa_ref, wb_ref, wc_ref, wo_ref,
                    nf_ref, w1_ref, w2_ref, nfin_ref, o_ref, *, final_norm):
    tm = h_ref.shape[0]
    br = gt_ref[:, 0:D_MODEL].astype(F32) * _dot(og_ref[...], wa_ref[...])
    mixed = jnp.concatenate(
        [_dot(pl_ref[:, g * POOL_GROUP_DIM:(g + 1) * POOL_GROUP_DIM], wmix_ref[g])
         for g in range(POOL_GROUPS)], axis=1)
    pool_out = (mixed * pscale_ref[...]).astype(BF16)
    br = br + gt_ref[:, D_MODEL:2 * D_MODEL].astype(F32) * _dot(pool_out, wb_ref[...])
    rows = []
    for c in range(tm // SG_CHUNK):
        r0 = c * SG_CHUNK
        vv = vv_ref[r0:r0 + SG_CHUNK, :].astype(BF16)
        sg = jnp.concatenate(
            [_dot(wsg_ref[g], vv[:, g * SG_GROUP_DIM:(g + 1) * SG_GROUP_DIM]) for g in range(SG_GROUPS)],
            axis=1) + bsg_ref[...]
        rows.append((u_ref[r0:r0 + SG_CHUNK, :].astype(F32) * sg).astype(BF16))
    sg_out = jnp.concatenate(rows, axis=0)
    br = br + gt_ref[:, 2 * D_MODEL:3 * D_MODEL].astype(F32) * _dot(sg_out, wc_ref[...])
    h = h_ref[...] + _dot(br.astype(BF16), wo_ref[...])
    a = jnp.maximum(_dot(_rms(h, nf_ref[...]).astype(BF16), w1_ref[...]), 0.0)
    h = h + _dot((a * a).astype(BF16), w2_ref[...])
    if final_norm:
        h = _rms(h, nfin_ref[...])
    o_ref[...] = h


def _merge_ffn(h, og, pooled, u, vv, gates, wts, *, tm, final_norm):
    m = h.shape[0]
    row = lambda w: pl.BlockSpec((tm, w), lambda i: (i, 0))
    (wmix, pscale, wsg, bsg, wa, wb, wc, wo, nf, w1, w2, nfin) = wts
    return pl.pallas_call(
        functools.partial(_merge_ffn_body, final_norm=final_norm),
        grid=(m // tm,),
        in_specs=[row(D_MODEL), row(GLA_VAL_DIM), row(POOL_WIDTH), row(SG_WIDTH), row(SG_WIDTH),
                  row(N_BRANCHES * D_MODEL)] + [_resident(w.shape) for w in wts],
        out_specs=row(D_MODEL),
        out_shape=jax.ShapeDtypeStruct((m, D_MODEL), F32),
        compiler_params=pltpu.CompilerParams(
            dimension_semantics=("arbitrary",), vmem_limit_bytes=VMEM_LIMIT_BYTES),
        name="merge_ffn",
    )(h, og, pooled, u, vv, gates, *wts)


def _reorder_w_in(w):
    glr0 = 2 * GLA_KEY_DIM + 2 * GLA_VAL_DIM
    glr1 = glr0 + GLA_GATE_RANK
    pad = jnp.zeros((w.shape[0], LANES - GLA_GATE_RANK), w.dtype)
    return jnp.concatenate([w[:, :glr0], w[:, glr1:], w[:, glr0:glr1], pad], axis=1).astype(BF16)


def _sg_operands(w_spatial, b_spatial, length):
    reps = SG_CHUNK // length
    t = jnp.arange(SG_CHUNK)
    w = jnp.tile(w_spatial[:, :length, :length], (1, reps, reps))
    keep = ((t[:, None] // length) == (t[None, :] // length)) & ((t[None, :] % length) <= (t[:, None] % length))
    w = jnp.where(keep[None], w, 0.0).astype(BF16)
    bias = jnp.repeat(jnp.tile(b_spatial[:, :length], (1, reps)).T, SG_GROUP_DIM, axis=1)
    return w, bias.astype(F32)


def kernel(x_prompt, x_sample, state_gla, state_pool, norm_mix, w_in, w_gk2, b_gk, gla_norm, w_pool_mix,
           pool_scale, w_spatial, b_spatial, w_br_a, w_br_b, w_br_c, w_out, norm_ffn, w_ff1, w_ff2,
           norm_final):
    depth = w_in.shape[0]
    n_p, t_p, _ = x_prompt.shape
    n_s, t_s, _ = x_sample.shape
    hp = x_prompt.reshape(n_p * t_p, D_MODEL)
    hs = x_sample.reshape(n_s * t_s, D_MODEL)
    n_prev_s = POOL_HIST
    row = lambda a: a.reshape(1, -1).astype(F32)
    nfin = row(norm_final)

    gla_p, gla_s, pool_p, pool_s, sgv_s = [], [], [], [], []
    for l in range(depth):
        w_r = _reorder_w_in(w_in[l])
        wgk = jnp.pad(w_gk2[l], ((0, LANES - GLA_GATE_RANK), (0, 0))).astype(BF16)
        bgk, gn, nmix = row(b_gk[l]), row(gla_norm[l]), row(norm_mix[l])
        common = (w_pool_mix[l].astype(BF16), row(pool_scale[l]))
        tail = (w_br_a[l].astype(BF16), w_br_b[l].astype(BF16), w_br_c[l].astype(BF16),
                w_out[l].astype(BF16), row(norm_ffn[l]), w_ff1[l].astype(BF16), w_ff2[l].astype(BF16), nfin)
        final = l == depth - 1

        qk, vg, xp, u, vv, gates, glr = _inproj(hp, nmix, w_r, tm=256, vv_dtype=BF16)
        og, s_fin = _gla_prompt(qk, vg, glr, wgk, bgk, gn, n_seq=n_p, seq_len=t_p)
        pooled = _pool_prompt(xp, seq_len=t_p, tp=512)
        wts = common + _sg_operands(w_spatial[l], b_spatial[l], min(t_p, SG_CHUNK)) + tail
        hp = _merge_ffn(hp, og, pooled, u, vv, gates, wts, tm=256, final_norm=final)
        gla_p.append(s_fin)
        pool_p.append(xp.reshape(n_p, t_p, POOL_WIDTH)[:, t_p - POOL_HIST:, :])

        qk, vg, xp, u, vv, gates, glr = _inproj(hs, nmix, w_r, tm=256, vv_dtype=F32)
        og, s_new = _gla_sample(qk, vg, glr, state_gla[l], wgk, bgk, gn, seq_len=t_s)
        hist16 = jnp.pad(state_pool[l], ((0, 0), (1, 0), (0, 0)))
        pooled, new_hist = _pool_sample(xp.reshape(n_s, t_s, POOL_WIDTH), hist16, n_prev=n_prev_s, nb=16)
        wts = common + _sg_operands(w_spatial[l], b_spatial[l], min(t_s, SG_CHUNK)) + tail
        hs = _merge_ffn(hs, og, pooled.reshape(n_s * t_s, POOL_WIDTH), u, vv, gates, wts, tm=256,
                        final_norm=final)
        gla_s.append(s_new)
        pool_s.append(new_hist[:, 1:, :])
        sgv_s.append(vv.reshape(n_s, t_s, SG_WIDTH))

    return (hp.reshape(n_p, t_p, D_MODEL), hs.reshape(n_s, t_s, D_MODEL), jnp.stack(gla_p),
            jnp.stack(gla_s), jnp.stack(pool_p), jnp.stack(pool_s), jnp.stack(sgv_s))
```

```python
import functools
import math

import jax
import jax.numpy as jnp
import numpy as np
from jax import lax
from jax.experimental import pallas as pl
from jax.experimental.pallas import tpu as pltpu

F32 = jnp.float32
BF16 = jnp.bfloat16

D_MODEL = 1024
GLA_HEADS = 4
GLA_KEY_DIM = 512
GLA_VAL_DIM = 1024
GLA_HEAD_K = 128
GLA_HEAD_V = 256
GLA_GATE_RANK = 16
GLA_GATE_NORMALIZER = 16.0
POOL_WIDTH = 512
POOL_WINDOWS = (2, 4, 8, 16)
POOL_GROUPS = 4
POOL_GROUP_DIM = 128
POOL_HIST = 15
SG_WIDTH = 512
SG_GROUPS = 4
SG_GROUP_DIM = 128
SG_CHUNK = 128
N_BRANCHES = 3
D_FF = 4096
EPS = 1e-6

SUBLANES = 8
LANES = 128
VMEM_LIMIT_BYTES = 56 * 1024 * 1024

C_QK = (0, 1024)
C_V = (1024, 2048)
C_GOUT = (2048, 3072)
C_XP = (3072, 3584)
C_U = (3584, 4096)
C_VV = (4096, 4608)
C_GATES = (4608, 7680)
C_GLR = (7680, 7808)
IN_COLS_R = 7808

GLA_CHUNK = 128
GLA_BLOCK = SUBLANES
SAMPLE_SEQS_PER_STEP = 16
LOG2_DECAY_SCALE = math.log2(math.e) / GLA_GATE_NORMALIZER


def _resident(shape, layer=None):
    nd = len(shape)
    if layer is None:
        return pl.BlockSpec(shape, lambda *_: (0,) * nd, pipeline_mode=pl.Buffered(1))
    return pl.BlockSpec((None,) + tuple(shape), lambda *_: (layer,) + (0,) * nd,
                        pipeline_mode=pl.Buffered(1))


def _rms(x, w):
    return x * lax.rsqrt(jnp.mean(x * x, axis=-1, keepdims=True) + EPS) * w


def _dot(a, b):
    return jnp.dot(a, b, preferred_element_type=F32)


def _dot_nt(a, b):
    return lax.dot_general(a, b, (((1,), (1,)), ((), ())), preferred_element_type=F32)


def _row_iota(c, w):
    return lax.broadcasted_iota(jnp.int32, (c, w), 0)


def _bcast_row(x, blk, r):
    c, w = x.shape
    x3 = x.reshape(c // blk, blk, w)
    return jnp.broadcast_to(x3[:, r:r + 1, :], x3.shape).reshape(c, w)


def _seg_cumsum(g, seg):
    c, w = g.shape
    tpos = _row_iota(c, w) % SUBLANES
    for sh in (1, 2, 4):
        g = g + jnp.where(tpos >= sh, pltpu.roll(g, sh, 0), 0.0)
    if seg > SUBLANES:
        nb = seg // SUBLANES
        y = _bcast_row(g, SUBLANES, SUBLANES - 1).reshape(c // seg, nb, SUBLANES, w)
        sh = 1
        while sh < nb:
            y = y + jnp.concatenate([jnp.zeros((c // seg, sh, SUBLANES, w), F32), y[:, :nb - sh]], axis=1)
            sh *= 2
        excl = jnp.concatenate([jnp.zeros((c // seg, 1, SUBLANES, w), F32), y[:, :nb - 1]], axis=1)
        g = g + excl.reshape(c, w)
    return g


def _inproj_body(h_ref, nw_ref, w_ref, wgk_ref, bgk_ref,
                 qk_ref, vg_ref, xp_ref, u_ref, vv_ref, gt_ref, b2_ref, *, seg):
    xn = _rms(h_ref[...], nw_ref[...]).astype(BF16)

    def proj(lo, hi):
        return _dot(xn, w_ref[:, lo:hi])

    qk_ref[...] = proj(*C_QK).astype(BF16)
    vg_ref[:, :GLA_VAL_DIM] = proj(*C_V).astype(BF16)
    vg_ref[:, GLA_VAL_DIM:] = jax.nn.silu(proj(*C_GOUT)).astype(BF16)
    xp_ref[...] = proj(*C_XP)
    u_ref[...] = jax.nn.gelu(proj(*C_U)).astype(u_ref.dtype)
    vv_ref[...] = jax.nn.gelu(proj(*C_VV)).astype(vv_ref.dtype)
    for j in range(N_BRANCHES):
        lo = C_GATES[0] + j * D_MODEL
        gt_ref[:, j * D_MODEL:(j + 1) * D_MODEL] = jax.nn.sigmoid(proj(lo, lo + D_MODEL)).astype(BF16)
    g2 = jax.nn.log_sigmoid(_dot(proj(*C_GLR).astype(BF16), wgk_ref[...]) + bgk_ref[...]) * LOG2_DECAY_SCALE
    b2_ref[...] = _seg_cumsum(g2, seg)


def _inproj(h, norm_w, w_r, wgk, bgk, layer, *, tm, seg, vv_dtype):
    m = h.shape[0]
    row = lambda w: pl.BlockSpec((tm, w), lambda i: (i, 0))
    out_shape = (
        jax.ShapeDtypeStruct((m, 2 * GLA_KEY_DIM), BF16),
        jax.ShapeDtypeStruct((m, 2 * GLA_VAL_DIM), BF16),
        jax.ShapeDtypeStruct((m, POOL_WIDTH), F32),
        jax.ShapeDtypeStruct((m, SG_WIDTH), BF16),
        jax.ShapeDtypeStruct((m, SG_WIDTH), vv_dtype),
        jax.ShapeDtypeStruct((m, N_BRANCHES * D_MODEL), BF16),
        jax.ShapeDtypeStruct((m, GLA_KEY_DIM), F32),
    )
    return pl.pallas_call(
        functools.partial(_inproj_body, seg=seg),
        grid=(m // tm,),
        in_specs=[row(D_MODEL), _resident((1, D_MODEL), layer), _resident((D_MODEL, IN_COLS_R), layer),
                  _resident((LANES, GLA_KEY_DIM), layer), _resident((1, GLA_KEY_DIM), layer)],
        out_specs=[row(2 * GLA_KEY_DIM), row(2 * GLA_VAL_DIM), row(POOL_WIDTH), row(SG_WIDTH), row(SG_WIDTH),
                   row(N_BRANCHES * D_MODEL), row(GLA_KEY_DIM)],
        out_shape=out_shape,
        compiler_params=pltpu.CompilerParams(
            dimension_semantics=("arbitrary",), vmem_limit_bytes=VMEM_LIMIT_BYTES),
        name="inproj",
    )(h, norm_w, w_r, wgk, bgk)


def _score_codes(c, levels):
    t = np.arange(c)[:, None]
    s = np.arange(c)[None, :]
    code = np.zeros((c, c), np.int32)
    same = (t // GLA_BLOCK == s // GLA_BLOCK) & (s <= t)
    code = np.where(same, 1 + s % GLA_BLOCK, code)
    if levels:
        lvl, j = 2 * GLA_BLOCK, 0
        while lvl <= c:
            hit = (t // lvl == s // lvl) & (t // (lvl // 2) != s // (lvl // 2)) & (s < t)
            code = np.where(hit, 1 + GLA_BLOCK + j, code)
            lvl, j = lvl * 2, j + 1
    return jnp.asarray(code, jnp.int32)


def _rows_bcast(ref, rows, lo, reps):
    parts = []
    for r in rows:
        tile = jnp.broadcast_to(ref[pl.ds(r, 1), lo:lo + GLA_HEAD_K], (SUBLANES, GLA_HEAD_K))
        parts.extend([tile] * reps)
    return jnp.concatenate(parts, axis=0)


def _gla_scores(q, k, b, b_ref, kf_ref, ks, code, c, levels):
    att = jnp.zeros((c, c), F32)
    if levels:
        lvl, j = 2 * GLA_BLOCK, 0
        while lvl <= c:
            half = lvl // 2
            bmid = _rows_bcast(b_ref, [i * lvl + half - 1 for i in range(c // lvl)], ks, lvl // SUBLANES)
            e = jnp.exp2(-jnp.abs(b - bmid))
            att_l = _dot_nt((q * e).astype(BF16), (k * e).astype(BF16))
            att = jnp.where(code == 1 + GLA_BLOCK + j, att_l, att)
            lvl, j = lvl * 2, j + 1
    nblk = c // GLA_BLOCK
    for r in range(GLA_BLOCK):
        rows = [i * GLA_BLOCK + r for i in range(nblk)]
        br = _rows_bcast(b_ref, rows, ks, 1)
        kr = _rows_bcast(kf_ref, rows, ks, 1)
        a = jnp.sum(q * kr * jnp.exp2(b - br), axis=-1, keepdims=True)
        att = jnp.where(code == 1 + r, a, att)
    return att


def _gla_head_inputs(qk_ref, vg_ref, b_ref, h):
    ks, kv = h * GLA_HEAD_K, h * GLA_HEAD_V
    q = qk_ref[:, ks:ks + GLA_HEAD_K].astype(F32) * (GLA_HEAD_K ** -0.5)
    k = qk_ref[:, GLA_KEY_DIM + ks:GLA_KEY_DIM + ks + GLA_HEAD_K].astype(F32)
    v_bf = vg_ref[:, kv:kv + GLA_HEAD_V]
    gate = vg_ref[:, GLA_VAL_DIM + kv:GLA_VAL_DIM + kv + GLA_HEAD_V].astype(F32)
    b = b_ref[:, ks:ks + GLA_HEAD_K]
    return ks, kv, q, k, v_bf, gate, b


def _gla_prompt_body(qk_ref, vg_ref, b_ref, code_ref, gn_ref, og_ref, sfin_ref, s_scr, kf_scr):
    ci = pl.program_id(1)
    c = GLA_CHUNK

    @pl.when(ci == 0)
    def _():
        s_scr[...] = jnp.zeros_like(s_scr)

    kf_scr[...] = qk_ref[:, GLA_KEY_DIM:].astype(F32)
    code = code_ref[...]
    gn = gn_ref[...]
    for h in range(GLA_HEADS):
        ks, kv, q, k, v_bf, gate, b = _gla_head_inputs(qk_ref, vg_ref, b_ref, h)
        s = s_scr[h]
        o = _dot((q * jnp.exp2(b)).astype(BF16), s.astype(BF16))
        att = _gla_scores(q, k, b, b_ref, kf_scr, ks, code, c, True)
        o = o + _dot(att.astype(BF16), v_bf)
        og_ref[:, kv:kv + GLA_HEAD_V] = (_rms(o, gn) * gate).astype(BF16)

        b_t = b.T
        b_last = b_t[:, c - 1:c]
        kd_t = (k.T * jnp.exp2(b_last - b_t)).astype(BF16)
        s_scr[h] = jnp.exp2(b_last) * s + _dot(kd_t, v_bf)

    @pl.when(ci == pl.num_programs(1) - 1)
    def _():
        sfin_ref[0] = s_scr[...]


def _gla_prompt(qk, vg, b2, gn, layer, *, n_seq, seq_len):
    c = GLA_CHUNK
    nc = seq_len // c
    row = lambda w: pl.BlockSpec((c, w), lambda n, i: (n * nc + i, 0))
    return pl.pallas_call(
        _gla_prompt_body,
        grid=(n_seq, nc),
        in_specs=[row(2 * GLA_KEY_DIM), row(2 * GLA_VAL_DIM), row(GLA_KEY_DIM), _resident((c, c)),
                  _resident((1, GLA_HEAD_V), layer)],
        out_specs=[row(GLA_VAL_DIM),
                   pl.BlockSpec((1, GLA_HEADS, GLA_HEAD_K, GLA_HEAD_V), lambda n, i: (n, 0, 0, 0))],
        out_shape=(jax.ShapeDtypeStruct((n_seq * seq_len, GLA_VAL_DIM), BF16),
                   jax.ShapeDtypeStruct((n_seq, GLA_HEADS, GLA_HEAD_K, GLA_HEAD_V), F32)),
        scratch_shapes=[pltpu.VMEM((GLA_HEADS, GLA_HEAD_K, GLA_HEAD_V), F32),
                        pltpu.VMEM((c, GLA_KEY_DIM), F32)],
        compiler_params=pltpu.CompilerParams(
            dimension_semantics=("arbitrary", "arbitrary"), vmem_limit_bytes=VMEM_LIMIT_BYTES),
        name="gla_prompt",
    )(qk, vg, b2, _score_codes(c, True), gn)


def _gla_sample_body(qk_ref, vg_ref, b_ref, s0_ref, code_ref, gn_ref, *rest, seq_len):
    og_ref, snew_ref, kf_scr = rest[-3:]
    nb = SAMPLE_SEQS_PER_STEP
    c = nb * seq_len
    kf_scr[...] = qk_ref[:, GLA_KEY_DIM:].astype(F32)
    code = code_ref[...]
    gn = gn_ref[...]
    lane_seq = lax.broadcasted_iota(jnp.int32, (GLA_HEAD_K, c), 1) // seq_len
    last_rows = [(n + 1) * seq_len - 1 for n in range(nb)]
    for h in range(GLA_HEADS):
        ks, kv, q, k, v_bf, gate, b = _gla_head_inputs(qk_ref, vg_ref, b_ref, h)
        qi = q * jnp.exp2(b)
        inter = [
            _dot(qi[n * seq_len:(n + 1) * seq_len, :].astype(BF16), s0_ref[n, h].astype(BF16))
            for n in range(nb)
        ]
        att = _gla_scores(q, k, b, b_ref, kf_scr, ks, code, c, False)
        o = jnp.concatenate(inter, axis=0) + _dot(att.astype(BF16), v_bf)
        og_ref[:, kv:kv + GLA_HEAD_V] = (_rms(o, gn) * gate).astype(BF16)

        b_last_rows = _rows_bcast(b_ref, last_rows, ks, 1)
        kd_t = (k * jnp.exp2(b_last_rows - b)).T
        b_t = b.T
        for n in range(nb):
            col = last_rows[n]
            decay = jnp.exp2(b_t[:, col:col + 1])
            kd_n = jnp.where(lane_seq == n, kd_t, 0.0).astype(BF16)
            snew_ref[n, h] = decay * s0_ref[n, h] + _dot(kd_n, v_bf)


def _gla_sample(qk, vg, b2, state_all, gn, layer, prev_new_state, *, seq_len):
    assert seq_len == GLA_BLOCK
    nb = SAMPLE_SEQS_PER_STEP
    n_seq = state_all.shape[1]
    c = nb * seq_len
    row = lambda w: pl.BlockSpec((c, w), lambda i: (i, 0))
    st = pl.BlockSpec((None, nb, GLA_HEADS, GLA_HEAD_K, GLA_HEAD_V), lambda i: (layer, i, 0, 0, 0))
    in_specs = [row(2 * GLA_KEY_DIM), row(2 * GLA_VAL_DIM), row(GLA_KEY_DIM), st, _resident((c, c)),
                _resident((1, GLA_HEAD_V), layer)]
    args = [qk, vg, b2, state_all, _score_codes(c, False), gn]
    aliases = {}
    if prev_new_state is not None:
        in_specs.append(pl.BlockSpec(memory_space=pl.ANY))
        args.append(prev_new_state)
        aliases = {len(args) - 1: 1}
    return pl.pallas_call(
        functools.partial(_gla_sample_body, seq_len=seq_len),
        grid=(n_seq // nb,),
        in_specs=in_specs,
        out_specs=[row(GLA_VAL_DIM), st],
        out_shape=(jax.ShapeDtypeStruct((n_seq * seq_len, GLA_VAL_DIM), BF16),
                   jax.ShapeDtypeStruct(state_all.shape, F32)),
        scratch_shapes=[pltpu.VMEM((c, GLA_KEY_DIM), F32)],
        input_output_aliases=aliases,
        compiler_params=pltpu.CompilerParams(
            dimension_semantics=("arbitrary",), vmem_limit_bytes=VMEM_LIMIT_BYTES),
        name="gla_sample",
    )(*args)


def _window_sums(full):
    outs = []
    s = full
    w = 1
    for gi, win in enumerate(POOL_WINDOWS):
        while w < win:
            s = s + pltpu.roll(s, w, 0)
            w *= 2
        outs.append(s[:, gi * POOL_GROUP_DIM:(gi + 1) * POOL_GROUP_DIM])
    return jnp.concatenate(outs, axis=1)


def _pool_counts(pos, n_prev):
    grp = lax.broadcasted_iota(jnp.int32, pos.shape, pos.ndim - 1) // POOL_GROUP_DIM
    win = jnp.zeros(pos.shape, jnp.int32)
    for gi, w in enumerate(POOL_WINDOWS):
        win = jnp.where(grp == gi, w, win)
    return jnp.minimum(win, pos + 1 + n_prev).astype(F32)


def _pool_prompt_body(x_ref, prev_ref, o_ref, *, tiles_per_seq):
    i = pl.program_id(0)
    tp = x_ref.shape[0]
    hist_rows = prev_ref.shape[0]
    first = (i % tiles_per_seq) == 0
    x = x_ref[...]
    prev = jnp.where(first, 0.0, prev_ref[...])
    sums = _window_sums(jnp.concatenate([prev, x], axis=0))[hist_rows:, :]
    pos = _row_iota(tp, POOL_WIDTH) + (i % tiles_per_seq) * tp
    o_ref[...] = (sums / _pool_counts(pos, 0) - x).astype(BF16)


def _pool_prompt(xp, *, seq_len, tp):
    m = xp.shape[0]
    hist_rows = 2 * SUBLANES
    ratio = tp // hist_rows
    return pl.pallas_call(
        functools.partial(_pool_prompt_body, tiles_per_seq=seq_len // tp),
        grid=(m // tp,),
        in_specs=[pl.BlockSpec((tp, POOL_WIDTH), lambda i: (i, 0)),
                  pl.BlockSpec((hist_rows, POOL_WIDTH), lambda i: (jnp.maximum(i * ratio - 1, 0), 0))],
        out_specs=pl.BlockSpec((tp, POOL_WIDTH), lambda i: (i, 0)),
        out_shape=jax.ShapeDtypeStruct((m, POOL_WIDTH), BF16),
        compiler_params=pltpu.CompilerParams(dimension_semantics=("arbitrary",)),
        name="pool_prompt",
    )(xp, xp)


def _pool_sample_body(x_ref, hist_ref, o_ref, nh_ref, *, n_prev):
    nb, t, _ = x_ref.shape
    hr = hist_ref.shape[1]
    x = x_ref[...]
    full = jnp.concatenate([hist_ref[...], x], axis=1)
    sums = _window_sums(full.reshape(nb * (hr + t), POOL_WIDTH)).reshape(nb, hr + t, POOL_WIDTH)
    pos = lax.broadcasted_iota(jnp.int32, (nb, t, POOL_WIDTH), 1)
    o_ref[...] = (sums[:, hr:, :] / _pool_counts(pos, n_prev) - x).astype(BF16)
    nh_ref[...] = full[:, t:, :]


def _pool_sample(xp3, hist16, *, n_prev, nb):
    n, t, _ = xp3.shape
    hr = hist16.shape[1]
    blk = lambda r: pl.BlockSpec((nb, r, POOL_WIDTH), lambda i: (i, 0, 0))
    return pl.pallas_call(
        functools.partial(_pool_sample_body, n_prev=n_prev),
        grid=(n // nb,),
        in_specs=[blk(t), blk(hr)],
        out_specs=[blk(t), blk(hr)],
        out_shape=(jax.ShapeDtypeStruct((n, t, POOL_WIDTH), BF16),
                   jax.ShapeDtypeStruct((n, hr, POOL_WIDTH), F32)),
        compiler_params=pltpu.CompilerParams(dimension_semantics=("arbitrary",)),
        name="pool_sample",
    )(xp3, hist16)


def _merge_ffn_body(h_ref, og_ref, pl_ref, u_ref, vv_ref, gt_ref,
                    wmix_ref, pscale_ref, wsg_ref, bsg_ref, wa_ref, wb_ref, wc_ref, wo_ref,
                    nf_ref, w1_ref, w2_ref, nfin_ref, o_ref, *, final_norm):
    tm = h_ref.shape[0]
    br = gt_ref[:, 0:D_MODEL].astype(F32) * _dot(og_ref[...], wa_ref[...])
    mixed = jnp.concatenate(
        [_dot(pl_ref[:, g * POOL_GROUP_DIM:(g + 1) * POOL_GROUP_DIM], wmix_ref[g])
         for g in range(POOL_GROUPS)], axis=1)
    pool_out = (mixed * pscale_ref[...]).astype(BF16)
    br = br + gt_ref[:, D_MODEL:2 * D_MODEL].astype(F32) * _dot(pool_out, wb_ref[...])
    rows = []
    for c in range(tm // SG_CHUNK):
        r0 = c * SG_CHUNK
        vv = vv_ref[r0:r0 + SG_CHUNK, :].astype(BF16)
        sg = jnp.concatenate(
            [_dot(wsg_ref[g], vv[:, g * SG_GROUP_DIM:(g + 1) * SG_GROUP_DIM]) for g in range(SG_GROUPS)],
            axis=1) + bsg_ref[...]
        rows.append((u_ref[r0:r0 + SG_CHUNK, :].astype(F32) * sg).astype(BF16))
    sg_out = jnp.concatenate(rows, axis=0)
    br = br + gt_ref[:, 2 * D_MODEL:3 * D_MODEL].astype(F32) * _dot(sg_out, wc_ref[...])
    h = h_ref[...] + _dot(br.astype(BF16), wo_ref[...])
    a = jnp.maximum(_dot(_rms(h, nf_ref[...]).astype(BF16), w1_ref[...]), 0.0)
    h = h + _dot((a * a).astype(BF16), w2_ref[...])
    if final_norm:
        h = _rms(h, nfin_ref[...])
    o_ref[...] = h


def _merge_ffn(h, og, pooled, u, vv, gates, layer_wts, sg_wts, nfin, layer, *, tm, final_norm):
    m = h.shape[0]
    row = lambda w: pl.BlockSpec((tm, w), lambda i: (i, 0))
    (wmix, pscale, wa, wb, wc, wo, nf, w1, w2) = layer_wts
    wsg, bsg = sg_wts
    lw = lambda w: _resident(w.shape[1:], layer)
    return pl.pallas_call(
        functools.partial(_merge_ffn_body, final_norm=final_norm),
        grid=(m // tm,),
        in_specs=[row(D_MODEL), row(GLA_VAL_DIM), row(POOL_WIDTH), row(SG_WIDTH), row(SG_WIDTH),
                  row(N_BRANCHES * D_MODEL),
                  lw(wmix), lw(pscale), lw(wsg), lw(bsg), lw(wa), lw(wb), lw(wc), lw(wo), lw(nf), lw(w1), lw(w2),
                  _resident(nfin.shape)],
        out_specs=row(D_MODEL),
        out_shape=jax.ShapeDtypeStruct((m, D_MODEL), F32),
        compiler_params=pltpu.CompilerParams(
            dimension_semantics=("arbitrary",), vmem_limit_bytes=VMEM_LIMIT_BYTES),
        name="merge_ffn",
    )(h, og, pooled, u, vv, gates, wmix, pscale, wsg, bsg, wa, wb, wc, wo, nf, w1, w2, nfin)


def _reorder_w_in(w):
    glr0 = 2 * GLA_KEY_DIM + 2 * GLA_VAL_DIM
    glr1 = glr0 + GLA_GATE_RANK
    pad = jnp.zeros(w.shape[:-1] + (LANES - GLA_GATE_RANK,), w.dtype)
    return jnp.concatenate([w[..., :glr0], w[..., glr1:], w[..., glr0:glr1], pad], axis=-1).astype(BF16)


def _sg_operands(w_spatial, b_spatial, length):
    reps = SG_CHUNK // length
    t = jnp.arange(SG_CHUNK)
    w = jnp.tile(w_spatial[:, :, :length, :length], (1, 1, reps, reps))
    keep = ((t[:, None] // length) == (t[None, :] // length)) & ((t[None, :] % length) <= (t[:, None] % length))
    w = jnp.where(keep, w, 0.0).astype(BF16)
    bias = jnp.repeat(jnp.swapaxes(jnp.tile(b_spatial[:, :, :length], (1, 1, reps)), 1, 2), SG_GROUP_DIM, axis=2)
    return w, bias.astype(F32)


def kernel(x_prompt, x_sample, state_gla, state_pool, norm_mix, w_in, w_gk2, b_gk, gla_norm, w_pool_mix,
           pool_scale, w_spatial, b_spatial, w_br_a, w_br_b, w_br_c, w_out, norm_ffn, w_ff1, w_ff2,
           norm_final):
    depth = w_in.shape[0]
    n_p, t_p, _ = x_prompt.shape
    n_s, t_s, _ = x_sample.shape
    hp = x_prompt.reshape(n_p * t_p, D_MODEL)
    hs = x_sample.reshape(n_s * t_s, D_MODEL)
    n_prev_s = POOL_HIST
    rows = lambda a: a.reshape(depth, 1, -1).astype(F32)
    bf = lambda a: a.astype(BF16)

    w_r = _reorder_w_in(w_in)
    wgk = bf(jnp.pad(w_gk2, ((0, 0), (0, LANES - GLA_GATE_RANK), (0, 0))))
    bgk, gn, nmix = rows(b_gk), rows(gla_norm), rows(norm_mix)
    layer_wts = (bf(w_pool_mix), rows(pool_scale), bf(w_br_a), bf(w_br_b), bf(w_br_c), bf(w_out),
                 rows(norm_ffn), bf(w_ff1), bf(w_ff2))
    sg_p = _sg_operands(w_spatial, b_spatial, min(t_p, SG_CHUNK))
    sg_s = _sg_operands(w_spatial, b_spatial, min(t_s, SG_CHUNK))
    nfin = norm_final.reshape(1, -1).astype(F32)
    hist16 = jnp.pad(state_pool, ((0, 0), (0, 0), (1, 0), (0, 0)))

    gla_p, pool_p, pool_s, sgv_s = [], [], [], []
    gla_s = None
    for l in range(depth):
        final = l == depth - 1
        qk, vg, xp, u, vv, gates, b2 = _inproj(hp, nmix, w_r, wgk, bgk, l, tm=256, seg=GLA_CHUNK, vv_dtype=BF16)
        og, s_fin = _gla_prompt(qk, vg, b2, gn, l, n_seq=n_p, seq_len=t_p)
        pooled = _pool_prompt(xp, seq_len=t_p, tp=512)
        hp = _merge_ffn(hp, og, pooled, u, vv, gates, layer_wts, sg_p, nfin, l, tm=256, final_norm=final)
        gla_p.append(s_fin)
        pool_p.append(xp.reshape(n_p, t_p, POOL_WIDTH)[:, t_p - POOL_HIST:, :])

        qk, vg, xp, u, vv, gates, b2 = _inproj(hs, nmix, w_r, wgk, bgk, l, tm=256, seg=t_s, vv_dtype=F32)
        og, gla_s = _gla_sample(qk, vg, b2, state_gla, gn, l, gla_s, seq_len=t_s)
        pooled, new_hist = _pool_sample(xp.reshape(n_s, t_s, POOL_WIDTH), hist16[l], n_prev=n_prev_s, nb=16)
        hs = _merge_ffn(hs, og, pooled.reshape(n_s * t_s, POOL_WIDTH), u, vv, gates, layer_wts, sg_s, nfin, l,
                        tm=256, final_norm=final)
        pool_s.append(new_hist[:, 1:, :])
        sgv_s.append(vv.reshape(n_s, t_s, SG_WIDTH))

    return (hp.reshape(n_p, t_p, D_MODEL), hs.reshape(n_s, t_s, D_MODEL), jnp.stack(gla_p), gla_s,
            jnp.stack(pool_p), jnp.stack(pool_s), jnp.stack(sgv_s))
```

```python
import functools
import math

import jax
import jax.numpy as jnp
import numpy as np
from jax import lax
from jax.experimental import pallas as pl
from jax.experimental.pallas import tpu as pltpu

F32 = jnp.float32
BF16 = jnp.bfloat16

D_MODEL = 1024
GLA_HEADS = 4
GLA_KEY_DIM = 512
GLA_VAL_DIM = 1024
GLA_HEAD_K = 128
GLA_HEAD_V = 256
GLA_GATE_RANK = 16
GLA_GATE_NORMALIZER = 16.0
POOL_WIDTH = 512
POOL_WINDOWS = (2, 4, 8, 16)
POOL_GROUPS = 4
POOL_GROUP_DIM = 128
POOL_HIST = 15
SG_WIDTH = 512
SG_GROUPS = 4
SG_GROUP_DIM = 128
SG_CHUNK = 128
N_BRANCHES = 3
D_FF = 4096
EPS = 1e-6

SUBLANES = 8
LANES = 128
VMEM_LIMIT_BYTES = 56 * 1024 * 1024

C_QK = (0, 1024)
C_V = (1024, 2048)
C_GOUT = (2048, 3072)
C_XP = (3072, 3584)
C_U = (3584, 4096)
C_VV = (4096, 4608)
C_GATES = (4608, 7680)
C_GLR = (7680, 7808)
IN_COLS_R = 7808

GLA_CHUNK = 128
GLA_BLOCK = SUBLANES
SAMPLE_SEQS_PER_STEP = 16
LOG2_DECAY_SCALE = math.log2(math.e) / GLA_GATE_NORMALIZER


def _resident(shape, layer=None):
    nd = len(shape)
    if layer is None:
        return pl.BlockSpec(shape, lambda *_: (0,) * nd, pipeline_mode=pl.Buffered(1))
    return pl.BlockSpec((None,) + tuple(shape), lambda *_: (layer,) + (0,) * nd,
                        pipeline_mode=pl.Buffered(1))


def _rms(x, w):
    return x * lax.rsqrt(jnp.mean(x * x, axis=-1, keepdims=True) + EPS) * w


def _dot(a, b):
    return jnp.dot(a, b, preferred_element_type=F32)


def _dot_nt(a, b):
    return lax.dot_general(a, b, (((1,), (1,)), ((), ())), preferred_element_type=F32)


def _row_iota(c, w):
    return lax.broadcasted_iota(jnp.int32, (c, w), 0)


def _bcast_row(x, blk, r):
    c, w = x.shape
    x3 = x.reshape(c // blk, blk, w)
    return jnp.broadcast_to(x3[:, r:r + 1, :], x3.shape).reshape(c, w)


def _seg_cumsum(g, seg):
    c, w = g.shape
    tpos = _row_iota(c, w) % SUBLANES
    for sh in (1, 2, 4):
        g = g + jnp.where(tpos >= sh, pltpu.roll(g, sh, 0), 0.0)
    if seg > SUBLANES:
        nb = seg // SUBLANES
        y = _bcast_row(g, SUBLANES, SUBLANES - 1).reshape(c // seg, nb, SUBLANES, w)
        sh = 1
        while sh < nb:
            y = y + jnp.concatenate([jnp.zeros((c // seg, sh, SUBLANES, w), F32), y[:, :nb - sh]], axis=1)
            sh *= 2
        excl = jnp.concatenate([jnp.zeros((c // seg, 1, SUBLANES, w), F32), y[:, :nb - 1]], axis=1)
        g = g + excl.reshape(c, w)
    return g


def _inproj_body(h_ref, nw_ref, w_ref, wgk_ref, bgk_ref,
                 qk_ref, vg_ref, xp_ref, u_ref, vv_ref, gt_ref, b2_ref, *, seg):
    xn = _rms(h_ref[...], nw_ref[...]).astype(BF16)

    def proj(lo, hi):
        return _dot(xn, w_ref[:, lo:hi])

    g2 = jax.nn.log_sigmoid(_dot(proj(*C_GLR).astype(BF16), wgk_ref[...]) + bgk_ref[...]) * LOG2_DECAY_SCALE
    b2_ref[...] = _seg_cumsum(g2, seg)
    vg_ref[:, GLA_VAL_DIM:] = jax.nn.silu(proj(*C_GOUT)).astype(BF16)
    u_ref[...] = jax.nn.gelu(proj(*C_U)).astype(u_ref.dtype)
    vv_ref[...] = jax.nn.gelu(proj(*C_VV)).astype(vv_ref.dtype)
    for j in range(N_BRANCHES):
        lo = C_GATES[0] + j * D_MODEL
        gt_ref[:, j * D_MODEL:(j + 1) * D_MODEL] = jax.nn.sigmoid(proj(lo, lo + D_MODEL)).astype(BF16)
    xp_ref[...] = proj(*C_XP)
    vg_ref[:, :GLA_VAL_DIM] = proj(*C_V).astype(BF16)
    qk_ref[...] = proj(*C_QK).astype(BF16)


def _inproj(h, norm_w, w_r, wgk, bgk, layer, *, tm, seg, vv_dtype):
    m = h.shape[0]
    row = lambda w: pl.BlockSpec((tm, w), lambda i: (i, 0))
    out_shape = (
        jax.ShapeDtypeStruct((m, 2 * GLA_KEY_DIM), BF16),
        jax.ShapeDtypeStruct((m, 2 * GLA_VAL_DIM), BF16),
        jax.ShapeDtypeStruct((m, POOL_WIDTH), F32),
        jax.ShapeDtypeStruct((m, SG_WIDTH), BF16),
        jax.ShapeDtypeStruct((m, SG_WIDTH), vv_dtype),
        jax.ShapeDtypeStruct((m, N_BRANCHES * D_MODEL), BF16),
        jax.ShapeDtypeStruct((m, GLA_KEY_DIM), F32),
    )
    return pl.pallas_call(
        functools.partial(_inproj_body, seg=seg),
        grid=(m // tm,),
        in_specs=[row(D_MODEL), _resident((1, D_MODEL), layer), _resident((D_MODEL, IN_COLS_R), layer),
                  _resident((LANES, GLA_KEY_DIM), layer), _resident((1, GLA_KEY_DIM), layer)],
        out_specs=[row(2 * GLA_KEY_DIM), row(2 * GLA_VAL_DIM), row(POOL_WIDTH), row(SG_WIDTH), row(SG_WIDTH),
                   row(N_BRANCHES * D_MODEL), row(GLA_KEY_DIM)],
        out_shape=out_shape,
        compiler_params=pltpu.CompilerParams(
            dimension_semantics=("arbitrary",), vmem_limit_bytes=VMEM_LIMIT_BYTES),
        name="inproj",
    )(h, norm_w, w_r, wgk, bgk)


CODE_BLOCK = 1


def _score_codes(c, levels):
    t = np.arange(c)[:, None]
    s = np.arange(c)[None, :]
    code = np.zeros((c, c), np.int32)
    code = np.where((t // GLA_BLOCK == s // GLA_BLOCK) & (s <= t), CODE_BLOCK, code)
    if levels:
        lvl, j = 2 * GLA_BLOCK, 0
        while lvl <= c:
            hit = (t // lvl == s // lvl) & (t // (lvl // 2) != s // (lvl // 2)) & (s < t)
            code = np.where(hit, CODE_BLOCK + 1 + j, code)
            lvl, j = lvl * 2, j + 1
    return jnp.asarray(code, jnp.int32)


def _block_sum_matrix(c):
    r = np.arange(GLA_BLOCK * GLA_HEAD_K)[:, None] // GLA_HEAD_K
    s = np.arange(c)[None, :]
    return jnp.asarray(r == s % GLA_BLOCK, BF16)


def _neg_abs(x):
    return lax.bitcast_convert_type(lax.bitcast_convert_type(x, jnp.uint32) | jnp.uint32(0x80000000), F32)


def _rows_bcast(ref, rows, lo, reps):
    parts = []
    for r in rows:
        tile = jnp.broadcast_to(ref[pl.ds(r, 1), lo:lo + GLA_HEAD_K], (SUBLANES, GLA_HEAD_K))
        parts.extend([tile] * reps)
    return jnp.concatenate(parts, axis=0)


def _gla_scores(q, k, b, b_ref, kf_ref, sel_ref, ks, code, c, levels):
    att = jnp.zeros((c, c), F32)
    if levels:
        lvl, j = 2 * GLA_BLOCK, 0
        while lvl <= c:
            half = lvl // 2
            bmid = _rows_bcast(b_ref, [i * lvl + half - 1 for i in range(c // lvl)], ks, lvl // SUBLANES)
            e = jnp.exp2(_neg_abs(b - bmid))
            att_l = _dot_nt((q * e).astype(BF16), (k * e).astype(BF16))
            att = jnp.where(code == CODE_BLOCK + 1 + j, att_l, att)
            lvl, j = lvl * 2, j + 1
    nblk = c // GLA_BLOCK
    parts = []
    for r in range(GLA_BLOCK):
        rows = [i * GLA_BLOCK + r for i in range(nblk)]
        br = _rows_bcast(b_ref, rows, ks, 1)
        kr = _rows_bcast(kf_ref, rows, ks, 1)
        parts.append((q * kr * jnp.exp2(jnp.minimum(b - br, 0.0))).astype(BF16))
    same_block = _dot(jnp.concatenate(parts, axis=1), sel_ref[...])
    return jnp.where(code == CODE_BLOCK, same_block, att)


def _gla_head_inputs(qk_ref, vg_ref, b_ref, h):
    ks, kv = h * GLA_HEAD_K, h * GLA_HEAD_V
    q = qk_ref[:, ks:ks + GLA_HEAD_K].astype(F32) * (GLA_HEAD_K ** -0.5)
    k = qk_ref[:, GLA_KEY_DIM + ks:GLA_KEY_DIM + ks + GLA_HEAD_K].astype(F32)
    v_bf = vg_ref[:, kv:kv + GLA_HEAD_V]
    gate = vg_ref[:, GLA_VAL_DIM + kv:GLA_VAL_DIM + kv + GLA_HEAD_V].astype(F32)
    b = b_ref[:, ks:ks + GLA_HEAD_K]
    return ks, kv, q, k, v_bf, gate, b


def _gla_prompt_body(qk_ref, vg_ref, b_ref, code_ref, sel_ref, gn_ref, og_ref, sfin_ref, s_scr, kf_scr):
    ci = pl.program_id(1)
    c = GLA_CHUNK

    @pl.when(ci == 0)
    def _():
        s_scr[...] = jnp.zeros_like(s_scr)

    kf_scr[...] = qk_ref[:, GLA_KEY_DIM:].astype(F32)
    code = code_ref[...]
    gn = gn_ref[...]
    for h in range(GLA_HEADS):
        ks, kv, q, k, v_bf, gate, b = _gla_head_inputs(qk_ref, vg_ref, b_ref, h)
        s = s_scr[h]
        o = _dot((q * jnp.exp2(b)).astype(BF16), s.astype(BF16))
        att = _gla_scores(q, k, b, b_ref, kf_scr, sel_ref, ks, code, c, True)
        o = o + _dot(att.astype(BF16), v_bf)
        og_ref[:, kv:kv + GLA_HEAD_V] = (_rms(o, gn) * gate).astype(BF16)

        b_t = b.T
        b_last = b_t[:, c - 1:c]
        kd_t = (k.T * jnp.exp2(b_last - b_t)).astype(BF16)
        s_scr[h] = jnp.exp2(b_last) * s + _dot(kd_t, v_bf)

    @pl.when(ci == pl.num_programs(1) - 1)
    def _():
        sfin_ref[0] = s_scr[...]


def _gla_prompt(qk, vg, b2, gn, layer, *, n_seq, seq_len):
    c = GLA_CHUNK
    nc = seq_len // c
    row = lambda w: pl.BlockSpec((c, w), lambda n, i: (n * nc + i, 0))
    return pl.pallas_call(
        _gla_prompt_body,
        grid=(n_seq, nc),
        in_specs=[row(2 * GLA_KEY_DIM), row(2 * GLA_VAL_DIM), row(GLA_KEY_DIM), _resident((c, c)),
                  _resident((GLA_BLOCK * GLA_HEAD_K, c)), _resident((1, GLA_HEAD_V), layer)],
        out_specs=[row(GLA_VAL_DIM),
                   pl.BlockSpec((1, GLA_HEADS, GLA_HEAD_K, GLA_HEAD_V), lambda n, i: (n, 0, 0, 0))],
        out_shape=(jax.ShapeDtypeStruct((n_seq * seq_len, GLA_VAL_DIM), BF16),
                   jax.ShapeDtypeStruct((n_seq, GLA_HEADS, GLA_HEAD_K, GLA_HEAD_V), F32)),
        scratch_shapes=[pltpu.VMEM((GLA_HEADS, GLA_HEAD_K, GLA_HEAD_V), F32),
                        pltpu.VMEM((c, GLA_KEY_DIM), F32)],
        compiler_params=pltpu.CompilerParams(
            dimension_semantics=("arbitrary", "arbitrary"), vmem_limit_bytes=VMEM_LIMIT_BYTES),
        name="gla_prompt",
    )(qk, vg, b2, _score_codes(c, True), _block_sum_matrix(c), gn)


def _gla_sample_body(qk_ref, vg_ref, b_ref, s0_ref, code_ref, sel_ref, gn_ref, *rest, seq_len):
    og_ref, snew_ref, kf_scr = rest[-3:]
    nb = SAMPLE_SEQS_PER_STEP
    c = nb * seq_len
    kf_scr[...] = qk_ref[:, GLA_KEY_DIM:].astype(F32)
    code = code_ref[...]
    gn = gn_ref[...]
    lane_seq = lax.broadcasted_iota(jnp.int32, (GLA_HEAD_K, c), 1) // seq_len
    last_rows = [(n + 1) * seq_len - 1 for n in range(nb)]
    for h in range(GLA_HEADS):
        ks, kv, q, k, v_bf, gate, b = _gla_head_inputs(qk_ref, vg_ref, b_ref, h)
        qi = q * jnp.exp2(b)
        inter = [
            _dot(qi[n * seq_len:(n + 1) * seq_len, :].astype(BF16), s0_ref[n, h].astype(BF16))
            for n in range(nb)
        ]
        att = _gla_scores(q, k, b, b_ref, kf_scr, sel_ref, ks, code, c, False)
        o = jnp.concatenate(inter, axis=0) + _dot(att.astype(BF16), v_bf)
        og_ref[:, kv:kv + GLA_HEAD_V] = (_rms(o, gn) * gate).astype(BF16)

        b_last_rows = _rows_bcast(b_ref, last_rows, ks, 1)
        kd_t = (k * jnp.exp2(b_last_rows - b)).T
        b_t = b.T
        for n in range(nb):
            col = last_rows[n]
            decay = jnp.exp2(b_t[:, col:col + 1])
            kd_n = jnp.where(lane_seq == n, kd_t, 0.0).astype(BF16)
            snew_ref[n, h] = decay * s0_ref[n, h] + _dot(kd_n, v_bf)


def _gla_sample(qk, vg, b2, state_all, gn, layer, prev_new_state, *, seq_len):
    assert seq_len == GLA_BLOCK
    nb = SAMPLE_SEQS_PER_STEP
    n_seq = state_all.shape[1]
    c = nb * seq_len
    row = lambda w: pl.BlockSpec((c, w), lambda i: (i, 0))
    st = pl.BlockSpec((None, nb, GLA_HEADS, GLA_HEAD_K, GLA_HEAD_V), lambda i: (layer, i, 0, 0, 0))
    in_specs = [row(2 * GLA_KEY_DIM), row(2 * GLA_VAL_DIM), row(GLA_KEY_DIM), st, _resident((c, c)),
                _resident((GLA_BLOCK * GLA_HEAD_K, c)), _resident((1, GLA_HEAD_V), layer)]
    args = [qk, vg, b2, state_all, _score_codes(c, False), _block_sum_matrix(c), gn]
    aliases = {}
    if prev_new_state is not None:
        in_specs.append(pl.BlockSpec(memory_space=pl.ANY))
        args.append(prev_new_state)
        aliases = {len(args) - 1: 1}
    return pl.pallas_call(
        functools.partial(_gla_sample_body, seq_len=seq_len),
        grid=(n_seq // nb,),
        in_specs=in_specs,
        out_specs=[row(GLA_VAL_DIM), st],
        out_shape=(jax.ShapeDtypeStruct((n_seq * seq_len, GLA_VAL_DIM), BF16),
                   jax.ShapeDtypeStruct(state_all.shape, F32)),
        scratch_shapes=[pltpu.VMEM((c, GLA_KEY_DIM), F32)],
        input_output_aliases=aliases,
        compiler_params=pltpu.CompilerParams(
            dimension_semantics=("arbitrary",), vmem_limit_bytes=VMEM_LIMIT_BYTES),
        name="gla_sample",
    )(*args)


def _window_sums(full):
    outs = []
    s = full
    w = 1
    for gi, win in enumerate(POOL_WINDOWS):
        while w < win:
            s = s + pltpu.roll(s, w, 0)
            w *= 2
        outs.append(s[:, gi * POOL_GROUP_DIM:(gi + 1) * POOL_GROUP_DIM])
    return jnp.concatenate(outs, axis=1)


def _pool_counts(pos, n_prev):
    grp = lax.broadcasted_iota(jnp.int32, pos.shape, pos.ndim - 1) // POOL_GROUP_DIM
    win = jnp.zeros(pos.shape, jnp.int32)
    for gi, w in enumerate(POOL_WINDOWS):
        win = jnp.where(grp == gi, w, win)
    return jnp.minimum(win, pos + 1 + n_prev).astype(F32)


def _pool_prompt_body(x_ref, prev_ref, o_ref, *, tiles_per_seq):
    i = pl.program_id(0)
    tp = x_ref.shape[0]
    hist_rows = prev_ref.shape[0]
    first = (i % tiles_per_seq) == 0
    x = x_ref[...]
    prev = jnp.where(first, 0.0, prev_ref[...])
    sums = _window_sums(jnp.concatenate([prev, x], axis=0))[hist_rows:, :]
    pos = _row_iota(tp, POOL_WIDTH) + (i % tiles_per_seq) * tp
    o_ref[...] = (sums / _pool_counts(pos, 0) - x).astype(BF16)


def _pool_prompt(xp, *, seq_len, tp):
    m = xp.shape[0]
    hist_rows = 2 * SUBLANES
    ratio = tp // hist_rows
    return pl.pallas_call(
        functools.partial(_pool_prompt_body, tiles_per_seq=seq_len // tp),
        grid=(m // tp,),
        in_specs=[pl.BlockSpec((tp, POOL_WIDTH), lambda i: (i, 0)),
                  pl.BlockSpec((hist_rows, POOL_WIDTH), lambda i: (jnp.maximum(i * ratio - 1, 0), 0))],
        out_specs=pl.BlockSpec((tp, POOL_WIDTH), lambda i: (i, 0)),
        out_shape=jax.ShapeDtypeStruct((m, POOL_WIDTH), BF16),
        compiler_params=pltpu.CompilerParams(dimension_semantics=("arbitrary",)),
        name="pool_prompt",
    )(xp, xp)


def _pool_sample_body(x_ref, hist_ref, o_ref, nh_ref, *, n_prev):
    nb, t, _ = x_ref.shape
    hr = hist_ref.shape[1]
    x = x_ref[...]
    full = jnp.concatenate([hist_ref[...], x], axis=1)
    sums = _window_sums(full.reshape(nb * (hr + t), POOL_WIDTH)).reshape(nb, hr + t, POOL_WIDTH)
    pos = lax.broadcasted_iota(jnp.int32, (nb, t, POOL_WIDTH), 1)
    o_ref[...] = (sums[:, hr:, :] / _pool_counts(pos, n_prev) - x).astype(BF16)
    nh_ref[...] = full[:, t:, :]


def _pool_sample(xp3, hist16, *, n_prev, nb):
    n, t, _ = xp3.shape
    hr = hist16.shape[1]
    blk = lambda r: pl.BlockSpec((nb, r, POOL_WIDTH), lambda i: (i, 0, 0))
    return pl.pallas_call(
        functools.partial(_pool_sample_body, n_prev=n_prev),
        grid=(n // nb,),
        in_specs=[blk(t), blk(hr)],
        out_specs=[blk(t), blk(hr)],
        out_shape=(jax.ShapeDtypeStruct((n, t, POOL_WIDTH), BF16),
                   jax.ShapeDtypeStruct((n, hr, POOL_WIDTH), F32)),
        compiler_params=pltpu.CompilerParams(dimension_semantics=("arbitrary",)),
        name="pool_sample",
    )(xp3, hist16)


def _merge_ffn_body(h_ref, og_ref, pl_ref, u_ref, vv_ref, gt_ref,
                    wmix_ref, pscale_ref, wsg_ref, bsg_ref, wa_ref, wb_ref, wc_ref, wo_ref,
                    nf_ref, w1_ref, w2_ref, nfin_ref, o_ref, *, final_norm):
    tm = h_ref.shape[0]
    br = gt_ref[:, 0:D_MODEL].astype(F32) * _dot(og_ref[...], wa_ref[...])
    mixed = jnp.concatenate(
        [_dot(pl_ref[:, g * POOL_GROUP_DIM:(g + 1) * POOL_GROUP_DIM], wmix_ref[g])
         for g in range(POOL_GROUPS)], axis=1)
    pool_out = (mixed * pscale_ref[...]).astype(BF16)
    br = br + gt_ref[:, D_MODEL:2 * D_MODEL].astype(F32) * _dot(pool_out, wb_ref[...])
    rows = []
    for c in range(tm // SG_CHUNK):
        r0 = c * SG_CHUNK
        vv = vv_ref[r0:r0 + SG_CHUNK, :].astype(BF16)
        sg = jnp.concatenate(
            [_dot(wsg_ref[g], vv[:, g * SG_GROUP_DIM:(g + 1) * SG_GROUP_DIM]) for g in range(SG_GROUPS)],
            axis=1) + bsg_ref[...]
        rows.append((u_ref[r0:r0 + SG_CHUNK, :].astype(F32) * sg).astype(BF16))
    sg_out = jnp.concatenate(rows, axis=0)
    br = br + gt_ref[:, 2 * D_MODEL:3 * D_MODEL].astype(F32) * _dot(sg_out, wc_ref[...])
    h = h_ref[...] + _dot(br.astype(BF16), wo_ref[...])
    a = jnp.maximum(_dot(_rms(h, nf_ref[...]).astype(BF16), w1_ref[...]), 0.0)
    h = h + _dot((a * a).astype(BF16), w2_ref[...])
    if final_norm:
        h = _rms(h, nfin_ref[...])
    o_ref[...] = h


def _merge_ffn(h, og, pooled, u, vv, gates, layer_wts, sg_wts, nfin, layer, *, tm, final_norm):
    m = h.shape[0]
    row = lambda w: pl.BlockSpec((tm, w), lambda i: (i, 0))
    (wmix, pscale, wa, wb, wc, wo, nf, w1, w2) = layer_wts
    wsg, bsg = sg_wts
    lw = lambda w: _resident(w.shape[1:], layer)
    return pl.pallas_call(
        functools.partial(_merge_ffn_body, final_norm=final_norm),
        grid=(m // tm,),
        in_specs=[row(D_MODEL), row(GLA_VAL_DIM), row(POOL_WIDTH), row(SG_WIDTH), row(SG_WIDTH),
                  row(N_BRANCHES * D_MODEL),
                  lw(wmix), lw(pscale), lw(wsg), lw(bsg), lw(wa), lw(wb), lw(wc), lw(wo), lw(nf), lw(w1), lw(w2),
                  _resident(nfin.shape)],
        out_specs=row(D_MODEL),
        out_shape=jax.ShapeDtypeStruct((m, D_MODEL), F32),
        compiler_params=pltpu.CompilerParams(
            dimension_semantics=("arbitrary",), vmem_limit_bytes=VMEM_LIMIT_BYTES),
        name="merge_ffn",
    )(h, og, pooled, u, vv, gates, wmix, pscale, wsg, bsg, wa, wb, wc, wo, nf, w1, w2, nfin)


def _reorder_w_in(w):
    glr0 = 2 * GLA_KEY_DIM + 2 * GLA_VAL_DIM
    glr1 = glr0 + GLA_GATE_RANK
    pad = jnp.zeros(w.shape[:-1] + (LANES - GLA_GATE_RANK,), w.dtype)
    return jnp.concatenate([w[..., :glr0], w[..., glr1:], w[..., glr0:glr1], pad], axis=-1).astype(BF16)


def _sg_operands(w_spatial, b_spatial, length):
    reps = SG_CHUNK // length
    t = jnp.arange(SG_CHUNK)
    w = jnp.tile(w_spatial[:, :, :length, :length], (1, 1, reps, reps))
    keep = ((t[:, None] // length) == (t[None, :] // length)) & ((t[None, :] % length) <= (t[:, None] % length))
    w = jnp.where(keep, w, 0.0).astype(BF16)
    bias = jnp.repeat(jnp.swapaxes(jnp.tile(b_spatial[:, :, :length], (1, 1, reps)), 1, 2), SG_GROUP_DIM, axis=2)
    return w, bias.astype(F32)


def kernel(x_prompt, x_sample, state_gla, state_pool, norm_mix, w_in, w_gk2, b_gk, gla_norm, w_pool_mix,
           pool_scale, w_spatial, b_spatial, w_br_a, w_br_b, w_br_c, w_out, norm_ffn, w_ff1, w_ff2,
           norm_final):
    depth = w_in.shape[0]
    n_p, t_p, _ = x_prompt.shape
    n_s, t_s, _ = x_sample.shape
    hp = x_prompt.reshape(n_p * t_p, D_MODEL)
    hs = x_sample.reshape(n_s * t_s, D_MODEL)
    n_prev_s = POOL_HIST
    rows = lambda a: a.reshape(depth, 1, -1).astype(F32)
    bf = lambda a: a.astype(BF16)

    w_r = _reorder_w_in(w_in)
    wgk = bf(jnp.pad(w_gk2, ((0, 0), (0, LANES - GLA_GATE_RANK), (0, 0))))
    bgk, gn, nmix = rows(b_gk), rows(gla_norm), rows(norm_mix)
    layer_wts = (bf(w_pool_mix), rows(pool_scale), bf(w_br_a), bf(w_br_b), bf(w_br_c), bf(w_out),
                 rows(norm_ffn), bf(w_ff1), bf(w_ff2))
    sg_p = _sg_operands(w_spatial, b_spatial, min(t_p, SG_CHUNK))
    sg_s = _sg_operands(w_spatial, b_spatial, min(t_s, SG_CHUNK))
    nfin = norm_final.reshape(1, -1).astype(F32)
    hist16 = jnp.pad(state_pool, ((0, 0), (0, 0), (1, 0), (0, 0)))

    gla_p, pool_p, pool_s, sgv_s = [], [], [], []
    gla_s = None
    for l in range(depth):
        final = l == depth - 1
        qk, vg, xp, u, vv, gates, b2 = _inproj(hp, nmix, w_r, wgk, bgk, l, tm=256, seg=GLA_CHUNK, vv_dtype=BF16)
        og, s_fin = _gla_prompt(qk, vg, b2, gn, l, n_seq=n_p, seq_len=t_p)
        pooled = _pool_prompt(xp, seq_len=t_p, tp=512)
        hp = _merge_ffn(hp, og, pooled, u, vv, gates, layer_wts, sg_p, nfin, l, tm=256, final_norm=final)
        gla_p.append(s_fin)
        pool_p.append(xp.reshape(n_p, t_p, POOL_WIDTH)[:, t_p - POOL_HIST:, :])

        qk, vg, xp, u, vv, gates, b2 = _inproj(hs, nmix, w_r, wgk, bgk, l, tm=256, seg=t_s, vv_dtype=F32)
        og, gla_s = _gla_sample(qk, vg, b2, state_gla, gn, l, gla_s, seq_len=t_s)
        pooled, new_hist = _pool_sample(xp.reshape(n_s, t_s, POOL_WIDTH), hist16[l], n_prev=n_prev_s, nb=16)
        hs = _merge_ffn(hs, og, pooled.reshape(n_s * t_s, POOL_WIDTH), u, vv, gates, layer_wts, sg_s, nfin, l,
                        tm=256, final_norm=final)
        pool_s.append(new_hist[:, 1:, :])
        sgv_s.append(vv.reshape(n_s, t_s, SG_WIDTH))

    return (hp.reshape(n_p, t_p, D_MODEL), hs.reshape(n_s, t_s, D_MODEL), jnp.stack(gla_p), gla_s,
            jnp.stack(pool_p), jnp.stack(pool_s), jnp.stack(sgv_s))
```

```python
import functools
import math

import jax
import jax.numpy as jnp
import numpy as np
from jax import lax
from jax.experimental import pallas as pl
from jax.experimental.pallas import tpu as pltpu

F32 = jnp.float32
BF16 = jnp.bfloat16

D_MODEL = 1024
GLA_HEADS = 4
GLA_KEY_DIM = 512
GLA_VAL_DIM = 1024
GLA_HEAD_K = 128
GLA_HEAD_V = 256
GLA_GATE_RANK = 16
GLA_GATE_NORMALIZER = 16.0
POOL_WIDTH = 512
POOL_WINDOWS = (2, 4, 8, 16)
POOL_GROUPS = 4
POOL_GROUP_DIM = 128
POOL_HIST = 15
SG_WIDTH = 512
SG_GROUPS = 4
SG_GROUP_DIM = 128
SG_CHUNK = 128
N_BRANCHES = 3
D_FF = 4096
EPS = 1e-6

SUBLANES = 8
LANES = 128
VMEM_LIMIT_BYTES = 56 * 1024 * 1024

C_QK = (0, 1024)
C_V = (1024, 2048)
C_GOUT = (2048, 3072)
C_XP = (3072, 3584)
C_U = (3584, 4096)
C_VV = (4096, 4608)
C_GATES = (4608, 7680)
C_GLR = (7680, 7808)
IN_COLS_R = 7808

GLA_CHUNK = 128
GLA_BLOCK = SUBLANES
SAMPLE_SEQS_PER_STEP = 16
LOG2_DECAY_SCALE = math.log2(math.e) / GLA_GATE_NORMALIZER


def _resident(shape, layer=None):
    nd = len(shape)
    if layer is None:
        return pl.BlockSpec(shape, lambda *_: (0,) * nd, pipeline_mode=pl.Buffered(1))
    return pl.BlockSpec((None,) + tuple(shape), lambda *_: (layer,) + (0,) * nd,
                        pipeline_mode=pl.Buffered(1))


def _rms(x, w):
    return x * lax.rsqrt(jnp.mean(x * x, axis=-1, keepdims=True) + EPS) * w


def _dot(a, b):
    return jnp.dot(a, b, preferred_element_type=F32)


def _dot_nt(a, b):
    return lax.dot_general(a, b, (((1,), (1,)), ((), ())), preferred_element_type=F32)


def _row_iota(c, w):
    return lax.broadcasted_iota(jnp.int32, (c, w), 0)


def _bcast_row(x, blk, r):
    c, w = x.shape
    x3 = x.reshape(c // blk, blk, w)
    return jnp.broadcast_to(x3[:, r:r + 1, :], x3.shape).reshape(c, w)


def _seg_cumsum(g, seg):
    c, w = g.shape
    tpos = _row_iota(c, w) % SUBLANES
    for sh in (1, 2, 4):
        g = g + jnp.where(tpos >= sh, pltpu.roll(g, sh, 0), 0.0)
    if seg > SUBLANES:
        nb = seg // SUBLANES
        y = _bcast_row(g, SUBLANES, SUBLANES - 1).reshape(c // seg, nb, SUBLANES, w)
        sh = 1
        while sh < nb:
            y = y + jnp.concatenate([jnp.zeros((c // seg, sh, SUBLANES, w), F32), y[:, :nb - sh]], axis=1)
            sh *= 2
        excl = jnp.concatenate([jnp.zeros((c // seg, 1, SUBLANES, w), F32), y[:, :nb - 1]], axis=1)
        g = g + excl.reshape(c, w)
    return g


def _inproj_body(h_ref, nw_ref, w_ref, wgk_ref, bgk_ref,
                 qk_ref, vg_ref, xp_ref, u_ref, vv_ref, gt_ref, b2_ref, *, seg):
    xn = _rms(h_ref[...], nw_ref[...]).astype(BF16)

    def proj(lo, hi):
        return _dot(xn, w_ref[:, lo:hi])

    g2 = jax.nn.log_sigmoid(_dot(proj(*C_GLR).astype(BF16), wgk_ref[...]) + bgk_ref[...]) * LOG2_DECAY_SCALE
    b2_ref[...] = _seg_cumsum(g2, seg)
    vg_ref[:, GLA_VAL_DIM:] = jax.nn.silu(proj(*C_GOUT)).astype(BF16)
    u_ref[...] = jax.nn.gelu(proj(*C_U)).astype(u_ref.dtype)
    vv_ref[...] = jax.nn.gelu(proj(*C_VV)).astype(vv_ref.dtype)
    for j in range(N_BRANCHES):
        lo = C_GATES[0] + j * D_MODEL
        gt_ref[:, j * D_MODEL:(j + 1) * D_MODEL] = jax.nn.sigmoid(proj(lo, lo + D_MODEL)).astype(BF16)
    xp_ref[...] = proj(*C_XP)
    vg_ref[:, :GLA_VAL_DIM] = proj(*C_V).astype(BF16)
    qk_ref[...] = proj(*C_QK).astype(BF16)


def _inproj(h, norm_w, w_r, wgk, bgk, layer, *, tm, seg, vv_dtype):
    m = h.shape[0]
    row = lambda w: pl.BlockSpec((tm, w), lambda i: (i, 0))
    out_shape = (
        jax.ShapeDtypeStruct((m, 2 * GLA_KEY_DIM), BF16),
        jax.ShapeDtypeStruct((m, 2 * GLA_VAL_DIM), BF16),
        jax.ShapeDtypeStruct((m, POOL_WIDTH), F32),
        jax.ShapeDtypeStruct((m, SG_WIDTH), BF16),
        jax.ShapeDtypeStruct((m, SG_WIDTH), vv_dtype),
        jax.ShapeDtypeStruct((m, N_BRANCHES * D_MODEL), BF16),
        jax.ShapeDtypeStruct((m, GLA_KEY_DIM), F32),
    )
    return pl.pallas_call(
        functools.partial(_inproj_body, seg=seg),
        grid=(m // tm,),
        in_specs=[row(D_MODEL), _resident((1, D_MODEL), layer), _resident((D_MODEL, IN_COLS_R), layer),
                  _resident((LANES, GLA_KEY_DIM), layer), _resident((1, GLA_KEY_DIM), layer)],
        out_specs=[row(2 * GLA_KEY_DIM), row(2 * GLA_VAL_DIM), row(POOL_WIDTH), row(SG_WIDTH), row(SG_WIDTH),
                   row(N_BRANCHES * D_MODEL), row(GLA_KEY_DIM)],
        out_shape=out_shape,
        compiler_params=pltpu.CompilerParams(
            dimension_semantics=("arbitrary",), vmem_limit_bytes=VMEM_LIMIT_BYTES),
        name="inproj",
    )(h, norm_w, w_r, wgk, bgk)


CODE_BLOCK = 1


def _score_codes(c, levels):
    t = np.arange(c)[:, None]
    s = np.arange(c)[None, :]
    code = np.zeros((c, c), np.int32)
    code = np.where((t // GLA_BLOCK == s // GLA_BLOCK) & (s <= t), CODE_BLOCK, code)
    if levels:
        lvl, j = 2 * GLA_BLOCK, 0
        while lvl <= c:
            hit = (t // lvl == s // lvl) & (t // (lvl // 2) != s // (lvl // 2)) & (s < t)
            code = np.where(hit, CODE_BLOCK + 1 + j, code)
            lvl, j = lvl * 2, j + 1
    return jnp.asarray(code, jnp.int32)


def _block_sum_matrix(c):
    r = np.arange(GLA_BLOCK * GLA_HEAD_K)[:, None] // GLA_HEAD_K
    s = np.arange(c)[None, :]
    return jnp.asarray(r == s % GLA_BLOCK, BF16)


def _level_signs(c):
    t = np.arange(c)[:, None]
    out = []
    lvl = 2 * GLA_BLOCK
    while lvl <= c:
        out.append(np.broadcast_to(np.where(t % lvl >= lvl // 2, 1.0, -1.0), (c, GLA_HEAD_K)))
        lvl *= 2
    return jnp.asarray(np.stack(out), F32)


def _rows_bcast(ref, rows, lo, reps):
    parts = []
    for r in rows:
        tile = jnp.broadcast_to(ref[pl.ds(r, 1), lo:lo + GLA_HEAD_K], (SUBLANES, GLA_HEAD_K))
        parts.extend([tile] * reps)
    return jnp.concatenate(parts, axis=0)


def _gla_scores(q, k, b, b_ref, kf_ref, sel_ref, ks, code, c, sgn_ref=None, r0=0):
    att = jnp.zeros((c, c), F32)
    if sgn_ref is not None:
        lvl, j = 2 * GLA_BLOCK, 0
        while lvl <= c:
            half = lvl // 2
            bmid = _rows_bcast(b_ref, [r0 + i * lvl + half - 1 for i in range(c // lvl)], ks, lvl // SUBLANES)
            e = jnp.exp2((b - bmid) * sgn_ref[j])
            att_l = _dot_nt((q * e).astype(BF16), (k * e).astype(BF16))
            att = jnp.where(code == CODE_BLOCK + 1 + j, att_l, att)
            lvl, j = lvl * 2, j + 1
    nblk = c // GLA_BLOCK
    parts = []
    for r in range(GLA_BLOCK):
        rows = [r0 + i * GLA_BLOCK + r for i in range(nblk)]
        br = _rows_bcast(b_ref, rows, ks, 1)
        kr = _rows_bcast(kf_ref, rows, ks, 1)
        parts.append((q * kr * jnp.exp2(jnp.minimum(b - br, 0.0))).astype(BF16))
    same_block = _dot(jnp.concatenate(parts, axis=1), sel_ref[...])
    return jnp.where(code == CODE_BLOCK, same_block, att)


def _gla_head_inputs(qk_ref, vg_ref, b_ref, h):
    ks, kv = h * GLA_HEAD_K, h * GLA_HEAD_V
    q = qk_ref[:, ks:ks + GLA_HEAD_K].astype(F32) * (GLA_HEAD_K ** -0.5)
    k = qk_ref[:, GLA_KEY_DIM + ks:GLA_KEY_DIM + ks + GLA_HEAD_K].astype(F32)
    v_bf = vg_ref[:, kv:kv + GLA_HEAD_V]
    gate = vg_ref[:, GLA_VAL_DIM + kv:GLA_VAL_DIM + kv + GLA_HEAD_V].astype(F32)
    b = b_ref[:, ks:ks + GLA_HEAD_K]
    return ks, kv, q, k, v_bf, gate, b


def _inproj_gla_body(h_ref, nw_ref, w_ref, wgk_ref, bgk_ref, code_ref, sgn_ref, sel_ref, gn_ref,
                     og_ref, xp_ref, u_ref, vv_ref, gt_ref, sfin_ref,
                     s_scr, q_scr, k_scr, v_scr, g_scr, b_scr, *, tiles_per_seq):
    i = pl.program_id(0)
    tm = h_ref.shape[0]
    c = GLA_CHUNK

    @pl.when(i % tiles_per_seq == 0)
    def _():
        s_scr[...] = jnp.zeros_like(s_scr)

    xn = _rms(h_ref[...], nw_ref[...]).astype(BF16)

    def proj(lo, hi):
        return _dot(xn, w_ref[:, lo:hi])

    g2 = jax.nn.log_sigmoid(_dot(proj(*C_GLR).astype(BF16), wgk_ref[...]) + bgk_ref[...]) * LOG2_DECAY_SCALE
    b_scr[...] = _seg_cumsum(g2, c)
    g_scr[...] = jax.nn.silu(proj(*C_GOUT)).astype(BF16)
    v_scr[...] = proj(*C_V).astype(BF16)
    qk = proj(*C_QK)
    q_scr[...] = qk[:, :GLA_KEY_DIM] * (GLA_HEAD_K ** -0.5)
    k_scr[...] = qk[:, GLA_KEY_DIM:]

    def tail_u():
        u_ref[...] = jax.nn.gelu(proj(*C_U)).astype(u_ref.dtype)

    def tail_vv():
        vv_ref[...] = jax.nn.gelu(proj(*C_VV)).astype(vv_ref.dtype)

    def tail_gate(j):
        lo = C_GATES[0] + j * D_MODEL
        gt_ref[:, j * D_MODEL:(j + 1) * D_MODEL] = jax.nn.sigmoid(proj(lo, lo + D_MODEL)).astype(BF16)

    def tail_xp():
        xp_ref[...] = proj(*C_XP)

    tails = [tail_u, tail_vv] + [functools.partial(tail_gate, j) for j in range(N_BRANCHES)] + [tail_xp]

    code = code_ref[...]
    gn = gn_ref[...]
    for ch in range(tm // c):
        r0 = ch * c
        for h in range(GLA_HEADS):
            ks, kv = h * GLA_HEAD_K, h * GLA_HEAD_V
            q = q_scr[r0:r0 + c, ks:ks + GLA_HEAD_K]
            k = k_scr[r0:r0 + c, ks:ks + GLA_HEAD_K]
            b = b_scr[r0:r0 + c, ks:ks + GLA_HEAD_K]
            v_bf = v_scr[r0:r0 + c, kv:kv + GLA_HEAD_V]
            gate = g_scr[r0:r0 + c, kv:kv + GLA_HEAD_V].astype(F32)
            s = s_scr[h]
            o = _dot((q * jnp.exp2(b)).astype(BF16), s.astype(BF16))
            att = _gla_scores(q, k, b, b_scr, k_scr, sel_ref, ks, code, c, sgn_ref, r0)
            o = o + _dot(att.astype(BF16), v_bf)
            og_ref[r0:r0 + c, kv:kv + GLA_HEAD_V] = (_rms(o, gn) * gate).astype(BF16)

            b_t = b.T
            b_last = b_t[:, c - 1:c]
            kd_t = (k.T * jnp.exp2(b_last - b_t)).astype(BF16)
            s_scr[h] = jnp.exp2(b_last) * s + _dot(kd_t, v_bf)
            if tails:
                tails.pop(0)()
    for t in tails:
        t()

    @pl.when(i % tiles_per_seq == tiles_per_seq - 1)
    def _():
        sfin_ref[0] = s_scr[...]


def _inproj_gla(h, norm_w, w_r, wgk, bgk, gn, layer, *, tm, n_seq, seq_len):
    m = h.shape[0]
    c = GLA_CHUNK
    tiles_per_seq = seq_len // tm
    row = lambda w: pl.BlockSpec((tm, w), lambda i: (i, 0))
    out_shape = (
        jax.ShapeDtypeStruct((m, GLA_VAL_DIM), BF16),
        jax.ShapeDtypeStruct((m, POOL_WIDTH), F32),
        jax.ShapeDtypeStruct((m, SG_WIDTH), BF16),
        jax.ShapeDtypeStruct((m, SG_WIDTH), BF16),
        jax.ShapeDtypeStruct((m, N_BRANCHES * D_MODEL), BF16),
        jax.ShapeDtypeStruct((n_seq, GLA_HEADS, GLA_HEAD_K, GLA_HEAD_V), F32),
    )
    code, sgn, sel = _score_codes(c, True), _level_signs(c), _block_sum_matrix(c)
    return pl.pallas_call(
        functools.partial(_inproj_gla_body, tiles_per_seq=tiles_per_seq),
        grid=(m // tm,),
        in_specs=[row(D_MODEL), _resident((1, D_MODEL), layer), _resident((D_MODEL, IN_COLS_R), layer),
                  _resident((LANES, GLA_KEY_DIM), layer), _resident((1, GLA_KEY_DIM), layer),
                  _resident(code.shape), _resident(sgn.shape), _resident(sel.shape),
                  _resident((1, GLA_HEAD_V), layer)],
        out_specs=[row(GLA_VAL_DIM), row(POOL_WIDTH), row(SG_WIDTH), row(SG_WIDTH), row(N_BRANCHES * D_MODEL),
                   pl.BlockSpec((1, GLA_HEADS, GLA_HEAD_K, GLA_HEAD_V), lambda i: (i // tiles_per_seq, 0, 0, 0))],
        out_shape=out_shape,
        scratch_shapes=[pltpu.VMEM((GLA_HEADS, GLA_HEAD_K, GLA_HEAD_V), F32),
                        pltpu.VMEM((tm, GLA_KEY_DIM), F32), pltpu.VMEM((tm, GLA_KEY_DIM), F32),
                        pltpu.VMEM((tm, GLA_VAL_DIM), BF16), pltpu.VMEM((tm, GLA_VAL_DIM), BF16),
                        pltpu.VMEM((tm, GLA_KEY_DIM), F32)],
        compiler_params=pltpu.CompilerParams(
            dimension_semantics=("arbitrary",), vmem_limit_bytes=VMEM_LIMIT_BYTES),
        name="inproj_gla",
    )(h, norm_w, w_r, wgk, bgk, code, sgn, sel, gn)


def _gla_sample_body(qk_ref, vg_ref, b_ref, s0_ref, code_ref, sel_ref, gn_ref, *rest, seq_len):
    og_ref, snew_ref, kf_scr = rest[-3:]
    nb = SAMPLE_SEQS_PER_STEP
    c = nb * seq_len
    kf_scr[...] = qk_ref[:, GLA_KEY_DIM:].astype(F32)
    code = code_ref[...]
    gn = gn_ref[...]
    lane_seq = lax.broadcasted_iota(jnp.int32, (GLA_HEAD_K, c), 1) // seq_len
    last_rows = [(n + 1) * seq_len - 1 for n in range(nb)]
    for h in range(GLA_HEADS):
        ks, kv, q, k, v_bf, gate, b = _gla_head_inputs(qk_ref, vg_ref, b_ref, h)
        qi = q * jnp.exp2(b)
        inter = [
            _dot(qi[n * seq_len:(n + 1) * seq_len, :].astype(BF16), s0_ref[n, h].astype(BF16))
            for n in range(nb)
        ]
        att = _gla_scores(q, k, b, b_ref, kf_scr, sel_ref, ks, code, c)
        o = jnp.concatenate(inter, axis=0) + _dot(att.astype(BF16), v_bf)
        og_ref[:, kv:kv + GLA_HEAD_V] = (_rms(o, gn) * gate).astype(BF16)

        b_last_rows = _rows_bcast(b_ref, last_rows, ks, 1)
        kd_t = (k * jnp.exp2(b_last_rows - b)).T
        b_t = b.T
        for n in range(nb):
            col = last_rows[n]
            decay = jnp.exp2(b_t[:, col:col + 1])
            kd_n = jnp.where(lane_seq == n, kd_t, 0.0).astype(BF16)
            snew_ref[n, h] = decay * s0_ref[n, h] + _dot(kd_n, v_bf)


def _gla_sample(qk, vg, b2, state_all, gn, layer, prev_new_state, *, seq_len):
    assert seq_len == GLA_BLOCK
    nb = SAMPLE_SEQS_PER_STEP
    n_seq = state_all.shape[1]
    c = nb * seq_len
    row = lambda w: pl.BlockSpec((c, w), lambda i: (i, 0))
    st = pl.BlockSpec((None, nb, GLA_HEADS, GLA_HEAD_K, GLA_HEAD_V), lambda i: (layer, i, 0, 0, 0))
    in_specs = [row(2 * GLA_KEY_DIM), row(2 * GLA_VAL_DIM), row(GLA_KEY_DIM), st, _resident((c, c)),
                _resident((GLA_BLOCK * GLA_HEAD_K, c)), _resident((1, GLA_HEAD_V), layer)]
    args = [qk, vg, b2, state_all, _score_codes(c, False), _block_sum_matrix(c), gn]
    aliases = {}
    if prev_new_state is not None:
        in_specs.append(pl.BlockSpec(memory_space=pl.ANY))
        args.append(prev_new_state)
        aliases = {len(args) - 1: 1}
    return pl.pallas_call(
        functools.partial(_gla_sample_body, seq_len=seq_len),
        grid=(n_seq // nb,),
        in_specs=in_specs,
        out_specs=[row(GLA_VAL_DIM), st],
        out_shape=(jax.ShapeDtypeStruct((n_seq * seq_len, GLA_VAL_DIM), BF16),
                   jax.ShapeDtypeStruct(state_all.shape, F32)),
        scratch_shapes=[pltpu.VMEM((c, GLA_KEY_DIM), F32)],
        input_output_aliases=aliases,
        compiler_params=pltpu.CompilerParams(
            dimension_semantics=("arbitrary",), vmem_limit_bytes=VMEM_LIMIT_BYTES),
        name="gla_sample",
    )(*args)


def _window_sums(full):
    outs = []
    s = full
    w = 1
    for gi, win in enumerate(POOL_WINDOWS):
        while w < win:
            s = s + pltpu.roll(s, w, 0)
            w *= 2
        outs.append(s[:, gi * POOL_GROUP_DIM:(gi + 1) * POOL_GROUP_DIM])
    return jnp.concatenate(outs, axis=1)


def _pool_counts(pos, n_prev):
    grp = lax.broadcasted_iota(jnp.int32, pos.shape, pos.ndim - 1) // POOL_GROUP_DIM
    win = jnp.zeros(pos.shape, jnp.int32)
    for gi, w in enumerate(POOL_WINDOWS):
        win = jnp.where(grp == gi, w, win)
    return jnp.minimum(win, pos + 1 + n_prev).astype(F32)


def _pool_prompt_body(x_ref, prev_ref, o_ref, *, tiles_per_seq):
    i = pl.program_id(0)
    tp = x_ref.shape[0]
    hist_rows = prev_ref.shape[0]
    first = (i % tiles_per_seq) == 0
    x = x_ref[...]
    prev = jnp.where(first, 0.0, prev_ref[...])
    sums = _window_sums(jnp.concatenate([prev, x], axis=0))[hist_rows:, :]
    pos = _row_iota(tp, POOL_WIDTH) + (i % tiles_per_seq) * tp
    o_ref[...] = (sums / _pool_counts(pos, 0) - x).astype(BF16)


def _pool_prompt(xp, *, seq_len, tp):
    m = xp.shape[0]
    hist_rows = 2 * SUBLANES
    ratio = tp // hist_rows
    return pl.pallas_call(
        functools.partial(_pool_prompt_body, tiles_per_seq=seq_len // tp),
        grid=(m // tp,),
        in_specs=[pl.BlockSpec((tp, POOL_WIDTH), lambda i: (i, 0)),
                  pl.BlockSpec((hist_rows, POOL_WIDTH), lambda i: (jnp.maximum(i * ratio - 1, 0), 0))],
        out_specs=pl.BlockSpec((tp, POOL_WIDTH), lambda i: (i, 0)),
        out_shape=jax.ShapeDtypeStruct((m, POOL_WIDTH), BF16),
        compiler_params=pltpu.CompilerParams(dimension_semantics=("arbitrary",)),
        name="pool_prompt",
    )(xp, xp)


def _pool_sample_body(x_ref, hist_ref, o_ref, nh_ref, *, n_prev):
    nb, t, _ = x_ref.shape
    hr = hist_ref.shape[1]
    x = x_ref[...]
    full = jnp.concatenate([hist_ref[...], x], axis=1)
    sums = _window_sums(full.reshape(nb * (hr + t), POOL_WIDTH)).reshape(nb, hr + t, POOL_WIDTH)
    pos = lax.broadcasted_iota(jnp.int32, (nb, t, POOL_WIDTH), 1)
    o_ref[...] = (sums[:, hr:, :] / _pool_counts(pos, n_prev) - x).astype(BF16)
    nh_ref[...] = full[:, t:, :]


def _pool_sample(xp3, hist16, *, n_prev, nb):
    n, t, _ = xp3.shape
    hr = hist16.shape[1]
    blk = lambda r: pl.BlockSpec((nb, r, POOL_WIDTH), lambda i: (i, 0, 0))
    return pl.pallas_call(
        functools.partial(_pool_sample_body, n_prev=n_prev),
        grid=(n // nb,),
        in_specs=[blk(t), blk(hr)],
        out_specs=[blk(t), blk(hr)],
        out_shape=(jax.ShapeDtypeStruct((n, t, POOL_WIDTH), BF16),
                   jax.ShapeDtypeStruct((n, hr, POOL_WIDTH), F32)),
        compiler_params=pltpu.CompilerParams(dimension_semantics=("arbitrary",)),
        name="pool_sample",
    )(xp3, hist16)


def _merge_ffn_body(h_ref, og_ref, pl_ref, u_ref, vv_ref, gt_ref,
                    wmix_ref, pscale_ref, wsg_ref, bsg_ref, wa_ref, wb_ref, wc_ref, wo_ref,
                    nf_ref, w1_ref, w2_ref, nfin_ref, o_ref, *, final_norm):
    tm = h_ref.shape[0]
    br = gt_ref[:, 0:D_MODEL].astype(F32) * _dot(og_ref[...], wa_ref[...])
    mixed = jnp.concatenate(
        [_dot(pl_ref[:, g * POOL_GROUP_DIM:(g + 1) * POOL_GROUP_DIM], wmix_ref[g])
         for g in range(POOL_GROUPS)], axis=1)
    pool_out = (mixed * pscale_ref[...]).astype(BF16)
    br = br + gt_ref[:, D_MODEL:2 * D_MODEL].astype(F32) * _dot(pool_out, wb_ref[...])
    rows = []
    for c in range(tm // SG_CHUNK):
        r0 = c * SG_CHUNK
        vv = vv_ref[r0:r0 + SG_CHUNK, :].astype(BF16)
        sg = jnp.concatenate(
            [_dot(wsg_ref[g], vv[:, g * SG_GROUP_DIM:(g + 1) * SG_GROUP_DIM]) for g in range(SG_GROUPS)],
            axis=1) + bsg_ref[...]
        rows.append((u_ref[r0:r0 + SG_CHUNK, :].astype(F32) * sg).astype(BF16))
    sg_out = jnp.concatenate(rows, axis=0)
    br = br + gt_ref[:, 2 * D_MODEL:3 * D_MODEL].astype(F32) * _dot(sg_out, wc_ref[...])
    h = h_ref[...] + _dot(br.astype(BF16), wo_ref[...])
    a = jnp.maximum(_dot(_rms(h, nf_ref[...]).astype(BF16), w1_ref[...]), 0.0)
    h = h + _dot((a * a).astype(BF16), w2_ref[...])
    if final_norm:
        h = _rms(h, nfin_ref[...])
    o_ref[...] = h


def _merge_ffn(h, og, pooled, u, vv, gates, layer_wts, sg_wts, nfin, layer, *, tm, final_norm):
    m = h.shape[0]
    row = lambda w: pl.BlockSpec((tm, w), lambda i: (i, 0))
    (wmix, pscale, wa, wb, wc, wo, nf, w1, w2) = layer_wts
    wsg, bsg = sg_wts
    lw = lambda w: _resident(w.shape[1:], layer)
    return pl.pallas_call(
        functools.partial(_merge_ffn_body, final_norm=final_norm),
        grid=(m // tm,),
        in_specs=[row(D_MODEL), row(GLA_VAL_DIM), row(POOL_WIDTH), row(SG_WIDTH), row(SG_WIDTH),
                  row(N_BRANCHES * D_MODEL),
                  lw(wmix), lw(pscale), lw(wsg), lw(bsg), lw(wa), lw(wb), lw(wc), lw(wo), lw(nf), lw(w1), lw(w2),
                  _resident(nfin.shape)],
        out_specs=row(D_MODEL),
        out_shape=jax.ShapeDtypeStruct((m, D_MODEL), F32),
        compiler_params=pltpu.CompilerParams(
            dimension_semantics=("arbitrary",), vmem_limit_bytes=VMEM_LIMIT_BYTES),
        name="merge_ffn",
    )(h, og, pooled, u, vv, gates, wmix, pscale, wsg, bsg, wa, wb, wc, wo, nf, w1, w2, nfin)


def _reorder_w_in(w):
    glr0 = 2 * GLA_KEY_DIM + 2 * GLA_VAL_DIM
    glr1 = glr0 + GLA_GATE_RANK
    pad = jnp.zeros(w.shape[:-1] + (LANES - GLA_GATE_RANK,), w.dtype)
    return jnp.concatenate([w[..., :glr0], w[..., glr1:], w[..., glr0:glr1], pad], axis=-1).astype(BF16)


def _sg_operands(w_spatial, b_spatial, length):
    reps = SG_CHUNK // length
    t = jnp.arange(SG_CHUNK)
    w = jnp.tile(w_spatial[:, :, :length, :length], (1, 1, reps, reps))
    keep = ((t[:, None] // length) == (t[None, :] // length)) & ((t[None, :] % length) <= (t[:, None] % length))
    w = jnp.where(keep, w, 0.0).astype(BF16)
    bias = jnp.repeat(jnp.swapaxes(jnp.tile(b_spatial[:, :, :length], (1, 1, reps)), 1, 2), SG_GROUP_DIM, axis=2)
    return w, bias.astype(F32)


def kernel(x_prompt, x_sample, state_gla, state_pool, norm_mix, w_in, w_gk2, b_gk, gla_norm, w_pool_mix,
           pool_scale, w_spatial, b_spatial, w_br_a, w_br_b, w_br_c, w_out, norm_ffn, w_ff1, w_ff2,
           norm_final):
    depth = w_in.shape[0]
    n_p, t_p, _ = x_prompt.shape
    n_s, t_s, _ = x_sample.shape
    hp = x_prompt.reshape(n_p * t_p, D_MODEL)
    hs = x_sample.reshape(n_s * t_s, D_MODEL)
    n_prev_s = POOL_HIST
    rows = lambda a: a.reshape(depth, 1, -1).astype(F32)
    bf = lambda a: a.astype(BF16)

    w_r = _reorder_w_in(w_in)
    wgk = bf(jnp.pad(w_gk2, ((0, 0), (0, LANES - GLA_GATE_RANK), (0, 0))))
    bgk, gn, nmix = rows(b_gk), rows(gla_norm), rows(norm_mix)
    layer_wts = (bf(w_pool_mix), rows(pool_scale), bf(w_br_a), bf(w_br_b), bf(w_br_c), bf(w_out),
                 rows(norm_ffn), bf(w_ff1), bf(w_ff2))
    sg_p = _sg_operands(w_spatial, b_spatial, min(t_p, SG_CHUNK))
    sg_s = _sg_operands(w_spatial, b_spatial, min(t_s, SG_CHUNK))
    nfin = norm_final.reshape(1, -1).astype(F32)
    hist16 = jnp.pad(state_pool, ((0, 0), (0, 0), (1, 0), (0, 0)))

    gla_p, pool_p, pool_s, sgv_s = [], [], [], []
    gla_s = None
    for l in range(depth):
        final = l == depth - 1
        og, xp, u, vv, gates, s_fin = _inproj_gla(hp, nmix, w_r, wgk, bgk, gn, l, tm=256, n_seq=n_p, seq_len=t_p)
        pooled = _pool_prompt(xp, seq_len=t_p, tp=512)
        hp = _merge_ffn(hp, og, pooled, u, vv, gates, layer_wts, sg_p, nfin, l, tm=256, final_norm=final)
        gla_p.append(s_fin)
        pool_p.append(xp.reshape(n_p, t_p, POOL_WIDTH)[:, t_p - POOL_HIST:, :])

        qk, vg, xp, u, vv, gates, b2 = _inproj(hs, nmix, w_r, wgk, bgk, l, tm=256, seg=t_s, vv_dtype=F32)
        og, gla_s = _gla_sample(qk, vg, b2, state_gla, gn, l, gla_s, seq_len=t_s)
        pooled, new_hist = _pool_sample(xp.reshape(n_s, t_s, POOL_WIDTH), hist16[l], n_prev=n_prev_s, nb=16)
        hs = _merge_ffn(hs, og, pooled.reshape(n_s * t_s, POOL_WIDTH), u, vv, gates, layer_wts, sg_s, nfin, l,
                        tm=256, final_norm=final)
        pool_s.append(new_hist[:, 1:, :])
        sgv_s.append(vv.reshape(n_s, t_s, SG_WIDTH))

    return (hp.reshape(n_p, t_p, D_MODEL), hs.reshape(n_s, t_s, D_MODEL), jnp.stack(gla_p), gla_s,
            jnp.stack(pool_p), jnp.stack(pool_s), jnp.stack(sgv_s))
```

```python
import functools
import math

import jax
import jax.numpy as jnp
import numpy as np
from jax import lax
from jax.experimental import pallas as pl
from jax.experimental.pallas import tpu as pltpu

F32 = jnp.float32
BF16 = jnp.bfloat16

D_MODEL = 1024
GLA_HEADS = 4
GLA_KEY_DIM = 512
GLA_VAL_DIM = 1024
GLA_HEAD_K = 128
GLA_HEAD_V = 256
GLA_GATE_RANK = 16
GLA_GATE_NORMALIZER = 16.0
POOL_WIDTH = 512
POOL_WINDOWS = (2, 4, 8, 16)
POOL_GROUPS = 4
POOL_GROUP_DIM = 128
POOL_HIST = 15
SG_WIDTH = 512
SG_GROUPS = 4
SG_GROUP_DIM = 128
SG_CHUNK = 128
N_BRANCHES = 3
D_FF = 4096
EPS = 1e-6

SUBLANES = 8
LANES = 128
VMEM_LIMIT_BYTES = 56 * 1024 * 1024

C_QK = (0, 1024)
C_V = (1024, 2048)
C_GOUT = (2048, 3072)
C_XP = (3072, 3584)
C_U = (3584, 4096)
C_VV = (4096, 4608)
C_GATES = (4608, 7680)
C_GLR = (7680, 7808)
IN_COLS_R = 7808

TM_PROJ = 256
TM_MERGE = 512
PROJ_SLICE = 256
HEADS_IN_FLIGHT = 1
GLA_CHUNK = 128
GLA_BLOCK = SUBLANES
SAMPLE_SEQS_PER_STEP = 16
LOG2_DECAY_SCALE = math.log2(math.e) / GLA_GATE_NORMALIZER


def _resident(shape, layer=None):
    nd = len(shape)
    if layer is None:
        return pl.BlockSpec(shape, lambda *_: (0,) * nd, pipeline_mode=pl.Buffered(1))
    return pl.BlockSpec((None,) + tuple(shape), lambda *_: (layer,) + (0,) * nd,
                        pipeline_mode=pl.Buffered(1))


def _rms(x, w):
    return x * lax.rsqrt(jnp.mean(x * x, axis=-1, keepdims=True) + EPS) * w


def _dot(a, b):
    return jnp.dot(a, b, preferred_element_type=F32)


def _dot_nt(a, b):
    return lax.dot_general(a, b, (((1,), (1,)), ((), ())), preferred_element_type=F32)


def _row_iota(c, w):
    return lax.broadcasted_iota(jnp.int32, (c, w), 0)


def _bcast_row(x, blk, r):
    c, w = x.shape
    x3 = x.reshape(c // blk, blk, w)
    return jnp.broadcast_to(x3[:, r:r + 1, :], x3.shape).reshape(c, w)


def _seg_cumsum(g, seg):
    c, w = g.shape
    tpos = _row_iota(c, w) % SUBLANES
    for sh in (1, 2, 4):
        g = g + jnp.where(tpos >= sh, pltpu.roll(g, sh, 0), 0.0)
    if seg > SUBLANES:
        nb = seg // SUBLANES
        y = _bcast_row(g, SUBLANES, SUBLANES - 1).reshape(c // seg, nb, SUBLANES, w)
        sh = 1
        while sh < nb:
            y = y + jnp.concatenate([jnp.zeros((c // seg, sh, SUBLANES, w), F32), y[:, :nb - sh]], axis=1)
            sh *= 2
        excl = jnp.concatenate([jnp.zeros((c // seg, 1, SUBLANES, w), F32), y[:, :nb - 1]], axis=1)
        g = g + excl.reshape(c, w)
    return g


def _inproj_body(h_ref, nw_ref, w_ref, wgk_ref, bgk_ref,
                 qk_ref, vg_ref, xp_ref, u_ref, vv_ref, gt_ref, b2_ref, *, seg):
    xn = _rms(h_ref[...], nw_ref[...]).astype(BF16)

    def proj(lo, hi):
        return _dot(xn, w_ref[:, lo:hi])

    g2 = jax.nn.log_sigmoid(_dot(proj(*C_GLR).astype(BF16), wgk_ref[...]) + bgk_ref[...]) * LOG2_DECAY_SCALE
    b2_ref[...] = _seg_cumsum(g2, seg)
    vg_ref[:, GLA_VAL_DIM:] = jax.nn.silu(proj(*C_GOUT)).astype(BF16)
    u_ref[...] = jax.nn.gelu(proj(*C_U)).astype(u_ref.dtype)
    vv_ref[...] = jax.nn.gelu(proj(*C_VV)).astype(vv_ref.dtype)
    for j in range(N_BRANCHES):
        lo = C_GATES[0] + j * D_MODEL
        gt_ref[:, j * D_MODEL:(j + 1) * D_MODEL] = jax.nn.sigmoid(proj(lo, lo + D_MODEL)).astype(BF16)
    xp_ref[...] = proj(*C_XP)
    vg_ref[:, :GLA_VAL_DIM] = proj(*C_V).astype(BF16)
    qk_ref[...] = proj(*C_QK).astype(BF16)


def _inproj(h, norm_w, w_r, wgk, bgk, layer, *, tm, seg, vv_dtype):
    m = h.shape[0]
    row = lambda w: pl.BlockSpec((tm, w), lambda i: (i, 0))
    out_shape = (
        jax.ShapeDtypeStruct((m, 2 * GLA_KEY_DIM), BF16),
        jax.ShapeDtypeStruct((m, 2 * GLA_VAL_DIM), BF16),
        jax.ShapeDtypeStruct((m, POOL_WIDTH), F32),
        jax.ShapeDtypeStruct((m, SG_WIDTH), BF16),
        jax.ShapeDtypeStruct((m, SG_WIDTH), vv_dtype),
        jax.ShapeDtypeStruct((m, N_BRANCHES * D_MODEL), BF16),
        jax.ShapeDtypeStruct((m, GLA_KEY_DIM), F32),
    )
    return pl.pallas_call(
        functools.partial(_inproj_body, seg=seg),
        grid=(m // tm,),
        in_specs=[row(D_MODEL), _resident((1, D_MODEL), layer), _resident((D_MODEL, IN_COLS_R), layer),
                  _resident((LANES, GLA_KEY_DIM), layer), _resident((1, GLA_KEY_DIM), layer)],
        out_specs=[row(2 * GLA_KEY_DIM), row(2 * GLA_VAL_DIM), row(POOL_WIDTH), row(SG_WIDTH), row(SG_WIDTH),
                   row(N_BRANCHES * D_MODEL), row(GLA_KEY_DIM)],
        out_shape=out_shape,
        compiler_params=pltpu.CompilerParams(
            dimension_semantics=("arbitrary",), vmem_limit_bytes=VMEM_LIMIT_BYTES),
        name="inproj",
    )(h, norm_w, w_r, wgk, bgk)


CODE_BLOCK = 1


def _score_codes(c, levels):
    t = np.arange(c)[:, None]
    s = np.arange(c)[None, :]
    code = np.zeros((c, c), np.int32)
    code = np.where((t // GLA_BLOCK == s // GLA_BLOCK) & (s <= t), CODE_BLOCK, code)
    if levels:
        lvl, j = 2 * GLA_BLOCK, 0
        while lvl <= c:
            hit = (t // lvl == s // lvl) & (t // (lvl // 2) != s // (lvl // 2)) & (s < t)
            code = np.where(hit, CODE_BLOCK + 1 + j, code)
            lvl, j = lvl * 2, j + 1
    return jnp.asarray(code, jnp.int32)


def _block_sum_matrix(c):
    r = np.arange(GLA_BLOCK * GLA_HEAD_K)[:, None] // GLA_HEAD_K
    s = np.arange(c)[None, :]
    return jnp.asarray(r == s % GLA_BLOCK, BF16)


def _level_signs(c):
    t = np.arange(c)[:, None]
    out = []
    lvl = 2 * GLA_BLOCK
    while lvl <= c:
        out.append(np.broadcast_to(np.where(t % lvl >= lvl // 2, 1.0, -1.0), (c, GLA_HEAD_K)))
        lvl *= 2
    return jnp.asarray(np.stack(out), F32)


def _rows_bcast(ref, rows, lo, reps):
    parts = []
    for r in rows:
        tile = jnp.broadcast_to(ref[pl.ds(r, 1), lo:lo + GLA_HEAD_K], (SUBLANES, GLA_HEAD_K))
        parts.extend([tile] * reps)
    return jnp.concatenate(parts, axis=0)


def _run(steps):
    try:
        while True:
            next(steps)
    except StopIteration as stop:
        return stop.value


def _gla_scores_steps(q_ref, k_ref, b_ref, sel_ref, code_ref, r0, ks, c, sgn_ref=None):
    tile = lambda ref: ref[r0:r0 + c, ks:ks + GLA_HEAD_K]
    att = jnp.zeros((c, c), F32)
    if sgn_ref is not None:
        lvl, j = 2 * GLA_BLOCK, 0
        while lvl <= c:
            half = lvl // 2
            bmid = _rows_bcast(b_ref, [r0 + i * lvl + half - 1 for i in range(c // lvl)], ks, lvl // SUBLANES)
            e = jnp.exp2((tile(b_ref) - bmid) * sgn_ref[j])
            att_l = _dot_nt((tile(q_ref) * e).astype(BF16), (tile(k_ref) * e).astype(BF16))
            att = jnp.where(code_ref[...] == CODE_BLOCK + 1 + j, att_l, att)
            lvl, j = lvl * 2, j + 1
            yield
    nblk = c // GLA_BLOCK
    parts = []
    for r in range(GLA_BLOCK):
        rows = [r0 + i * GLA_BLOCK + r for i in range(nblk)]
        br = _rows_bcast(b_ref, rows, ks, 1)
        kr = _rows_bcast(k_ref, rows, ks, 1)
        parts.append((tile(q_ref) * kr * jnp.exp2(jnp.minimum(tile(b_ref) - br, 0.0))).astype(BF16))
        yield
    same_block = _dot(jnp.concatenate(parts, axis=1), sel_ref[...])
    return jnp.where(code_ref[...] == CODE_BLOCK, same_block, att)


def _inproj_gla_body(h_ref, nw_ref, w_ref, wgk_ref, bgk_ref, code_ref, sgn_ref, sel_ref, gn_ref,
                     og_ref, xp_ref, u_ref, vv_ref, gt_ref, sfin_ref,
                     s_scr, q_scr, k_scr, v_scr, g_scr, b_scr, *, tiles_per_seq):
    i = pl.program_id(0)
    tm = h_ref.shape[0]
    c = GLA_CHUNK

    @pl.when(i % tiles_per_seq == 0)
    def _():
        s_scr[...] = jnp.zeros_like(s_scr)

    xn = _rms(h_ref[...], nw_ref[...]).astype(BF16)

    def proj(lo, hi):
        return _dot(xn, w_ref[:, lo:hi])

    def slice_task(lo, dst_ref, dst_lo, act):
        def run():
            dst_ref[:, dst_lo:dst_lo + PROJ_SLICE] = act(proj(lo, lo + PROJ_SLICE)).astype(dst_ref.dtype)
        return run

    def slice_tasks(cols, dst_ref, act=lambda x: x):
        return [slice_task(lo, dst_ref, lo - cols[0], act) for lo in range(cols[0], cols[1], PROJ_SLICE)]

    q_scale = GLA_HEAD_K ** -0.5
    head_tasks = (slice_tasks(C_GOUT, g_scr, jax.nn.silu) + slice_tasks(C_V, v_scr)
                  + slice_tasks((C_QK[0], C_QK[0] + GLA_KEY_DIM), q_scr, lambda x: x * q_scale)
                  + slice_tasks((C_QK[0] + GLA_KEY_DIM, C_QK[1]), k_scr))
    tails = (slice_tasks(C_U, u_ref, jax.nn.gelu) + slice_tasks(C_VV, vv_ref, jax.nn.gelu)
             + slice_tasks(C_GATES, gt_ref, jax.nn.sigmoid) + slice_tasks(C_XP, xp_ref))

    glr = proj(*C_GLR).astype(BF16)
    g2 = jax.nn.log_sigmoid(_dot(glr, wgk_ref[...]) + bgk_ref[...]) * LOG2_DECAY_SCALE
    per_head = len(head_tasks) // GLA_HEADS
    for h in range(GLA_HEADS):
        ks = h * GLA_HEAD_K
        b_scr[:, ks:ks + GLA_HEAD_K] = _seg_cumsum(g2[:, ks:ks + GLA_HEAD_K], c)
        for t in head_tasks[h * per_head:(h + 1) * per_head]:
            t()

    def pump(n=1):
        for _ in range(min(n, len(tails))):
            tails.pop(0)()

    def head_steps(r0, h):
        ks, kv = h * GLA_HEAD_K, h * GLA_HEAD_V
        tile = lambda ref: ref[r0:r0 + c, ks:ks + GLA_HEAD_K]
        v_bf = lambda: v_scr[r0:r0 + c, kv:kv + GLA_HEAD_V]
        s = s_scr[h]
        o = _dot((tile(q_scr) * jnp.exp2(tile(b_scr))).astype(BF16), s.astype(BF16))
        b_t = tile(b_scr).T
        b_last = b_t[:, c - 1:c]
        kd_t = (tile(k_scr).T * jnp.exp2(b_last - b_t)).astype(BF16)
        s_scr[h] = jnp.exp2(b_last) * s + _dot(kd_t, v_bf())
        yield
        att = yield from _gla_scores_steps(q_scr, k_scr, b_scr, sel_ref, code_ref, r0, ks, c, sgn_ref)
        o = o + _dot(att.astype(BF16), v_bf())
        yield
        gate = g_scr[r0:r0 + c, kv:kv + GLA_HEAD_V].astype(F32)
        og_ref[r0:r0 + c, kv:kv + GLA_HEAD_V] = (_rms(o, gn_ref[...]) * gate).astype(BF16)

    for ch, h0 in [(ch, h0) for ch in range(tm // c) for h0 in range(0, GLA_HEADS, HEADS_IN_FLIGHT)]:
        active = [head_steps(ch * c, h) for h in range(h0, h0 + HEADS_IN_FLIGHT)]
        while active:
            for steps in list(active):
                try:
                    next(steps)
                except StopIteration:
                    active.remove(steps)
            pump()
    pump(len(tails))

    @pl.when(i % tiles_per_seq == tiles_per_seq - 1)
    def _():
        sfin_ref[0] = s_scr[...]


def _inproj_gla(h, norm_w, w_r, wgk, bgk, gn, layer, *, tm, n_seq, seq_len):
    m = h.shape[0]
    c = GLA_CHUNK
    tiles_per_seq = seq_len // tm
    row = lambda w: pl.BlockSpec((tm, w), lambda i: (i, 0))
    out_shape = (
        jax.ShapeDtypeStruct((m, GLA_VAL_DIM), BF16),
        jax.ShapeDtypeStruct((m, POOL_WIDTH), F32),
        jax.ShapeDtypeStruct((m, SG_WIDTH), BF16),
        jax.ShapeDtypeStruct((m, SG_WIDTH), BF16),
        jax.ShapeDtypeStruct((m, N_BRANCHES * D_MODEL), BF16),
        jax.ShapeDtypeStruct((n_seq, GLA_HEADS, GLA_HEAD_K, GLA_HEAD_V), F32),
    )
    code, sgn, sel = _score_codes(c, True), _level_signs(c), _block_sum_matrix(c)
    return pl.pallas_call(
        functools.partial(_inproj_gla_body, tiles_per_seq=tiles_per_seq),
        grid=(m // tm,),
        in_specs=[row(D_MODEL), _resident((1, D_MODEL), layer), _resident((D_MODEL, IN_COLS_R), layer),
                  _resident((LANES, GLA_KEY_DIM), layer), _resident((1, GLA_KEY_DIM), layer),
                  _resident(code.shape), _resident(sgn.shape), _resident(sel.shape),
                  _resident((1, GLA_HEAD_V), layer)],
        out_specs=[row(GLA_VAL_DIM), row(POOL_WIDTH), row(SG_WIDTH), row(SG_WIDTH), row(N_BRANCHES * D_MODEL),
                   pl.BlockSpec((1, GLA_HEADS, GLA_HEAD_K, GLA_HEAD_V), lambda i: (i // tiles_per_seq, 0, 0, 0))],
        out_shape=out_shape,
        scratch_shapes=[pltpu.VMEM((GLA_HEADS, GLA_HEAD_K, GLA_HEAD_V), F32),
                        pltpu.VMEM((tm, GLA_KEY_DIM), F32), pltpu.VMEM((tm, GLA_KEY_DIM), F32),
                        pltpu.VMEM((tm, GLA_VAL_DIM), BF16), pltpu.VMEM((tm, GLA_VAL_DIM), BF16),
                        pltpu.VMEM((tm, GLA_KEY_DIM), F32)],
        compiler_params=pltpu.CompilerParams(
            dimension_semantics=("arbitrary",), vmem_limit_bytes=VMEM_LIMIT_BYTES),
        name="inproj_gla",
    )(h, norm_w, w_r, wgk, bgk, code, sgn, sel, gn)


def _gla_sample_body(qk_ref, vg_ref, b_ref, s0_ref, code_ref, sel_ref, gn_ref, *rest, seq_len):
    og_ref, snew_ref, q_scr, k_scr = rest[-4:]
    nb = SAMPLE_SEQS_PER_STEP
    c = nb * seq_len
    q_scr[...] = qk_ref[:, :GLA_KEY_DIM].astype(F32) * (GLA_HEAD_K ** -0.5)
    k_scr[...] = qk_ref[:, GLA_KEY_DIM:].astype(F32)
    gn = gn_ref[...]
    lane_seq = lax.broadcasted_iota(jnp.int32, (GLA_HEAD_K, c), 1) // seq_len
    last_rows = [(n + 1) * seq_len - 1 for n in range(nb)]
    for h in range(GLA_HEADS):
        ks, kv = h * GLA_HEAD_K, h * GLA_HEAD_V
        k = k_scr[:, ks:ks + GLA_HEAD_K]
        b = b_ref[:, ks:ks + GLA_HEAD_K]
        v_bf = vg_ref[:, kv:kv + GLA_HEAD_V]
        gate = vg_ref[:, GLA_VAL_DIM + kv:GLA_VAL_DIM + kv + GLA_HEAD_V].astype(F32)
        qi = q_scr[:, ks:ks + GLA_HEAD_K] * jnp.exp2(b)
        inter = [
            _dot(qi[n * seq_len:(n + 1) * seq_len, :].astype(BF16), s0_ref[n, h].astype(BF16))
            for n in range(nb)
        ]
        att = _run(_gla_scores_steps(q_scr, k_scr, b_ref, sel_ref, code_ref, 0, ks, c))
        o = jnp.concatenate(inter, axis=0) + _dot(att.astype(BF16), v_bf)
        og_ref[:, kv:kv + GLA_HEAD_V] = (_rms(o, gn) * gate).astype(BF16)

        b_last_rows = _rows_bcast(b_ref, last_rows, ks, 1)
        kd_t = (k * jnp.exp2(b_last_rows - b)).T
        b_t = b.T
        for n in range(nb):
            col = last_rows[n]
            decay = jnp.exp2(b_t[:, col:col + 1])
            kd_n = jnp.where(lane_seq == n, kd_t, 0.0).astype(BF16)
            snew_ref[n, h] = decay * s0_ref[n, h] + _dot(kd_n, v_bf)


def _gla_sample(qk, vg, b2, state_all, gn, layer, prev_new_state, *, seq_len):
    assert seq_len == GLA_BLOCK
    nb = SAMPLE_SEQS_PER_STEP
    n_seq = state_all.shape[1]
    c = nb * seq_len
    row = lambda w: pl.BlockSpec((c, w), lambda i: (i, 0))
    st = pl.BlockSpec((None, nb, GLA_HEADS, GLA_HEAD_K, GLA_HEAD_V), lambda i: (layer, i, 0, 0, 0))
    in_specs = [row(2 * GLA_KEY_DIM), row(2 * GLA_VAL_DIM), row(GLA_KEY_DIM), st, _resident((c, c)),
                _resident((GLA_BLOCK * GLA_HEAD_K, c)), _resident((1, GLA_HEAD_V), layer)]
    args = [qk, vg, b2, state_all, _score_codes(c, False), _block_sum_matrix(c), gn]
    aliases = {}
    if prev_new_state is not None:
        in_specs.append(pl.BlockSpec(memory_space=pl.ANY))
        args.append(prev_new_state)
        aliases = {len(args) - 1: 1}
    return pl.pallas_call(
        functools.partial(_gla_sample_body, seq_len=seq_len),
        grid=(n_seq // nb,),
        in_specs=in_specs,
        out_specs=[row(GLA_VAL_DIM), st],
        out_shape=(jax.ShapeDtypeStruct((n_seq * seq_len, GLA_VAL_DIM), BF16),
                   jax.ShapeDtypeStruct(state_all.shape, F32)),
        scratch_shapes=[pltpu.VMEM((c, GLA_KEY_DIM), F32), pltpu.VMEM((c, GLA_KEY_DIM), F32)],
        input_output_aliases=aliases,
        compiler_params=pltpu.CompilerParams(
            dimension_semantics=("arbitrary",), vmem_limit_bytes=VMEM_LIMIT_BYTES),
        name="gla_sample",
    )(*args)


def _window_sums(full):
    outs = []
    s = full
    w = 1
    for gi, win in enumerate(POOL_WINDOWS):
        while w < win:
            s = s + pltpu.roll(s, w, 0)
            w *= 2
        outs.append(s[:, gi * POOL_GROUP_DIM:(gi + 1) * POOL_GROUP_DIM])
    return jnp.concatenate(outs, axis=1)


def _pool_counts(pos, n_prev):
    grp = lax.broadcasted_iota(jnp.int32, pos.shape, pos.ndim - 1) // POOL_GROUP_DIM
    win = jnp.zeros(pos.shape, jnp.int32)
    for gi, w in enumerate(POOL_WINDOWS):
        win = jnp.where(grp == gi, w, win)
    return jnp.minimum(win, pos + 1 + n_prev).astype(F32)


POOL_PREV_ROWS = 2 * SUBLANES


def _pool_prompt_tile(x_ref, prev_ref, tile_in_seq):
    tp = x_ref.shape[0]
    x = x_ref[...]
    prev = jnp.where(tile_in_seq == 0, 0.0, prev_ref[...])
    sums = _window_sums(jnp.concatenate([prev, x], axis=0))[POOL_PREV_ROWS:, :]
    pos = _row_iota(tp, POOL_WIDTH) + tile_in_seq * tp
    return (sums / _pool_counts(pos, 0) - x).astype(BF16)


def _pool_sample_body(x_ref, hist_ref, o_ref, nh_ref, *, n_prev):
    nb, t, _ = x_ref.shape
    hr = hist_ref.shape[1]
    x = x_ref[...]
    full = jnp.concatenate([hist_ref[...], x], axis=1)
    sums = _window_sums(full.reshape(nb * (hr + t), POOL_WIDTH)).reshape(nb, hr + t, POOL_WIDTH)
    pos = lax.broadcasted_iota(jnp.int32, (nb, t, POOL_WIDTH), 1)
    o_ref[...] = (sums[:, hr:, :] / _pool_counts(pos, n_prev) - x).astype(BF16)
    nh_ref[...] = full[:, t:, :]


def _pool_sample(xp3, hist16, *, n_prev, nb):
    n, t, _ = xp3.shape
    hr = hist16.shape[1]
    blk = lambda r: pl.BlockSpec((nb, r, POOL_WIDTH), lambda i: (i, 0, 0))
    return pl.pallas_call(
        functools.partial(_pool_sample_body, n_prev=n_prev),
        grid=(n // nb,),
        in_specs=[blk(t), blk(hr)],
        out_specs=[blk(t), blk(hr)],
        out_shape=(jax.ShapeDtypeStruct((n, t, POOL_WIDTH), BF16),
                   jax.ShapeDtypeStruct((n, hr, POOL_WIDTH), F32)),
        compiler_params=pltpu.CompilerParams(dimension_semantics=("arbitrary",)),
        name="pool_sample",
    )(xp3, hist16)


def _merge_ffn_body(h_ref, og_ref, u_ref, vv_ref, gt_ref, *rest, final_norm, pool_tiles_per_seq):
    if pool_tiles_per_seq is None:
        pl_ref, rest = rest[0], rest[1:]
        pooled = lambda g: pl_ref[:, g * POOL_GROUP_DIM:(g + 1) * POOL_GROUP_DIM]
    else:
        xp_ref, prev_ref, rest = rest[0], rest[1], rest[2:]
        pooled_all = _pool_prompt_tile(xp_ref, prev_ref, pl.program_id(0) % pool_tiles_per_seq)
        pooled = lambda g: pooled_all[:, g * POOL_GROUP_DIM:(g + 1) * POOL_GROUP_DIM]
    (wmix_ref, pscale_ref, wsg_ref, bsg_ref, wa_ref, wb_ref, wc_ref, wo_ref,
     nf_ref, w1_ref, w2_ref, nfin_ref, o_ref) = rest
    tm = h_ref.shape[0]
    br = gt_ref[:, 0:D_MODEL].astype(F32) * _dot(og_ref[...], wa_ref[...])
    mixed = jnp.concatenate([_dot(pooled(g), wmix_ref[g]) for g in range(POOL_GROUPS)], axis=1)
    pool_out = (mixed * pscale_ref[...]).astype(BF16)
    br = br + gt_ref[:, D_MODEL:2 * D_MODEL].astype(F32) * _dot(pool_out, wb_ref[...])
    rows = []
    for c in range(tm // SG_CHUNK):
        r0 = c * SG_CHUNK
        vv = vv_ref[r0:r0 + SG_CHUNK, :].astype(BF16)
        sg = jnp.concatenate(
            [_dot(wsg_ref[g], vv[:, g * SG_GROUP_DIM:(g + 1) * SG_GROUP_DIM]) for g in range(SG_GROUPS)],
            axis=1) + bsg_ref[...]
        rows.append((u_ref[r0:r0 + SG_CHUNK, :].astype(F32) * sg).astype(BF16))
    sg_out = jnp.concatenate(rows, axis=0)
    br = br + gt_ref[:, 2 * D_MODEL:3 * D_MODEL].astype(F32) * _dot(sg_out, wc_ref[...])
    h = h_ref[...] + _dot(br.astype(BF16), wo_ref[...])
    a = jnp.maximum(_dot(_rms(h, nf_ref[...]).astype(BF16), w1_ref[...]), 0.0)
    h = h + _dot((a * a).astype(BF16), w2_ref[...])
    if final_norm:
        h = _rms(h, nfin_ref[...])
    o_ref[...] = h


def _merge_ffn(h, og, pool_in, u, vv, gates, layer_wts, sg_wts, nfin, layer, *, tm, final_norm, pool_seq_len=None):
    m = h.shape[0]
    row = lambda w: pl.BlockSpec((tm, w), lambda i: (i, 0))
    (wmix, pscale, wa, wb, wc, wo, nf, w1, w2) = layer_wts
    wsg, bsg = sg_wts
    lw = lambda w: _resident(w.shape[1:], layer)
    if pool_seq_len is None:
        pool_specs, pool_args, tiles_per_seq = [row(POOL_WIDTH)], [pool_in], None
    else:
        ratio = tm // POOL_PREV_ROWS
        pool_specs = [row(POOL_WIDTH),
                      pl.BlockSpec((POOL_PREV_ROWS, POOL_WIDTH), lambda i: (jnp.maximum(i * ratio - 1, 0), 0))]
        pool_args, tiles_per_seq = [pool_in, pool_in], pool_seq_len // tm
    return pl.pallas_call(
        functools.partial(_merge_ffn_body, final_norm=final_norm, pool_tiles_per_seq=tiles_per_seq),
        grid=(m // tm,),
        in_specs=[row(D_MODEL), row(GLA_VAL_DIM), row(SG_WIDTH), row(SG_WIDTH), row(N_BRANCHES * D_MODEL)]
        + pool_specs
        + [lw(wmix), lw(pscale), lw(wsg), lw(bsg), lw(wa), lw(wb), lw(wc), lw(wo), lw(nf), lw(w1), lw(w2),
           _resident(nfin.shape)],
        out_specs=row(D_MODEL),
        out_shape=jax.ShapeDtypeStruct((m, D_MODEL), F32),
        compiler_params=pltpu.CompilerParams(
            dimension_semantics=("arbitrary",), vmem_limit_bytes=VMEM_LIMIT_BYTES),
        name="merge_ffn",
    )(h, og, u, vv, gates, *pool_args, wmix, pscale, wsg, bsg, wa, wb, wc, wo, nf, w1, w2, nfin)


def _reorder_w_in(w):
    glr0 = 2 * GLA_KEY_DIM + 2 * GLA_VAL_DIM
    glr1 = glr0 + GLA_GATE_RANK
    pad = jnp.zeros(w.shape[:-1] + (LANES - GLA_GATE_RANK,), w.dtype)
    return jnp.concatenate([w[..., :glr0], w[..., glr1:], w[..., glr0:glr1], pad], axis=-1).astype(BF16)


def _sg_operands(w_spatial, b_spatial, length):
    reps = SG_CHUNK // length
    t = jnp.arange(SG_CHUNK)
    w = jnp.tile(w_spatial[:, :, :length, :length], (1, 1, reps, reps))
    keep = ((t[:, None] // length) == (t[None, :] // length)) & ((t[None, :] % length) <= (t[:, None] % length))
    w = jnp.where(keep, w, 0.0).astype(BF16)
    bias = jnp.repeat(jnp.swapaxes(jnp.tile(b_spatial[:, :, :length], (1, 1, reps)), 1, 2), SG_GROUP_DIM, axis=2)
    return w, bias.astype(F32)


def kernel(x_prompt, x_sample, state_gla, state_pool, norm_mix, w_in, w_gk2, b_gk, gla_norm, w_pool_mix,
           pool_scale, w_spatial, b_spatial, w_br_a, w_br_b, w_br_c, w_out, norm_ffn, w_ff1, w_ff2,
           norm_final):
    depth = w_in.shape[0]
    n_p, t_p, _ = x_prompt.shape
    n_s, t_s, _ = x_sample.shape
    hp = x_prompt.reshape(n_p * t_p, D_MODEL)
    hs = x_sample.reshape(n_s * t_s, D_MODEL)
    n_prev_s = POOL_HIST
    rows = lambda a: a.reshape(depth, 1, -1).astype(F32)
    bf = lambda a: a.astype(BF16)

    w_r = _reorder_w_in(w_in)
    wgk = bf(jnp.pad(w_gk2, ((0, 0), (0, LANES - GLA_GATE_RANK), (0, 0))))
    bgk, gn, nmix = rows(b_gk), rows(gla_norm), rows(norm_mix)
    layer_wts = (bf(w_pool_mix), rows(pool_scale), bf(w_br_a), bf(w_br_b), bf(w_br_c), bf(w_out),
                 rows(norm_ffn), bf(w_ff1), bf(w_ff2))
    sg_p = _sg_operands(w_spatial, b_spatial, min(t_p, SG_CHUNK))
    sg_s = _sg_operands(w_spatial, b_spatial, min(t_s, SG_CHUNK))
    nfin = norm_final.reshape(1, -1).astype(F32)
    hist16 = jnp.pad(state_pool, ((0, 0), (0, 0), (1, 0), (0, 0)))

    gla_p, pool_p, pool_s, sgv_s = [], [], [], []
    gla_s = None
    for l in range(depth):
        final = l == depth - 1
        og, xp, u, vv, gates, s_fin = _inproj_gla(hp, nmix, w_r, wgk, bgk, gn, l, tm=TM_PROJ, n_seq=n_p,
                                                  seq_len=t_p)
        hp = _merge_ffn(hp, og, xp, u, vv, gates, layer_wts, sg_p, nfin, l, tm=TM_MERGE, final_norm=final,
                        pool_seq_len=t_p)
        gla_p.append(s_fin)
        pool_p.append(xp.reshape(n_p, t_p, POOL_WIDTH)[:, t_p - POOL_HIST:, :])

        qk, vg, xp, u, vv, gates, b2 = _inproj(hs, nmix, w_r, wgk, bgk, l, tm=TM_PROJ, seg=t_s, vv_dtype=F32)
        og, gla_s = _gla_sample(qk, vg, b2, state_gla, gn, l, gla_s, seq_len=t_s)
        pooled, new_hist = _pool_sample(xp.reshape(n_s, t_s, POOL_WIDTH), hist16[l], n_prev=n_prev_s, nb=16)
        hs = _merge_ffn(hs, og, pooled.reshape(n_s * t_s, POOL_WIDTH), u, vv, gates, layer_wts, sg_s, nfin, l,
                        tm=TM_MERGE, final_norm=final)
        pool_s.append(new_hist[:, 1:, :])
        sgv_s.append(vv.reshape(n_s, t_s, SG_WIDTH))

    return (hp.reshape(n_p, t_p, D_MODEL), hs.reshape(n_s, t_s, D_MODEL), jnp.stack(gla_p), gla_s,
            jnp.stack(pool_p), jnp.stack(pool_s), jnp.stack(sgv_s))
```

```python
import functools
import math

import jax
import jax.numpy as jnp
import numpy as np
from jax import lax
from jax.experimental import pallas as pl
from jax.experimental.pallas import tpu as pltpu

F32 = jnp.float32
BF16 = jnp.bfloat16

D_MODEL = 1024
GLA_HEADS = 4
GLA_KEY_DIM = 512
GLA_VAL_DIM = 1024
GLA_HEAD_K = 128
GLA_HEAD_V = 256
GLA_GATE_RANK = 16
GLA_GATE_NORMALIZER = 16.0
POOL_WIDTH = 512
POOL_WINDOWS = (2, 4, 8, 16)
POOL_GROUPS = 4
POOL_GROUP_DIM = 128
POOL_HIST = 15
SG_WIDTH = 512
SG_GROUPS = 4
SG_GROUP_DIM = 128
SG_CHUNK = 128
N_BRANCHES = 3
D_FF = 4096
EPS = 1e-6

SUBLANES = 8
LANES = 128
VMEM_LIMIT_BYTES = 56 * 1024 * 1024

C_QK = (0, 1024)
C_V = (1024, 2048)
C_GOUT = (2048, 3072)
C_XP = (3072, 3584)
C_U = (3584, 4096)
C_VV = (4096, 4608)
C_GATES = (4608, 7680)
C_GLR = (7680, 7808)
IN_COLS_R = 7808

TM_PROJ = 256
TM_MERGE = 512
PROJ_SLICE = 256
HEADS_IN_FLIGHT = 1
GLA_CHUNK = 128
GLA_BLOCK = SUBLANES
SAMPLE_SEQS_PER_STEP = 16
LOG2_DECAY_SCALE = math.log2(math.e) / GLA_GATE_NORMALIZER


def _resident(shape, layer=None):
    nd = len(shape)
    if layer is None:
        return pl.BlockSpec(shape, lambda *_: (0,) * nd, pipeline_mode=pl.Buffered(1))
    return pl.BlockSpec((None,) + tuple(shape), lambda *_: (layer,) + (0,) * nd,
                        pipeline_mode=pl.Buffered(1))


def _rms(x, w):
    return x * lax.rsqrt(jnp.mean(x * x, axis=-1, keepdims=True) + EPS) * w


def _dot(a, b):
    return jnp.dot(a, b, preferred_element_type=F32)


def _dot_nt(a, b):
    return lax.dot_general(a, b, (((1,), (1,)), ((), ())), preferred_element_type=F32)


def _row_iota(c, w):
    return lax.broadcasted_iota(jnp.int32, (c, w), 0)


def _bcast_row(x, blk, r):
    c, w = x.shape
    x3 = x.reshape(c // blk, blk, w)
    return jnp.broadcast_to(x3[:, r:r + 1, :], x3.shape).reshape(c, w)


def _seg_cumsum(g, seg):
    c, w = g.shape
    tpos = _row_iota(c, w) % SUBLANES
    for sh in (1, 2, 4):
        g = g + jnp.where(tpos >= sh, pltpu.roll(g, sh, 0), 0.0)
    if seg > SUBLANES:
        nb = seg // SUBLANES
        y = _bcast_row(g, SUBLANES, SUBLANES - 1).reshape(c // seg, nb, SUBLANES, w)
        sh = 1
        while sh < nb:
            y = y + jnp.concatenate([jnp.zeros((c // seg, sh, SUBLANES, w), F32), y[:, :nb - sh]], axis=1)
            sh *= 2
        excl = jnp.concatenate([jnp.zeros((c // seg, 1, SUBLANES, w), F32), y[:, :nb - 1]], axis=1)
        g = g + excl.reshape(c, w)
    return g


def _inproj_body(h_ref, nw_ref, w_ref, wgk_ref, bgk_ref,
                 qk_ref, vg_ref, xp_ref, u_ref, vv_ref, gt_ref, b2_ref, *, seg):
    xn = _rms(h_ref[...], nw_ref[...]).astype(BF16)

    def proj(lo, hi):
        return _dot(xn, w_ref[:, lo:hi])

    g2 = jax.nn.log_sigmoid(_dot(proj(*C_GLR).astype(BF16), wgk_ref[...]) + bgk_ref[...]) * LOG2_DECAY_SCALE
    b2_ref[...] = _seg_cumsum(g2, seg)
    vg_ref[:, GLA_VAL_DIM:] = jax.nn.silu(proj(*C_GOUT)).astype(BF16)
    u_ref[...] = jax.nn.gelu(proj(*C_U)).astype(u_ref.dtype)
    vv_ref[...] = jax.nn.gelu(proj(*C_VV)).astype(vv_ref.dtype)
    for j in range(N_BRANCHES):
        lo = C_GATES[0] + j * D_MODEL
        gt_ref[:, j * D_MODEL:(j + 1) * D_MODEL] = jax.nn.sigmoid(proj(lo, lo + D_MODEL)).astype(BF16)
    xp_ref[...] = proj(*C_XP)
    vg_ref[:, :GLA_VAL_DIM] = proj(*C_V).astype(BF16)
    qk_ref[...] = proj(*C_QK).astype(BF16)


def _inproj(h, norm_w, w_r, wgk, bgk, layer, *, tm, seg, vv_dtype):
    m = h.shape[0]
    row = lambda w: pl.BlockSpec((tm, w), lambda i: (i, 0))
    out_shape = (
        jax.ShapeDtypeStruct((m, 2 * GLA_KEY_DIM), BF16),
        jax.ShapeDtypeStruct((m, 2 * GLA_VAL_DIM), BF16),
        jax.ShapeDtypeStruct((m, POOL_WIDTH), F32),
        jax.ShapeDtypeStruct((m, SG_WIDTH), BF16),
        jax.ShapeDtypeStruct((m, SG_WIDTH), vv_dtype),
        jax.ShapeDtypeStruct((m, N_BRANCHES * D_MODEL), BF16),
        jax.ShapeDtypeStruct((m, GLA_KEY_DIM), F32),
    )
    return pl.pallas_call(
        functools.partial(_inproj_body, seg=seg),
        grid=(m // tm,),
        in_specs=[row(D_MODEL), _resident((1, D_MODEL), layer), _resident((D_MODEL, IN_COLS_R), layer),
                  _resident((LANES, GLA_KEY_DIM), layer), _resident((1, GLA_KEY_DIM), layer)],
        out_specs=[row(2 * GLA_KEY_DIM), row(2 * GLA_VAL_DIM), row(POOL_WIDTH), row(SG_WIDTH), row(SG_WIDTH),
                   row(N_BRANCHES * D_MODEL), row(GLA_KEY_DIM)],
        out_shape=out_shape,
        compiler_params=pltpu.CompilerParams(
            dimension_semantics=("arbitrary",), vmem_limit_bytes=VMEM_LIMIT_BYTES),
        name="inproj",
    )(h, norm_w, w_r, wgk, bgk)


CODE_BLOCK = 1


def _score_codes(c, levels):
    t = np.arange(c)[:, None]
    s = np.arange(c)[None, :]
    code = np.zeros((c, c), np.int32)
    code = np.where((t // GLA_BLOCK == s // GLA_BLOCK) & (s <= t), CODE_BLOCK, code)
    if levels:
        lvl, j = 2 * GLA_BLOCK, 0
        while lvl <= c:
            hit = (t // lvl == s // lvl) & (t // (lvl // 2) != s // (lvl // 2)) & (s < t)
            code = np.where(hit, CODE_BLOCK + 1 + j, code)
            lvl, j = lvl * 2, j + 1
    return jnp.asarray(code, jnp.int32)


def _block_sum_matrix(c):
    r = np.arange(GLA_BLOCK * GLA_HEAD_K)[:, None] // GLA_HEAD_K
    s = np.arange(c)[None, :]
    return jnp.asarray(r == s % GLA_BLOCK, BF16)


def _level_signs(c):
    t = np.arange(c)[:, None]
    out = []
    lvl = 2 * GLA_BLOCK
    while lvl <= c:
        out.append(np.broadcast_to(np.where(t % lvl >= lvl // 2, 1.0, -1.0), (c, GLA_HEAD_K)))
        lvl *= 2
    return jnp.asarray(np.stack(out), F32)


def _rows_bcast(ref, rows, lo, reps):
    parts = []
    for r in rows:
        tile = jnp.broadcast_to(ref[pl.ds(r, 1), lo:lo + GLA_HEAD_K], (SUBLANES, GLA_HEAD_K))
        parts.extend([tile] * reps)
    return jnp.concatenate(parts, axis=0)


def _run(steps):
    try:
        while True:
            next(steps)
    except StopIteration as stop:
        return stop.value


def _gla_scores_steps(q_ref, k_ref, b_ref, sel_ref, code_ref, r0, ks, c, sgn_ref=None):
    tile = lambda ref: ref[r0:r0 + c, ks:ks + GLA_HEAD_K]
    att = jnp.zeros((c, c), F32)
    if sgn_ref is not None:
        lvl, j = 2 * GLA_BLOCK, 0
        while lvl <= c:
            half = lvl // 2
            bmid = _rows_bcast(b_ref, [r0 + i * lvl + half - 1 for i in range(c // lvl)], ks, lvl // SUBLANES)
            e = jnp.exp2((tile(b_ref) - bmid) * sgn_ref[j])
            att_l = _dot_nt((tile(q_ref) * e).astype(BF16), (tile(k_ref) * e).astype(BF16))
            att = jnp.where(code_ref[...] == CODE_BLOCK + 1 + j, att_l, att)
            lvl, j = lvl * 2, j + 1
            yield
    nblk = c // GLA_BLOCK
    parts = []
    for r in range(GLA_BLOCK):
        rows = [r0 + i * GLA_BLOCK + r for i in range(nblk)]
        br = _rows_bcast(b_ref, rows, ks, 1)
        kr = _rows_bcast(k_ref, rows, ks, 1)
        parts.append((tile(q_ref) * kr * jnp.exp2(jnp.minimum(tile(b_ref) - br, 0.0))).astype(BF16))
        yield
    same_block = _dot(jnp.concatenate(parts, axis=1), sel_ref[...])
    return jnp.where(code_ref[...] == CODE_BLOCK, same_block, att)


def _inproj_gla_body(h_ref, nw_ref, w_ref, wgk_ref, bgk_ref, code_ref, sgn_ref, sel_ref, gn_ref,
                     og_ref, xp_ref, u_ref, vv_ref, gt_ref, sfin_ref,
                     s_scr, q_scr, k_scr, v_scr, g_scr, b_scr, *, tiles_per_seq):
    i = pl.program_id(0)
    tm = h_ref.shape[0]
    c = GLA_CHUNK

    @pl.when(i % tiles_per_seq == 0)
    def _():
        s_scr[...] = jnp.zeros_like(s_scr)

    xn = _rms(h_ref[...], nw_ref[...]).astype(BF16)

    def proj(lo, hi):
        return _dot(xn, w_ref[:, lo:hi])

    def slice_task(lo, dst_ref, dst_lo, act):
        def run():
            dst_ref[:, dst_lo:dst_lo + PROJ_SLICE] = act(proj(lo, lo + PROJ_SLICE)).astype(dst_ref.dtype)
        return run

    def slice_tasks(cols, dst_ref, act=lambda x: x):
        return [slice_task(lo, dst_ref, lo - cols[0], act) for lo in range(cols[0], cols[1], PROJ_SLICE)]

    q_scale = GLA_HEAD_K ** -0.5
    head_tasks = (slice_tasks(C_GOUT, g_scr, jax.nn.silu) + slice_tasks(C_V, v_scr)
                  + slice_tasks((C_QK[0], C_QK[0] + GLA_KEY_DIM), q_scr, lambda x: x * q_scale)
                  + slice_tasks((C_QK[0] + GLA_KEY_DIM, C_QK[1]), k_scr))
    tails = (slice_tasks(C_U, u_ref, jax.nn.gelu) + slice_tasks(C_VV, vv_ref, jax.nn.gelu)
             + slice_tasks(C_GATES, gt_ref, jax.nn.sigmoid) + slice_tasks(C_XP, xp_ref))

    glr = proj(*C_GLR).astype(BF16)
    g2 = jax.nn.log_sigmoid(_dot(glr, wgk_ref[...]) + bgk_ref[...]) * LOG2_DECAY_SCALE
    per_head = len(head_tasks) // GLA_HEADS
    for h in range(GLA_HEADS):
        ks = h * GLA_HEAD_K
        b_scr[:, ks:ks + GLA_HEAD_K] = _seg_cumsum(g2[:, ks:ks + GLA_HEAD_K], c)
        for t in head_tasks[h * per_head:(h + 1) * per_head]:
            t()

    def pump(n=1):
        for _ in range(min(n, len(tails))):
            tails.pop(0)()

    def head_steps(r0, h):
        ks, kv = h * GLA_HEAD_K, h * GLA_HEAD_V
        tile = lambda ref: ref[r0:r0 + c, ks:ks + GLA_HEAD_K]
        v_bf = lambda: v_scr[r0:r0 + c, kv:kv + GLA_HEAD_V]
        s = s_scr[h]
        o = _dot((tile(q_scr) * jnp.exp2(tile(b_scr))).astype(BF16), s.astype(BF16))
        b_t = tile(b_scr).T
        b_last = b_t[:, c - 1:c]
        kd_t = (tile(k_scr).T * jnp.exp2(b_last - b_t)).astype(BF16)
        s_scr[h] = jnp.exp2(b_last) * s + _dot(kd_t, v_bf())
        yield
        att = yield from _gla_scores_steps(q_scr, k_scr, b_scr, sel_ref, code_ref, r0, ks, c, sgn_ref)
        o = o + _dot(att.astype(BF16), v_bf())
        yield
        gate = g_scr[r0:r0 + c, kv:kv + GLA_HEAD_V].astype(F32)
        og_ref[r0:r0 + c, kv:kv + GLA_HEAD_V] = (_rms(o, gn_ref[...]) * gate).astype(BF16)

    for ch, h0 in [(ch, h0) for ch in range(tm // c) for h0 in range(0, GLA_HEADS, HEADS_IN_FLIGHT)]:
        active = [head_steps(ch * c, h) for h in range(h0, h0 + HEADS_IN_FLIGHT)]
        while active:
            for steps in list(active):
                try:
                    next(steps)
                except StopIteration:
                    active.remove(steps)
            pump()
    pump(len(tails))

    @pl.when(i % tiles_per_seq == tiles_per_seq - 1)
    def _():
        sfin_ref[0] = s_scr[...]


def _inproj_gla(h, norm_w, w_r, wgk, bgk, gn, layer, *, tm, n_seq, seq_len):
    m = h.shape[0]
    c = GLA_CHUNK
    tiles_per_seq = seq_len // tm
    row = lambda w: pl.BlockSpec((tm, w), lambda i: (i, 0))
    out_shape = (
        jax.ShapeDtypeStruct((m, GLA_VAL_DIM), BF16),
        jax.ShapeDtypeStruct((m, POOL_WIDTH), F32),
        jax.ShapeDtypeStruct((m, SG_WIDTH), BF16),
        jax.ShapeDtypeStruct((m, SG_WIDTH), BF16),
        jax.ShapeDtypeStruct((m, N_BRANCHES * D_MODEL), BF16),
        jax.ShapeDtypeStruct((n_seq, GLA_HEADS, GLA_HEAD_K, GLA_HEAD_V), F32),
    )
    code, sgn, sel = _score_codes(c, True), _level_signs(c), _block_sum_matrix(c)
    return pl.pallas_call(
        functools.partial(_inproj_gla_body, tiles_per_seq=tiles_per_seq),
        grid=(m // tm,),
        in_specs=[row(D_MODEL), _resident((1, D_MODEL), layer), _resident((D_MODEL, IN_COLS_R), layer),
                  _resident((LANES, GLA_KEY_DIM), layer), _resident((1, GLA_KEY_DIM), layer),
                  _resident(code.shape), _resident(sgn.shape), _resident(sel.shape),
                  _resident((1, GLA_HEAD_V), layer)],
        out_specs=[row(GLA_VAL_DIM), row(POOL_WIDTH), row(SG_WIDTH), row(SG_WIDTH), row(N_BRANCHES * D_MODEL),
                   pl.BlockSpec((1, GLA_HEADS, GLA_HEAD_K, GLA_HEAD_V), lambda i: (i // tiles_per_seq, 0, 0, 0))],
        out_shape=out_shape,
        scratch_shapes=[pltpu.VMEM((GLA_HEADS, GLA_HEAD_K, GLA_HEAD_V), F32),
                        pltpu.VMEM((tm, GLA_KEY_DIM), F32), pltpu.VMEM((tm, GLA_KEY_DIM), F32),
                        pltpu.VMEM((tm, GLA_VAL_DIM), BF16), pltpu.VMEM((tm, GLA_VAL_DIM), BF16),
                        pltpu.VMEM((tm, GLA_KEY_DIM), F32)],
        compiler_params=pltpu.CompilerParams(
            dimension_semantics=("arbitrary",), vmem_limit_bytes=VMEM_LIMIT_BYTES),
        name="inproj_gla",
    )(h, norm_w, w_r, wgk, bgk, code, sgn, sel, gn)


def _gla_sample_body(qk_ref, vg_ref, b_ref, s0_ref, code_ref, sel_ref, gn_ref, *rest, seq_len):
    og_ref, snew_ref, q_scr, k_scr = rest[-4:]
    nb = SAMPLE_SEQS_PER_STEP
    c = nb * seq_len
    q_scr[...] = qk_ref[:, :GLA_KEY_DIM].astype(F32) * (GLA_HEAD_K ** -0.5)
    k_scr[...] = qk_ref[:, GLA_KEY_DIM:].astype(F32)
    gn = gn_ref[...]
    lane_seq = lax.broadcasted_iota(jnp.int32, (GLA_HEAD_K, c), 1) // seq_len
    last_rows = [(n + 1) * seq_len - 1 for n in range(nb)]
    for h in range(GLA_HEADS):
        ks, kv = h * GLA_HEAD_K, h * GLA_HEAD_V
        k = k_scr[:, ks:ks + GLA_HEAD_K]
        b = b_ref[:, ks:ks + GLA_HEAD_K]
        v_bf = vg_ref[:, kv:kv + GLA_HEAD_V]
        gate = vg_ref[:, GLA_VAL_DIM + kv:GLA_VAL_DIM + kv + GLA_HEAD_V].astype(F32)
        qi = q_scr[:, ks:ks + GLA_HEAD_K] * jnp.exp2(b)
        inter = [
            _dot(qi[n * seq_len:(n + 1) * seq_len, :].astype(BF16), s0_ref[n, h].astype(BF16))
            for n in range(nb)
        ]
        att = _run(_gla_scores_steps(q_scr, k_scr, b_ref, sel_ref, code_ref, 0, ks, c))
        o = jnp.concatenate(inter, axis=0) + _dot(att.astype(BF16), v_bf)
        og_ref[:, kv:kv + GLA_HEAD_V] = (_rms(o, gn) * gate).astype(BF16)

        b_last_rows = _rows_bcast(b_ref, last_rows, ks, 1)
        kd_t = (k * jnp.exp2(b_last_rows - b)).T
        b_t = b.T
        for n in range(nb):
            col = last_rows[n]
            decay = jnp.exp2(b_t[:, col:col + 1])
            kd_n = jnp.where(lane_seq == n, kd_t, 0.0).astype(BF16)
            snew_ref[n, h] = decay * s0_ref[n, h] + _dot(kd_n, v_bf)


def _gla_sample(qk, vg, b2, state_all, gn, layer, prev_new_state, *, seq_len):
    assert seq_len == GLA_BLOCK
    nb = SAMPLE_SEQS_PER_STEP
    n_seq = state_all.shape[1]
    c = nb * seq_len
    row = lambda w: pl.BlockSpec((c, w), lambda i: (i, 0))
    st = pl.BlockSpec((None, nb, GLA_HEADS, GLA_HEAD_K, GLA_HEAD_V), lambda i: (layer, i, 0, 0, 0))
    in_specs = [row(2 * GLA_KEY_DIM), row(2 * GLA_VAL_DIM), row(GLA_KEY_DIM), st, _resident((c, c)),
                _resident((GLA_BLOCK * GLA_HEAD_K, c)), _resident((1, GLA_HEAD_V), layer)]
    args = [qk, vg, b2, state_all, _score_codes(c, False), _block_sum_matrix(c), gn]
    aliases = {}
    if prev_new_state is not None:
        in_specs.append(pl.BlockSpec(memory_space=pl.ANY))
        args.append(prev_new_state)
        aliases = {len(args) - 1: 1}
    return pl.pallas_call(
        functools.partial(_gla_sample_body, seq_len=seq_len),
        grid=(n_seq // nb,),
        in_specs=in_specs,
        out_specs=[row(GLA_VAL_DIM), st],
        out_shape=(jax.ShapeDtypeStruct((n_seq * seq_len, GLA_VAL_DIM), BF16),
                   jax.ShapeDtypeStruct(state_all.shape, F32)),
        scratch_shapes=[pltpu.VMEM((c, GLA_KEY_DIM), F32), pltpu.VMEM((c, GLA_KEY_DIM), F32)],
        input_output_aliases=aliases,
        compiler_params=pltpu.CompilerParams(
            dimension_semantics=("arbitrary",), vmem_limit_bytes=VMEM_LIMIT_BYTES),
        name="gla_sample",
    )(*args)


def _window_sums(full):
    outs = []
    s = full
    w = 1
    for gi, win in enumerate(POOL_WINDOWS):
        while w < win:
            s = s + pltpu.roll(s, w, 0)
            w *= 2
        outs.append(s[:, gi * POOL_GROUP_DIM:(gi + 1) * POOL_GROUP_DIM])
    return jnp.concatenate(outs, axis=1)


def _pool_counts(pos, n_prev):
    grp = lax.broadcasted_iota(jnp.int32, pos.shape, pos.ndim - 1) // POOL_GROUP_DIM
    win = jnp.zeros(pos.shape, jnp.int32)
    for gi, w in enumerate(POOL_WINDOWS):
        win = jnp.where(grp == gi, w, win)
    return jnp.minimum(win, pos + 1 + n_prev).astype(F32)


POOL_PREV_ROWS = 2 * SUBLANES


def _pool_prompt_tile(x_ref, prev_ref, tile_in_seq):
    tp = x_ref.shape[0]
    x = x_ref[...]
    prev = jnp.where(tile_in_seq == 0, 0.0, prev_ref[...])
    sums = _window_sums(jnp.concatenate([prev, x], axis=0))[POOL_PREV_ROWS:, :]
    pos = _row_iota(tp, POOL_WIDTH) + tile_in_seq * tp
    return (sums / _pool_counts(pos, 0) - x).astype(BF16)


def _pool_sample_body(x_ref, hist_ref, o_ref, nh_ref, *, n_prev):
    nb, t, _ = x_ref.shape
    hr = hist_ref.shape[1]
    x = x_ref[...]
    full = jnp.concatenate([hist_ref[...], x], axis=1)
    sums = _window_sums(full.reshape(nb * (hr + t), POOL_WIDTH)).reshape(nb, hr + t, POOL_WIDTH)
    pos = lax.broadcasted_iota(jnp.int32, (nb, t, POOL_WIDTH), 1)
    o_ref[...] = (sums[:, hr:, :] / _pool_counts(pos, n_prev) - x).astype(BF16)
    nh_ref[...] = full[:, t:, :]


def _pool_sample(xp3, hist16, *, n_prev, nb):
    n, t, _ = xp3.shape
    hr = hist16.shape[1]
    blk = lambda r: pl.BlockSpec((nb, r, POOL_WIDTH), lambda i: (i, 0, 0))
    return pl.pallas_call(
        functools.partial(_pool_sample_body, n_prev=n_prev),
        grid=(n // nb,),
        in_specs=[blk(t), blk(hr)],
        out_specs=[blk(t), blk(hr)],
        out_shape=(jax.ShapeDtypeStruct((n, t, POOL_WIDTH), BF16),
                   jax.ShapeDtypeStruct((n, hr, POOL_WIDTH), F32)),
        compiler_params=pltpu.CompilerParams(dimension_semantics=("arbitrary",)),
        name="pool_sample",
    )(xp3, hist16)


def _merge_ffn_body(h_ref, og_ref, u_ref, vv_ref, gt_ref, *rest, final_norm, pool_tiles_per_seq):
    if pool_tiles_per_seq is None:
        pl_ref, rest = rest[0], rest[1:]
        pooled = lambda g: pl_ref[:, g * POOL_GROUP_DIM:(g + 1) * POOL_GROUP_DIM]
    else:
        xp_ref, prev_ref, rest = rest[0], rest[1], rest[2:]
        pooled_all = _pool_prompt_tile(xp_ref, prev_ref, pl.program_id(0) % pool_tiles_per_seq)
        pooled = lambda g: pooled_all[:, g * POOL_GROUP_DIM:(g + 1) * POOL_GROUP_DIM]
    (wmix_ref, pscale_ref, wsg_ref, bsg_ref, wa_ref, wb_ref, wc_ref, wo_ref,
     nf_ref, w1_ref, w2_ref, nfin_ref, o_ref) = rest
    tm = h_ref.shape[0]
    br = gt_ref[:, 0:D_MODEL].astype(F32) * _dot(og_ref[...], wa_ref[...])
    mixed = jnp.concatenate([_dot(pooled(g), wmix_ref[g]) for g in range(POOL_GROUPS)], axis=1)
    pool_out = (mixed * pscale_ref[...]).astype(BF16)
    br = br + gt_ref[:, D_MODEL:2 * D_MODEL].astype(F32) * _dot(pool_out, wb_ref[...])
    rows = []
    for c in range(tm // SG_CHUNK):
        r0 = c * SG_CHUNK
        vv = vv_ref[r0:r0 + SG_CHUNK, :].astype(BF16)
        sg = jnp.concatenate(
            [_dot(wsg_ref[g], vv[:, g * SG_GROUP_DIM:(g + 1) * SG_GROUP_DIM]) for g in range(SG_GROUPS)],
            axis=1) + bsg_ref[...]
        rows.append((u_ref[r0:r0 + SG_CHUNK, :].astype(F32) * sg).astype(BF16))
    sg_out = jnp.concatenate(rows, axis=0)
    br = br + gt_ref[:, 2 * D_MODEL:3 * D_MODEL].astype(F32) * _dot(sg_out, wc_ref[...])
    h = h_ref[...] + _dot(br.astype(BF16), wo_ref[...])
    a = jnp.maximum(_dot(_rms(h, nf_ref[...]).astype(BF16), w1_ref[...]), 0.0)
    h = h + _dot((a * a).astype(BF16), w2_ref[...])
    if final_norm:
        h = _rms(h, nfin_ref[...])
    o_ref[...] = h


def _merge_ffn(h, og, pool_in, u, vv, gates, layer_wts, sg_wts, nfin, layer, *, tm, final_norm, pool_seq_len=None):
    m = h.shape[0]
    row = lambda w: pl.BlockSpec((tm, w), lambda i: (i, 0))
    (wmix, pscale, wa, wb, wc, wo, nf, w1, w2) = layer_wts
    wsg, bsg = sg_wts
    lw = lambda w: _resident(w.shape[1:], layer)
    if pool_seq_len is None:
        pool_specs, pool_args, tiles_per_seq = [row(POOL_WIDTH)], [pool_in], None
    else:
        ratio = tm // POOL_PREV_ROWS
        pool_specs = [row(POOL_WIDTH),
                      pl.BlockSpec((POOL_PREV_ROWS, POOL_WIDTH), lambda i: (jnp.maximum(i * ratio - 1, 0), 0))]
        pool_args, tiles_per_seq = [pool_in, pool_in], pool_seq_len // tm
    return pl.pallas_call(
        functools.partial(_merge_ffn_body, final_norm=final_norm, pool_tiles_per_seq=tiles_per_seq),
        grid=(m // tm,),
        in_specs=[row(D_MODEL), row(GLA_VAL_DIM), row(SG_WIDTH), row(SG_WIDTH), row(N_BRANCHES * D_MODEL)]
        + pool_specs
        + [lw(wmix), lw(pscale), lw(wsg), lw(bsg), lw(wa), lw(wb), lw(wc), lw(wo), lw(nf), lw(w1), lw(w2),
           _resident(nfin.shape)],
        out_specs=row(D_MODEL),
        out_shape=jax.ShapeDtypeStruct((m, D_MODEL), F32),
        compiler_params=pltpu.CompilerParams(
            dimension_semantics=("arbitrary",), vmem_limit_bytes=VMEM_LIMIT_BYTES),
        name="merge_ffn",
    )(h, og, u, vv, gates, *pool_args, wmix, pscale, wsg, bsg, wa, wb, wc, wo, nf, w1, w2, nfin)


def _reorder_w_in_body(w_ref, o_ref):
    glr0 = 2 * GLA_KEY_DIM + 2 * GLA_VAL_DIM
    glr1 = glr0 + GLA_GATE_RANK
    rows, cols = w_ref.shape
    o_ref[:, :glr0] = w_ref[:, :glr0].astype(BF16)
    o_ref[:, glr0:C_GLR[0]] = w_ref[:, glr1:].astype(BF16)
    o_ref[:, C_GLR[0]:C_GLR[0] + GLA_GATE_RANK] = w_ref[:, glr0:glr1].astype(BF16)
    o_ref[:, C_GLR[0] + GLA_GATE_RANK:] = jnp.zeros((rows, LANES - GLA_GATE_RANK), BF16)


def _reorder_w_in(w):
    depth, rows, cols = w.shape
    tr = 256
    return pl.pallas_call(
        _reorder_w_in_body,
        grid=(depth, rows // tr),
        in_specs=[pl.BlockSpec((None, tr, cols), lambda l, i: (l, i, 0))],
        out_specs=pl.BlockSpec((None, tr, IN_COLS_R), lambda l, i: (l, i, 0)),
        out_shape=jax.ShapeDtypeStruct((depth, rows, IN_COLS_R), BF16),
        compiler_params=pltpu.CompilerParams(dimension_semantics=("arbitrary", "arbitrary")),
        name="reorder_w_in",
    )(w)


def _sg_operands(w_spatial, b_spatial, length):
    reps = SG_CHUNK // length
    t = jnp.arange(SG_CHUNK)
    w = jnp.tile(w_spatial[:, :, :length, :length], (1, 1, reps, reps))
    keep = ((t[:, None] // length) == (t[None, :] // length)) & ((t[None, :] % length) <= (t[:, None] % length))
    w = jnp.where(keep, w, 0.0).astype(BF16)
    bias = jnp.repeat(jnp.swapaxes(jnp.tile(b_spatial[:, :, :length], (1, 1, reps)), 1, 2), SG_GROUP_DIM, axis=2)
    return w, bias.astype(F32)


def kernel(x_prompt, x_sample, state_gla, state_pool, norm_mix, w_in, w_gk2, b_gk, gla_norm, w_pool_mix,
           pool_scale, w_spatial, b_spatial, w_br_a, w_br_b, w_br_c, w_out, norm_ffn, w_ff1, w_ff2,
           norm_final):
    depth = w_in.shape[0]
    n_p, t_p, _ = x_prompt.shape
    n_s, t_s, _ = x_sample.shape
    hp = x_prompt.reshape(n_p * t_p, D_MODEL)
    hs = x_sample.reshape(n_s * t_s, D_MODEL)
    n_prev_s = POOL_HIST
    rows = lambda a: a.reshape(depth, 1, -1).astype(F32)
    bf = lambda a: a.astype(BF16)

    w_r = _reorder_w_in(w_in)
    wgk = bf(jnp.pad(w_gk2, ((0, 0), (0, LANES - GLA_GATE_RANK), (0, 0))))
    bgk, gn, nmix = rows(b_gk), rows(gla_norm), rows(norm_mix)
    layer_wts = (bf(w_pool_mix), rows(pool_scale), bf(w_br_a), bf(w_br_b), bf(w_br_c), bf(w_out),
                 rows(norm_ffn), bf(w_ff1), bf(w_ff2))
    sg_p = _sg_operands(w_spatial, b_spatial, min(t_p, SG_CHUNK))
    sg_s = _sg_operands(w_spatial, b_spatial, min(t_s, SG_CHUNK))
    nfin = norm_final.reshape(1, -1).astype(F32)
    hist16 = jnp.pad(state_pool, ((0, 0), (0, 0), (1, 0), (0, 0)))

    gla_p, pool_p, pool_s, sgv_s = [], [], [], []
    gla_s = None
    for l in range(depth):
        final = l == depth - 1
        og, xp, u, vv, gates, s_fin = _inproj_gla(hp, nmix, w_r, wgk, bgk, gn, l, tm=TM_PROJ, n_seq=n_p,
                                                  seq_len=t_p)
        hp = _merge_ffn(hp, og, xp, u, vv, gates, layer_wts, sg_p, nfin, l, tm=TM_MERGE, final_norm=final,
                        pool_seq_len=t_p)
        gla_p.append(s_fin)
        pool_p.append(xp.reshape(n_p, t_p, POOL_WIDTH)[:, t_p - POOL_HIST:, :])

        qk, vg, xp, u, vv, gates, b2 = _inproj(hs, nmix, w_r, wgk, bgk, l, tm=TM_PROJ, seg=t_s, vv_dtype=F32)
        og, gla_s = _gla_sample(qk, vg, b2, state_gla, gn, l, gla_s, seq_len=t_s)
        pooled, new_hist = _pool_sample(xp.reshape(n_s, t_s, POOL_WIDTH), hist16[l], n_prev=n_prev_s, nb=16)
        hs = _merge_ffn(hs, og, pooled.reshape(n_s * t_s, POOL_WIDTH), u, vv, gates, layer_wts, sg_s, nfin, l,
                        tm=TM_MERGE, final_norm=final)
        pool_s.append(new_hist[:, 1:, :])
        sgv_s.append(vv.reshape(n_s, t_s, SG_WIDTH))

    return (hp.reshape(n_p, t_p, D_MODEL), hs.reshape(n_s, t_s, D_MODEL), jnp.stack(gla_p), gla_s,
            jnp.stack(pool_p), jnp.stack(pool_s), jnp.stack(sgv_s))
```

```python
import functools
import math

import jax
import jax.numpy as jnp
import numpy as np
from jax import lax
from jax.experimental import pallas as pl
from jax.experimental.pallas import tpu as pltpu

F32 = jnp.float32
BF16 = jnp.bfloat16

D_MODEL = 1024
GLA_HEADS = 4
GLA_KEY_DIM = 512
GLA_VAL_DIM = 1024
GLA_HEAD_K = 128
GLA_HEAD_V = 256
GLA_GATE_RANK = 16
GLA_GATE_NORMALIZER = 16.0
POOL_WIDTH = 512
POOL_WINDOWS = (2, 4, 8, 16)
POOL_GROUPS = 4
POOL_GROUP_DIM = 128
POOL_HIST = 15
SG_WIDTH = 512
SG_GROUPS = 4
SG_GROUP_DIM = 128
SG_CHUNK = 128
N_BRANCHES = 3
D_FF = 4096
EPS = 1e-6

SUBLANES = 8
LANES = 128
VMEM_LIMIT_BYTES = 56 * 1024 * 1024

C_QK = (0, 1024)
C_V = (1024, 2048)
C_GOUT = (2048, 3072)
C_XP = (3072, 3584)
C_U = (3584, 4096)
C_VV = (4096, 4608)
C_GATES = (4608, 7680)
C_GLR = (7680, 7808)
IN_COLS_R = 7808

TM_PROJ = 256
TM_MERGE = 512
PROJ_SLICE = 256
HEADS_IN_FLIGHT = 1
GLA_CHUNK = 128
GLA_BLOCK = SUBLANES
SAMPLE_SEQS_PER_STEP = 16
LOG2_DECAY_SCALE = math.log2(math.e) / GLA_GATE_NORMALIZER


def _resident(shape, layer=None):
    nd = len(shape)
    if layer is None:
        return pl.BlockSpec(shape, lambda *_: (0,) * nd, pipeline_mode=pl.Buffered(1))
    return pl.BlockSpec((None,) + tuple(shape), lambda *_: (layer,) + (0,) * nd,
                        pipeline_mode=pl.Buffered(1))


def _rms(x, w):
    return x * lax.rsqrt(jnp.mean(x * x, axis=-1, keepdims=True) + EPS) * w


def _dot(a, b):
    return jnp.dot(a, b, preferred_element_type=F32)


def _dot_nt(a, b):
    return lax.dot_general(a, b, (((1,), (1,)), ((), ())), preferred_element_type=F32)


def _row_iota(c, w):
    return lax.broadcasted_iota(jnp.int32, (c, w), 0)


def _bcast_row(x, blk, r):
    c, w = x.shape
    x3 = x.reshape(c // blk, blk, w)
    return jnp.broadcast_to(x3[:, r:r + 1, :], x3.shape).reshape(c, w)


def _seg_cumsum(g, seg):
    c, w = g.shape
    tpos = _row_iota(c, w) % SUBLANES
    for sh in (1, 2, 4):
        g = g + jnp.where(tpos >= sh, pltpu.roll(g, sh, 0), 0.0)
    if seg > SUBLANES:
        nb = seg // SUBLANES
        y = _bcast_row(g, SUBLANES, SUBLANES - 1).reshape(c // seg, nb, SUBLANES, w)
        sh = 1
        while sh < nb:
            y = y + jnp.concatenate([jnp.zeros((c // seg, sh, SUBLANES, w), F32), y[:, :nb - sh]], axis=1)
            sh *= 2
        excl = jnp.concatenate([jnp.zeros((c // seg, 1, SUBLANES, w), F32), y[:, :nb - 1]], axis=1)
        g = g + excl.reshape(c, w)
    return g


def _inproj_body(h_ref, nw_ref, w_ref, wgk_ref, bgk_ref,
                 qk_ref, vg_ref, xp_ref, u_ref, vv_ref, gt_ref, b2_ref, *, seg):
    xn = _rms(h_ref[...], nw_ref[...]).astype(BF16)

    def proj(lo, hi):
        return _dot(xn, w_ref[:, lo:hi])

    g2 = jax.nn.log_sigmoid(_dot(proj(*C_GLR).astype(BF16), wgk_ref[...]) + bgk_ref[...]) * LOG2_DECAY_SCALE
    b2_ref[...] = _seg_cumsum(g2, seg)
    vg_ref[:, GLA_VAL_DIM:] = jax.nn.silu(proj(*C_GOUT)).astype(BF16)
    u_ref[...] = jax.nn.gelu(proj(*C_U)).astype(u_ref.dtype)
    vv_ref[...] = jax.nn.gelu(proj(*C_VV)).astype(vv_ref.dtype)
    for j in range(N_BRANCHES):
        lo = C_GATES[0] + j * D_MODEL
        gt_ref[:, j * D_MODEL:(j + 1) * D_MODEL] = jax.nn.sigmoid(proj(lo, lo + D_MODEL)).astype(BF16)
    xp_ref[...] = proj(*C_XP)
    vg_ref[:, :GLA_VAL_DIM] = proj(*C_V).astype(BF16)
    qk_ref[...] = proj(*C_QK).astype(BF16)


def _inproj(h, norm_w, w_r, wgk, bgk, layer, *, tm, seg, vv_dtype):
    m = h.shape[0]
    row = lambda w: pl.BlockSpec((tm, w), lambda i: (i, 0))
    out_shape = (
        jax.ShapeDtypeStruct((m, 2 * GLA_KEY_DIM), BF16),
        jax.ShapeDtypeStruct((m, 2 * GLA_VAL_DIM), BF16),
        jax.ShapeDtypeStruct((m, POOL_WIDTH), F32),
        jax.ShapeDtypeStruct((m, SG_WIDTH), BF16),
        jax.ShapeDtypeStruct((m, SG_WIDTH), vv_dtype),
        jax.ShapeDtypeStruct((m, N_BRANCHES * D_MODEL), BF16),
        jax.ShapeDtypeStruct((m, GLA_KEY_DIM), F32),
    )
    return pl.pallas_call(
        functools.partial(_inproj_body, seg=seg),
        grid=(m // tm,),
        in_specs=[row(D_MODEL), _resident((1, D_MODEL), layer), _resident((D_MODEL, IN_COLS_R), layer),
                  _resident((LANES, GLA_KEY_DIM), layer), _resident((1, GLA_KEY_DIM), layer)],
        out_specs=[row(2 * GLA_KEY_DIM), row(2 * GLA_VAL_DIM), row(POOL_WIDTH), row(SG_WIDTH), row(SG_WIDTH),
                   row(N_BRANCHES * D_MODEL), row(GLA_KEY_DIM)],
        out_shape=out_shape,
        compiler_params=pltpu.CompilerParams(
            dimension_semantics=("arbitrary",), vmem_limit_bytes=VMEM_LIMIT_BYTES),
        name="inproj",
    )(h, norm_w, w_r, wgk, bgk)


CODE_BLOCK = 1


def _score_codes(c, levels):
    t = np.arange(c)[:, None]
    s = np.arange(c)[None, :]
    code = np.zeros((c, c), np.int32)
    code = np.where((t // GLA_BLOCK == s // GLA_BLOCK) & (s <= t), CODE_BLOCK, code)
    if levels:
        lvl, j = 2 * GLA_BLOCK, 0
        while lvl <= c:
            hit = (t // lvl == s // lvl) & (t // (lvl // 2) != s // (lvl // 2)) & (s < t)
            code = np.where(hit, CODE_BLOCK + 1 + j, code)
            lvl, j = lvl * 2, j + 1
    return jnp.asarray(code, jnp.int32)


def _block_sum_matrix(c):
    r = np.arange(GLA_BLOCK * GLA_HEAD_K)[:, None] // GLA_HEAD_K
    s = np.arange(c)[None, :]
    return jnp.asarray(r == s % GLA_BLOCK, BF16)


def _level_signs(c):
    t = np.arange(c)[:, None]
    out = []
    lvl = 2 * GLA_BLOCK
    while lvl <= c:
        out.append(np.broadcast_to(np.where(t % lvl >= lvl // 2, 1.0, -1.0), (c, GLA_HEAD_K)))
        lvl *= 2
    return jnp.asarray(np.stack(out), F32)


def _rows_bcast(ref, rows, lo, reps):
    parts = []
    for r in rows:
        tile = jnp.broadcast_to(ref[pl.ds(r, 1), lo:lo + GLA_HEAD_K], (SUBLANES, GLA_HEAD_K))
        parts.extend([tile] * reps)
    return jnp.concatenate(parts, axis=0)


def _run(steps):
    try:
        while True:
            next(steps)
    except StopIteration as stop:
        return stop.value


def _gla_scores_steps(q_ref, k_ref, b_ref, sel_ref, code_ref, r0, ks, c, sgn_ref=None):
    tile = lambda ref: ref[r0:r0 + c, ks:ks + GLA_HEAD_K]
    att = jnp.zeros((c, c), F32)
    if sgn_ref is not None:
        lvl, j = 2 * GLA_BLOCK, 0
        while lvl <= c:
            half = lvl // 2
            bmid = _rows_bcast(b_ref, [r0 + i * lvl + half - 1 for i in range(c // lvl)], ks, lvl // SUBLANES)
            e = jnp.exp2((tile(b_ref) - bmid) * sgn_ref[j])
            att_l = _dot_nt((tile(q_ref) * e).astype(BF16), (tile(k_ref) * e).astype(BF16))
            att = jnp.where(code_ref[...] == CODE_BLOCK + 1 + j, att_l, att)
            lvl, j = lvl * 2, j + 1
            yield
    nblk = c // GLA_BLOCK
    parts = []
    for r in range(GLA_BLOCK):
        rows = [r0 + i * GLA_BLOCK + r for i in range(nblk)]
        br = _rows_bcast(b_ref, rows, ks, 1)
        kr = _rows_bcast(k_ref, rows, ks, 1)
        parts.append((tile(q_ref) * kr * jnp.exp2(jnp.minimum(tile(b_ref) - br, 0.0))).astype(BF16))
        yield
    same_block = _dot(jnp.concatenate(parts, axis=1), sel_ref[...])
    return jnp.where(code_ref[...] == CODE_BLOCK, same_block, att)


def _inproj_gla_body(h_ref, nw_ref, w_ref, wgk_ref, bgk_ref, code_ref, sgn_ref, sel_ref, gn_ref,
                     og_ref, xp_ref, u_ref, vv_ref, gt_ref, sfin_ref,
                     s_scr, q_scr, k_scr, v_scr, g_scr, b_scr, *, tiles_per_seq):
    i = pl.program_id(0)
    tm = h_ref.shape[0]
    c = GLA_CHUNK

    @pl.when(i % tiles_per_seq == 0)
    def _():
        s_scr[...] = jnp.zeros_like(s_scr)

    xn = _rms(h_ref[...], nw_ref[...]).astype(BF16)

    def proj(lo, hi):
        return _dot(xn, w_ref[:, lo:hi])

    def slice_task(lo, dst_ref, dst_lo, act):
        def run():
            dst_ref[:, dst_lo:dst_lo + PROJ_SLICE] = act(proj(lo, lo + PROJ_SLICE)).astype(dst_ref.dtype)
        return run

    def slice_tasks(cols, dst_ref, act=lambda x: x):
        return [slice_task(lo, dst_ref, lo - cols[0], act) for lo in range(cols[0], cols[1], PROJ_SLICE)]

    q_scale = GLA_HEAD_K ** -0.5
    head_tasks = (slice_tasks(C_GOUT, g_scr, jax.nn.silu) + slice_tasks(C_V, v_scr)
                  + slice_tasks((C_QK[0], C_QK[0] + GLA_KEY_DIM), q_scr, lambda x: x * q_scale)
                  + slice_tasks((C_QK[0] + GLA_KEY_DIM, C_QK[1]), k_scr))
    tails = (slice_tasks(C_U, u_ref, jax.nn.gelu) + slice_tasks(C_VV, vv_ref, jax.nn.gelu)
             + slice_tasks(C_GATES, gt_ref, jax.nn.sigmoid) + slice_tasks(C_XP, xp_ref))

    glr = proj(*C_GLR).astype(BF16)
    g2 = jax.nn.log_sigmoid(_dot(glr, wgk_ref[...]) + bgk_ref[...]) * LOG2_DECAY_SCALE
    per_head = len(head_tasks) // GLA_HEADS
    for h in range(GLA_HEADS):
        ks = h * GLA_HEAD_K
        b_scr[:, ks:ks + GLA_HEAD_K] = _seg_cumsum(g2[:, ks:ks + GLA_HEAD_K], c)
        for t in head_tasks[h * per_head:(h + 1) * per_head]:
            t()

    def pump(n=1):
        for _ in range(min(n, len(tails))):
            tails.pop(0)()

    def head_steps(r0, h):
        ks, kv = h * GLA_HEAD_K, h * GLA_HEAD_V
        tile = lambda ref: ref[r0:r0 + c, ks:ks + GLA_HEAD_K]
        v_bf = lambda: v_scr[r0:r0 + c, kv:kv + GLA_HEAD_V]
        s = s_scr[h]
        o = _dot((tile(q_scr) * jnp.exp2(tile(b_scr))).astype(BF16), s.astype(BF16))
        b_t = tile(b_scr).T
        b_last = b_t[:, c - 1:c]
        kd_t = (tile(k_scr).T * jnp.exp2(b_last - b_t)).astype(BF16)
        s_scr[h] = jnp.exp2(b_last) * s + _dot(kd_t, v_bf())
        yield
        att = yield from _gla_scores_steps(q_scr, k_scr, b_scr, sel_ref, code_ref, r0, ks, c, sgn_ref)
        o = o + _dot(att.astype(BF16), v_bf())
        yield
        gate = g_scr[r0:r0 + c, kv:kv + GLA_HEAD_V].astype(F32)
        og_ref[r0:r0 + c, kv:kv + GLA_HEAD_V] = (_rms(o, gn_ref[...]) * gate).astype(BF16)

    for ch, h0 in [(ch, h0) for ch in range(tm // c) for h0 in range(0, GLA_HEADS, HEADS_IN_FLIGHT)]:
        active = [head_steps(ch * c, h) for h in range(h0, h0 + HEADS_IN_FLIGHT)]
        while active:
            for steps in list(active):
                try:
                    next(steps)
                except StopIteration:
                    active.remove(steps)
            pump()
    pump(len(tails))

    @pl.when(i % tiles_per_seq == tiles_per_seq - 1)
    def _():
        sfin_ref[0] = s_scr[...]


def _inproj_gla(h, norm_w, w_r, wgk, bgk, gn, layer, *, tm, n_seq, seq_len):
    m = h.shape[0]
    c = GLA_CHUNK
    tiles_per_seq = seq_len // tm
    row = lambda w: pl.BlockSpec((tm, w), lambda i: (i, 0))
    out_shape = (
        jax.ShapeDtypeStruct((m, GLA_VAL_DIM), BF16),
        jax.ShapeDtypeStruct((m, POOL_WIDTH), F32),
        jax.ShapeDtypeStruct((m, SG_WIDTH), BF16),
        jax.ShapeDtypeStruct((m, SG_WIDTH), BF16),
        jax.ShapeDtypeStruct((m, N_BRANCHES * D_MODEL), BF16),
        jax.ShapeDtypeStruct((n_seq, GLA_HEADS, GLA_HEAD_K, GLA_HEAD_V), F32),
    )
    code, sgn, sel = _score_codes(c, True), _level_signs(c), _block_sum_matrix(c)
    return pl.pallas_call(
        functools.partial(_inproj_gla_body, tiles_per_seq=tiles_per_seq),
        grid=(m // tm,),
        in_specs=[row(D_MODEL), _resident((1, D_MODEL), layer), _resident((D_MODEL, IN_COLS_R), layer),
                  _resident((LANES, GLA_KEY_DIM), layer), _resident((1, GLA_KEY_DIM), layer),
                  _resident(code.shape), _resident(sgn.shape), _resident(sel.shape),
                  _resident((1, GLA_HEAD_V), layer)],
        out_specs=[row(GLA_VAL_DIM), row(POOL_WIDTH), row(SG_WIDTH), row(SG_WIDTH), row(N_BRANCHES * D_MODEL),
                   pl.BlockSpec((1, GLA_HEADS, GLA_HEAD_K, GLA_HEAD_V), lambda i: (i // tiles_per_seq, 0, 0, 0))],
        out_shape=out_shape,
        scratch_shapes=[pltpu.VMEM((GLA_HEADS, GLA_HEAD_K, GLA_HEAD_V), F32),
                        pltpu.VMEM((tm, GLA_KEY_DIM), F32), pltpu.VMEM((tm, GLA_KEY_DIM), F32),
                        pltpu.VMEM((tm, GLA_VAL_DIM), BF16), pltpu.VMEM((tm, GLA_VAL_DIM), BF16),
                        pltpu.VMEM((tm, GLA_KEY_DIM), F32)],
        compiler_params=pltpu.CompilerParams(
            dimension_semantics=("arbitrary",), vmem_limit_bytes=VMEM_LIMIT_BYTES),
        name="inproj_gla",
    )(h, norm_w, w_r, wgk, bgk, code, sgn, sel, gn)


def _gla_sample_body(qk_ref, vg_ref, b_ref, s0_ref, code_ref, sel_ref, gn_ref, *rest, seq_len):
    og_ref, snew_ref, q_scr, k_scr = rest[-4:]
    nb = SAMPLE_SEQS_PER_STEP
    c = nb * seq_len
    q_scr[...] = qk_ref[:, :GLA_KEY_DIM].astype(F32) * (GLA_HEAD_K ** -0.5)
    k_scr[...] = qk_ref[:, GLA_KEY_DIM:].astype(F32)
    gn = gn_ref[...]
    lane_seq = lax.broadcasted_iota(jnp.int32, (GLA_HEAD_K, c), 1) // seq_len
    last_rows = [(n + 1) * seq_len - 1 for n in range(nb)]
    for h in range(GLA_HEADS):
        ks, kv = h * GLA_HEAD_K, h * GLA_HEAD_V
        k = k_scr[:, ks:ks + GLA_HEAD_K]
        b = b_ref[:, ks:ks + GLA_HEAD_K]
        v_bf = vg_ref[:, kv:kv + GLA_HEAD_V]
        gate = vg_ref[:, GLA_VAL_DIM + kv:GLA_VAL_DIM + kv + GLA_HEAD_V].astype(F32)
        qi = q_scr[:, ks:ks + GLA_HEAD_K] * jnp.exp2(b)
        inter = [
            _dot(qi[n * seq_len:(n + 1) * seq_len, :].astype(BF16), s0_ref[n, h].astype(BF16))
            for n in range(nb)
        ]
        att = _run(_gla_scores_steps(q_scr, k_scr, b_ref, sel_ref, code_ref, 0, ks, c))
        o = jnp.concatenate(inter, axis=0) + _dot(att.astype(BF16), v_bf)
        og_ref[:, kv:kv + GLA_HEAD_V] = (_rms(o, gn) * gate).astype(BF16)

        b_last_rows = _rows_bcast(b_ref, last_rows, ks, 1)
        kd_t = (k * jnp.exp2(b_last_rows - b)).T
        b_t = b.T
        for n in range(nb):
            col = last_rows[n]
            decay = jnp.exp2(b_t[:, col:col + 1])
            kd_n = jnp.where(lane_seq == n, kd_t, 0.0).astype(BF16)
            snew_ref[n, h] = decay * s0_ref[n, h] + _dot(kd_n, v_bf)


def _gla_sample(qk, vg, b2, state_all, gn, layer, prev_new_state, *, seq_len):
    assert seq_len == GLA_BLOCK
    nb = SAMPLE_SEQS_PER_STEP
    n_seq = state_all.shape[1]
    c = nb * seq_len
    row = lambda w: pl.BlockSpec((c, w), lambda i: (i, 0))
    st = pl.BlockSpec((None, nb, GLA_HEADS, GLA_HEAD_K, GLA_HEAD_V), lambda i: (layer, i, 0, 0, 0))
    in_specs = [row(2 * GLA_KEY_DIM), row(2 * GLA_VAL_DIM), row(GLA_KEY_DIM), st, _resident((c, c)),
                _resident((GLA_BLOCK * GLA_HEAD_K, c)), _resident((1, GLA_HEAD_V), layer)]
    args = [qk, vg, b2, state_all, _score_codes(c, False), _block_sum_matrix(c), gn]
    aliases = {}
    if prev_new_state is not None:
        in_specs.append(pl.BlockSpec(memory_space=pl.ANY))
        args.append(prev_new_state)
        aliases = {len(args) - 1: 1}
    return pl.pallas_call(
        functools.partial(_gla_sample_body, seq_len=seq_len),
        grid=(n_seq // nb,),
        in_specs=in_specs,
        out_specs=[row(GLA_VAL_DIM), st],
        out_shape=(jax.ShapeDtypeStruct((n_seq * seq_len, GLA_VAL_DIM), BF16),
                   jax.ShapeDtypeStruct(state_all.shape, F32)),
        scratch_shapes=[pltpu.VMEM((c, GLA_KEY_DIM), F32), pltpu.VMEM((c, GLA_KEY_DIM), F32)],
        input_output_aliases=aliases,
        compiler_params=pltpu.CompilerParams(
            dimension_semantics=("arbitrary",), vmem_limit_bytes=VMEM_LIMIT_BYTES),
        name="gla_sample",
    )(*args)


def _window_sums(full):
    outs = []
    s = full
    w = 1
    for gi, win in enumerate(POOL_WINDOWS):
        while w < win:
            s = s + pltpu.roll(s, w, 0)
            w *= 2
        outs.append(s[:, gi * POOL_GROUP_DIM:(gi + 1) * POOL_GROUP_DIM])
    return jnp.concatenate(outs, axis=1)


def _pool_counts(pos, n_prev):
    grp = lax.broadcasted_iota(jnp.int32, pos.shape, pos.ndim - 1) // POOL_GROUP_DIM
    win = jnp.zeros(pos.shape, jnp.int32)
    for gi, w in enumerate(POOL_WINDOWS):
        win = jnp.where(grp == gi, w, win)
    return jnp.minimum(win, pos + 1 + n_prev).astype(F32)


POOL_PREV_ROWS = 2 * SUBLANES


def _pool_prompt_tile(x_ref, prev_ref, tile_in_seq):
    tp = x_ref.shape[0]
    x = x_ref[...]
    prev = jnp.where(tile_in_seq == 0, 0.0, prev_ref[...])
    sums = _window_sums(jnp.concatenate([prev, x], axis=0))[POOL_PREV_ROWS:, :]
    pos = _row_iota(tp, POOL_WIDTH) + tile_in_seq * tp
    return (sums / _pool_counts(pos, 0) - x).astype(BF16)


def _pool_sample_body(x_ref, hist_ref, o_ref, nh_ref, *, n_prev):
    nb, t, _ = x_ref.shape
    hr = hist_ref.shape[1]
    x = x_ref[...]
    full = jnp.concatenate([hist_ref[...], x], axis=1)
    sums = _window_sums(full.reshape(nb * (hr + t), POOL_WIDTH)).reshape(nb, hr + t, POOL_WIDTH)
    pos = lax.broadcasted_iota(jnp.int32, (nb, t, POOL_WIDTH), 1)
    o_ref[...] = (sums[:, hr:, :] / _pool_counts(pos, n_prev) - x).astype(BF16)
    nh_ref[...] = full[:, t:, :]


def _pool_sample(xp3, hist16, *, n_prev, nb):
    n, t, _ = xp3.shape
    hr = hist16.shape[1]
    blk = lambda r: pl.BlockSpec((nb, r, POOL_WIDTH), lambda i: (i, 0, 0))
    return pl.pallas_call(
        functools.partial(_pool_sample_body, n_prev=n_prev),
        grid=(n // nb,),
        in_specs=[blk(t), blk(hr)],
        out_specs=[blk(t), blk(hr)],
        out_shape=(jax.ShapeDtypeStruct((n, t, POOL_WIDTH), BF16),
                   jax.ShapeDtypeStruct((n, hr, POOL_WIDTH), F32)),
        compiler_params=pltpu.CompilerParams(dimension_semantics=("arbitrary",)),
        name="pool_sample",
    )(xp3, hist16)


def _merge_ffn_body(h_ref, og_ref, u_ref, vv_ref, gt_ref, *rest, final_norm, pool_tiles_per_seq):
    if pool_tiles_per_seq is None:
        pl_ref, rest = rest[0], rest[1:]
        pooled = lambda g: pl_ref[:, g * POOL_GROUP_DIM:(g + 1) * POOL_GROUP_DIM]
    else:
        xp_ref, prev_ref, rest = rest[0], rest[1], rest[2:]
        pooled_all = _pool_prompt_tile(xp_ref, prev_ref, pl.program_id(0) % pool_tiles_per_seq)
        pooled = lambda g: pooled_all[:, g * POOL_GROUP_DIM:(g + 1) * POOL_GROUP_DIM]
    (wmix_ref, pscale_ref, wsg_ref, bsg_ref, wa_ref, wb_ref, wc_ref, wo_ref,
     nf_ref, w1_ref, w2_ref, nfin_ref, o_ref) = rest
    tm = h_ref.shape[0]
    br = gt_ref[:, 0:D_MODEL].astype(F32) * _dot(og_ref[...], wa_ref[...])
    mixed = jnp.concatenate([_dot(pooled(g), wmix_ref[g]) for g in range(POOL_GROUPS)], axis=1)
    pool_out = (mixed * pscale_ref[...]).astype(BF16)
    br = br + gt_ref[:, D_MODEL:2 * D_MODEL].astype(F32) * _dot(pool_out, wb_ref[...])
    rows = []
    for c in range(tm // SG_CHUNK):
        r0 = c * SG_CHUNK
        vv = vv_ref[r0:r0 + SG_CHUNK, :].astype(BF16)
        sg = jnp.concatenate(
            [_dot(wsg_ref[g], vv[:, g * SG_GROUP_DIM:(g + 1) * SG_GROUP_DIM]) for g in range(SG_GROUPS)],
            axis=1) + bsg_ref[...]
        rows.append((u_ref[r0:r0 + SG_CHUNK, :].astype(F32) * sg).astype(BF16))
    sg_out = jnp.concatenate(rows, axis=0)
    br = br + gt_ref[:, 2 * D_MODEL:3 * D_MODEL].astype(F32) * _dot(sg_out, wc_ref[...])
    h = h_ref[...] + _dot(br.astype(BF16), wo_ref[...])
    a = jnp.maximum(_dot(_rms(h, nf_ref[...]).astype(BF16), w1_ref[...]), 0.0)
    h = h + _dot((a * a).astype(BF16), w2_ref[...])
    if final_norm:
        h = _rms(h, nfin_ref[...])
    o_ref[...] = h


def _merge_ffn(h, og, pool_in, u, vv, gates, layer_wts, sg_wts, nfin, layer, *, tm, final_norm, pool_seq_len=None):
    m = h.shape[0]
    row = lambda w: pl.BlockSpec((tm, w), lambda i: (i, 0))
    (wmix, pscale, wa, wb, wc, wo, nf, w1, w2) = layer_wts
    wsg, bsg = sg_wts
    lw = lambda w: _resident(w.shape[1:], layer)
    if pool_seq_len is None:
        pool_specs, pool_args, tiles_per_seq = [row(POOL_WIDTH)], [pool_in], None
    else:
        ratio = tm // POOL_PREV_ROWS
        pool_specs = [row(POOL_WIDTH),
                      pl.BlockSpec((POOL_PREV_ROWS, POOL_WIDTH), lambda i: (jnp.maximum(i * ratio - 1, 0), 0))]
        pool_args, tiles_per_seq = [pool_in, pool_in], pool_seq_len // tm
    return pl.pallas_call(
        functools.partial(_merge_ffn_body, final_norm=final_norm, pool_tiles_per_seq=tiles_per_seq),
        grid=(m // tm,),
        in_specs=[row(D_MODEL), row(GLA_VAL_DIM), row(SG_WIDTH), row(SG_WIDTH), row(N_BRANCHES * D_MODEL)]
        + pool_specs
        + [lw(wmix), lw(pscale), lw(wsg), lw(bsg), lw(wa), lw(wb), lw(wc), lw(wo), lw(nf), lw(w1), lw(w2),
           _resident(nfin.shape)],
        out_specs=row(D_MODEL),
        out_shape=jax.ShapeDtypeStruct((m, D_MODEL), F32),
        compiler_params=pltpu.CompilerParams(
            dimension_semantics=("arbitrary",), vmem_limit_bytes=VMEM_LIMIT_BYTES),
        name="merge_ffn",
    )(h, og, u, vv, gates, *pool_args, wmix, pscale, wsg, bsg, wa, wb, wc, wo, nf, w1, w2, nfin)


def _reorder_w_in_body(wt_ref, o_ref):
    glr0 = 2 * GLA_KEY_DIM + 2 * GLA_VAL_DIM
    glr1 = glr0 + GLA_GATE_RANK
    tk = wt_ref.shape[1]
    o_ref[:, :glr0] = wt_ref[:glr0, :].T.astype(BF16)
    o_ref[:, glr0:C_GLR[0]] = wt_ref[glr1:, :].T.astype(BF16)
    glr = jnp.concatenate([wt_ref[glr0:glr1, :], jnp.zeros((LANES - GLA_GATE_RANK, tk), F32)], axis=0)
    o_ref[:, C_GLR[0]:] = glr.T.astype(BF16)


def _reorder_w_in(w):
    depth, rows, cols = w.shape
    tk = 256
    return pl.pallas_call(
        _reorder_w_in_body,
        grid=(depth, rows // tk),
        in_specs=[pl.BlockSpec((None, cols, tk), lambda l, i: (l, 0, i))],
        out_specs=pl.BlockSpec((None, tk, IN_COLS_R), lambda l, i: (l, i, 0)),
        out_shape=jax.ShapeDtypeStruct((depth, rows, IN_COLS_R), BF16),
        compiler_params=pltpu.CompilerParams(dimension_semantics=("arbitrary", "arbitrary")),
        name="reorder_w_in",
    )(jnp.swapaxes(w, 1, 2))


def _sg_operands(w_spatial, b_spatial, length):
    reps = SG_CHUNK // length
    t = jnp.arange(SG_CHUNK)
    w = jnp.tile(w_spatial[:, :, :length, :length], (1, 1, reps, reps))
    keep = ((t[:, None] // length) == (t[None, :] // length)) & ((t[None, :] % length) <= (t[:, None] % length))
    w = jnp.where(keep, w, 0.0).astype(BF16)
    bias = jnp.repeat(jnp.swapaxes(jnp.tile(b_spatial[:, :, :length], (1, 1, reps)), 1, 2), SG_GROUP_DIM, axis=2)
    return w, bias.astype(F32)


def kernel(x_prompt, x_sample, state_gla, state_pool, norm_mix, w_in, w_gk2, b_gk, gla_norm, w_pool_mix,
           pool_scale, w_spatial, b_spatial, w_br_a, w_br_b, w_br_c, w_out, norm_ffn, w_ff1, w_ff2,
           norm_final):
    depth = w_in.shape[0]
    n_p, t_p, _ = x_prompt.shape
    n_s, t_s, _ = x_sample.shape
    hp = x_prompt.reshape(n_p * t_p, D_MODEL)
    hs = x_sample.reshape(n_s * t_s, D_MODEL)
    n_prev_s = POOL_HIST
    rows = lambda a: a.reshape(depth, 1, -1).astype(F32)
    bf = lambda a: a.astype(BF16)

    w_r = _reorder_w_in(w_in)
    wgk = bf(jnp.pad(w_gk2, ((0, 0), (0, LANES - GLA_GATE_RANK), (0, 0))))
    bgk, gn, nmix = rows(b_gk), rows(gla_norm), rows(norm_mix)
    layer_wts = (bf(w_pool_mix), rows(pool_scale), bf(w_br_a), bf(w_br_b), bf(w_br_c), bf(w_out),
                 rows(norm_ffn), bf(w_ff1), bf(w_ff2))
    sg_p = _sg_operands(w_spatial, b_spatial, min(t_p, SG_CHUNK))
    sg_s = _sg_operands(w_spatial, b_spatial, min(t_s, SG_CHUNK))
    nfin = norm_final.reshape(1, -1).astype(F32)
    hist16 = jnp.pad(state_pool, ((0, 0), (0, 0), (1, 0), (0, 0)))

    gla_p, pool_p, pool_s, sgv_s = [], [], [], []
    gla_s = None
    for l in range(depth):
        final = l == depth - 1
        og, xp, u, vv, gates, s_fin = _inproj_gla(hp, nmix, w_r, wgk, bgk, gn, l, tm=TM_PROJ, n_seq=n_p,
                                                  seq_len=t_p)
        hp = _merge_ffn(hp, og, xp, u, vv, gates, layer_wts, sg_p, nfin, l, tm=TM_MERGE, final_norm=final,
                        pool_seq_len=t_p)
        gla_p.append(s_fin)
        pool_p.append(xp.reshape(n_p, t_p, POOL_WIDTH)[:, t_p - POOL_HIST:, :])

        qk, vg, xp, u, vv, gates, b2 = _inproj(hs, nmix, w_r, wgk, bgk, l, tm=TM_PROJ, seg=t_s, vv_dtype=F32)
        og, gla_s = _gla_sample(qk, vg, b2, state_gla, gn, l, gla_s, seq_len=t_s)
        pooled, new_hist = _pool_sample(xp.reshape(n_s, t_s, POOL_WIDTH), hist16[l], n_prev=n_prev_s, nb=16)
        hs = _merge_ffn(hs, og, pooled.reshape(n_s * t_s, POOL_WIDTH), u, vv, gates, layer_wts, sg_s, nfin, l,
                        tm=TM_MERGE, final_norm=final)
        pool_s.append(new_hist[:, 1:, :])
        sgv_s.append(vv.reshape(n_s, t_s, SG_WIDTH))

    return (hp.reshape(n_p, t_p, D_MODEL), hs.reshape(n_s, t_s, D_MODEL), jnp.stack(gla_p), gla_s,
            jnp.stack(pool_p), jnp.stack(pool_s), jnp.stack(sgv_s))
```

```python
import functools
import math

import jax
import jax.numpy as jnp
import numpy as np
from jax import lax
from jax.experimental import pallas as pl
from jax.experimental.pallas import tpu as pltpu

F32 = jnp.float32
BF16 = jnp.bfloat16

D_MODEL = 1024
GLA_HEADS = 4
GLA_KEY_DIM = 512
GLA_VAL_DIM = 1024
GLA_HEAD_K = 128
GLA_HEAD_V = 256
GLA_GATE_RANK = 16
GLA_GATE_NORMALIZER = 16.0
POOL_WIDTH = 512
POOL_WINDOWS = (2, 4, 8, 16)
POOL_GROUPS = 4
POOL_GROUP_DIM = 128
POOL_HIST = 15
SG_WIDTH = 512
SG_GROUPS = 4
SG_GROUP_DIM = 128
SG_CHUNK = 128
N_BRANCHES = 3
D_FF = 4096
EPS = 1e-6

SUBLANES = 8
LANES = 128
VMEM_LIMIT_BYTES = 56 * 1024 * 1024

C_QK = (0, 1024)
C_V = (1024, 2048)
C_GOUT = (2048, 3072)
C_XP = (3072, 3584)
C_U = (3584, 4096)
C_VV = (4096, 4608)
C_GATES = (4608, 7680)
C_GLR = (7680, 7808)
IN_COLS_R = 7808

TM_PROJ = 512
TM_MERGE = 512
PROJ_SLICE = 256
HEADS_IN_FLIGHT = 1
GLA_CHUNK = 128
GLA_BLOCK = SUBLANES
SAMPLE_SEQS_PER_STEP = 16
LOG2_DECAY_SCALE = math.log2(math.e) / GLA_GATE_NORMALIZER


def _resident(shape, layer=None):
    nd = len(shape)
    if layer is None:
        return pl.BlockSpec(shape, lambda *_: (0,) * nd, pipeline_mode=pl.Buffered(1))
    return pl.BlockSpec((None,) + tuple(shape), lambda *_: (layer,) + (0,) * nd,
                        pipeline_mode=pl.Buffered(1))


def _rms(x, w):
    return x * lax.rsqrt(jnp.mean(x * x, axis=-1, keepdims=True) + EPS) * w


def _dot(a, b):
    return jnp.dot(a, b, preferred_element_type=F32)


def _dot_nt(a, b):
    return lax.dot_general(a, b, (((1,), (1,)), ((), ())), preferred_element_type=F32)


def _row_iota(c, w):
    return lax.broadcasted_iota(jnp.int32, (c, w), 0)


def _bcast_row(x, blk, r):
    c, w = x.shape
    x3 = x.reshape(c // blk, blk, w)
    return jnp.broadcast_to(x3[:, r:r + 1, :], x3.shape).reshape(c, w)


def _seg_cumsum(g, seg):
    c, w = g.shape
    tpos = _row_iota(c, w) % SUBLANES
    for sh in (1, 2, 4):
        g = g + jnp.where(tpos >= sh, pltpu.roll(g, sh, 0), 0.0)
    if seg > SUBLANES:
        nb = seg // SUBLANES
        y = _bcast_row(g, SUBLANES, SUBLANES - 1).reshape(c // seg, nb, SUBLANES, w)
        sh = 1
        while sh < nb:
            y = y + jnp.concatenate([jnp.zeros((c // seg, sh, SUBLANES, w), F32), y[:, :nb - sh]], axis=1)
            sh *= 2
        excl = jnp.concatenate([jnp.zeros((c // seg, 1, SUBLANES, w), F32), y[:, :nb - 1]], axis=1)
        g = g + excl.reshape(c, w)
    return g


def _inproj_body(h_ref, nw_ref, w_ref, wgk_ref, bgk_ref,
                 qk_ref, vg_ref, xp_ref, u_ref, vv_ref, gt_ref, b2_ref, *, seg):
    xn = _rms(h_ref[...], nw_ref[...]).astype(BF16)

    def proj(lo, hi):
        return _dot(xn, w_ref[:, lo:hi])

    g2 = jax.nn.log_sigmoid(_dot(proj(*C_GLR).astype(BF16), wgk_ref[...]) + bgk_ref[...]) * LOG2_DECAY_SCALE
    b2_ref[...] = _seg_cumsum(g2, seg)
    vg_ref[:, GLA_VAL_DIM:] = jax.nn.silu(proj(*C_GOUT)).astype(BF16)
    u_ref[...] = jax.nn.gelu(proj(*C_U)).astype(u_ref.dtype)
    vv_ref[...] = jax.nn.gelu(proj(*C_VV)).astype(vv_ref.dtype)
    for j in range(N_BRANCHES):
        lo = C_GATES[0] + j * D_MODEL
        gt_ref[:, j * D_MODEL:(j + 1) * D_MODEL] = jax.nn.sigmoid(proj(lo, lo + D_MODEL)).astype(BF16)
    xp_ref[...] = proj(*C_XP)
    vg_ref[:, :GLA_VAL_DIM] = proj(*C_V).astype(BF16)
    qk_ref[...] = proj(*C_QK).astype(BF16)


def _inproj(h, norm_w, w_r, wgk, bgk, layer, *, tm, seg, vv_dtype):
    m = h.shape[0]
    row = lambda w: pl.BlockSpec((tm, w), lambda i: (i, 0))
    out_shape = (
        jax.ShapeDtypeStruct((m, 2 * GLA_KEY_DIM), BF16),
        jax.ShapeDtypeStruct((m, 2 * GLA_VAL_DIM), BF16),
        jax.ShapeDtypeStruct((m, POOL_WIDTH), F32),
        jax.ShapeDtypeStruct((m, SG_WIDTH), BF16),
        jax.ShapeDtypeStruct((m, SG_WIDTH), vv_dtype),
        jax.ShapeDtypeStruct((m, N_BRANCHES * D_MODEL), BF16),
        jax.ShapeDtypeStruct((m, GLA_KEY_DIM), F32),
    )
    return pl.pallas_call(
        functools.partial(_inproj_body, seg=seg),
        grid=(m // tm,),
        in_specs=[row(D_MODEL), _resident((1, D_MODEL), layer), _resident((D_MODEL, IN_COLS_R), layer),
                  _resident((LANES, GLA_KEY_DIM), layer), _resident((1, GLA_KEY_DIM), layer)],
        out_specs=[row(2 * GLA_KEY_DIM), row(2 * GLA_VAL_DIM), row(POOL_WIDTH), row(SG_WIDTH), row(SG_WIDTH),
                   row(N_BRANCHES * D_MODEL), row(GLA_KEY_DIM)],
        out_shape=out_shape,
        compiler_params=pltpu.CompilerParams(
            dimension_semantics=("arbitrary",), vmem_limit_bytes=VMEM_LIMIT_BYTES),
        name="inproj",
    )(h, norm_w, w_r, wgk, bgk)


CODE_BLOCK = 1


def _score_codes(c, levels):
    t = np.arange(c)[:, None]
    s = np.arange(c)[None, :]
    code = np.zeros((c, c), np.int32)
    code = np.where((t // GLA_BLOCK == s // GLA_BLOCK) & (s <= t), CODE_BLOCK, code)
    if levels:
        lvl, j = 2 * GLA_BLOCK, 0
        while lvl <= c:
            hit = (t // lvl == s // lvl) & (t // (lvl // 2) != s // (lvl // 2)) & (s < t)
            code = np.where(hit, CODE_BLOCK + 1 + j, code)
            lvl, j = lvl * 2, j + 1
    return jnp.asarray(code, jnp.int32)


def _block_sum_matrix(c):
    r = np.arange(GLA_BLOCK * GLA_HEAD_K)[:, None] // GLA_HEAD_K
    s = np.arange(c)[None, :]
    return jnp.asarray(r == s % GLA_BLOCK, BF16)


def _level_signs(c):
    t = np.arange(c)[:, None]
    out = []
    lvl = 2 * GLA_BLOCK
    while lvl <= c:
        out.append(np.broadcast_to(np.where(t % lvl >= lvl // 2, 1.0, -1.0), (c, GLA_HEAD_K)))
        lvl *= 2
    return jnp.asarray(np.stack(out), F32)


def _rows_bcast(ref, rows, lo, reps):
    parts = []
    for r in rows:
        tile = jnp.broadcast_to(ref[pl.ds(r, 1), lo:lo + GLA_HEAD_K], (SUBLANES, GLA_HEAD_K))
        parts.extend([tile] * reps)
    return jnp.concatenate(parts, axis=0)


def _run(steps):
    try:
        while True:
            next(steps)
    except StopIteration as stop:
        return stop.value


def _gla_scores_steps(q_ref, k_ref, b_ref, sel_ref, code_ref, r0, ks, c, sgn_ref=None):
    tile = lambda ref: ref[r0:r0 + c, ks:ks + GLA_HEAD_K]
    att = jnp.zeros((c, c), F32)
    if sgn_ref is not None:
        lvl, j = 2 * GLA_BLOCK, 0
        while lvl <= c:
            half = lvl // 2
            bmid = _rows_bcast(b_ref, [r0 + i * lvl + half - 1 for i in range(c // lvl)], ks, lvl // SUBLANES)
            e = jnp.exp2((tile(b_ref) - bmid) * sgn_ref[j])
            att_l = _dot_nt((tile(q_ref) * e).astype(BF16), (tile(k_ref) * e).astype(BF16))
            att = jnp.where(code_ref[...] == CODE_BLOCK + 1 + j, att_l, att)
            lvl, j = lvl * 2, j + 1
            yield
    nblk = c // GLA_BLOCK
    parts = []
    for r in range(GLA_BLOCK):
        rows = [r0 + i * GLA_BLOCK + r for i in range(nblk)]
        br = _rows_bcast(b_ref, rows, ks, 1)
        kr = _rows_bcast(k_ref, rows, ks, 1)
        parts.append((tile(q_ref) * kr * jnp.exp2(jnp.minimum(tile(b_ref) - br, 0.0))).astype(BF16))
        yield
    same_block = _dot(jnp.concatenate(parts, axis=1), sel_ref[...])
    return jnp.where(code_ref[...] == CODE_BLOCK, same_block, att)


def _inproj_gla_body(h_ref, nw_ref, w_ref, wgk_ref, bgk_ref, code_ref, sgn_ref, sel_ref, gn_ref,
                     og_ref, xp_ref, u_ref, vv_ref, gt_ref, sfin_ref,
                     s_scr, q_scr, k_scr, v_scr, g_scr, b_scr, *, tiles_per_seq):
    i = pl.program_id(0)
    tm = h_ref.shape[0]
    c = GLA_CHUNK

    @pl.when(i % tiles_per_seq == 0)
    def _():
        s_scr[...] = jnp.zeros_like(s_scr)

    xn = _rms(h_ref[...], nw_ref[...]).astype(BF16)

    def proj(lo, hi):
        return _dot(xn, w_ref[:, lo:hi])

    def slice_task(lo, dst_ref, dst_lo, act):
        def run():
            dst_ref[:, dst_lo:dst_lo + PROJ_SLICE] = act(proj(lo, lo + PROJ_SLICE)).astype(dst_ref.dtype)
        return run

    def slice_tasks(cols, dst_ref, act=lambda x: x):
        return [slice_task(lo, dst_ref, lo - cols[0], act) for lo in range(cols[0], cols[1], PROJ_SLICE)]

    q_scale = GLA_HEAD_K ** -0.5
    head_tasks = (slice_tasks(C_GOUT, g_scr, jax.nn.silu) + slice_tasks(C_V, v_scr)
                  + slice_tasks((C_QK[0], C_QK[0] + GLA_KEY_DIM), q_scr, lambda x: x * q_scale)
                  + slice_tasks((C_QK[0] + GLA_KEY_DIM, C_QK[1]), k_scr))
    tails = (slice_tasks(C_U, u_ref, jax.nn.gelu) + slice_tasks(C_VV, vv_ref, jax.nn.gelu)
             + slice_tasks(C_GATES, gt_ref, jax.nn.sigmoid) + slice_tasks(C_XP, xp_ref))

    glr = proj(*C_GLR).astype(BF16)
    g2 = jax.nn.log_sigmoid(_dot(glr, wgk_ref[...]) + bgk_ref[...]) * LOG2_DECAY_SCALE
    per_head = len(head_tasks) // GLA_HEADS
    for h in range(GLA_HEADS):
        ks = h * GLA_HEAD_K
        b_scr[:, ks:ks + GLA_HEAD_K] = _seg_cumsum(g2[:, ks:ks + GLA_HEAD_K], c)
        for t in head_tasks[h * per_head:(h + 1) * per_head]:
            t()

    def pump(n=1):
        for _ in range(min(n, len(tails))):
            tails.pop(0)()

    def head_steps(r0, h):
        ks, kv = h * GLA_HEAD_K, h * GLA_HEAD_V
        tile = lambda ref: ref[r0:r0 + c, ks:ks + GLA_HEAD_K]
        v_bf = lambda: v_scr[r0:r0 + c, kv:kv + GLA_HEAD_V]
        s = s_scr[h]
        o = _dot((tile(q_scr) * jnp.exp2(tile(b_scr))).astype(BF16), s.astype(BF16))
        b_t = tile(b_scr).T
        b_last = b_t[:, c - 1:c]
        kd_t = (tile(k_scr).T * jnp.exp2(b_last - b_t)).astype(BF16)
        s_scr[h] = jnp.exp2(b_last) * s + _dot(kd_t, v_bf())
        yield
        att = yield from _gla_scores_steps(q_scr, k_scr, b_scr, sel_ref, code_ref, r0, ks, c, sgn_ref)
        o = o + _dot(att.astype(BF16), v_bf())
        yield
        gate = g_scr[r0:r0 + c, kv:kv + GLA_HEAD_V].astype(F32)
        og_ref[r0:r0 + c, kv:kv + GLA_HEAD_V] = (_rms(o, gn_ref[...]) * gate).astype(BF16)

    for ch, h0 in [(ch, h0) for ch in range(tm // c) for h0 in range(0, GLA_HEADS, HEADS_IN_FLIGHT)]:
        active = [head_steps(ch * c, h) for h in range(h0, h0 + HEADS_IN_FLIGHT)]
        while active:
            for steps in list(active):
                try:
                    next(steps)
                except StopIteration:
                    active.remove(steps)
            pump()
    pump(len(tails))

    @pl.when(i % tiles_per_seq == tiles_per_seq - 1)
    def _():
        sfin_ref[0] = s_scr[...]


def _inproj_gla(h, norm_w, w_r, wgk, bgk, gn, layer, *, tm, n_seq, seq_len):
    m = h.shape[0]
    c = GLA_CHUNK
    tiles_per_seq = seq_len // tm
    row = lambda w: pl.BlockSpec((tm, w), lambda i: (i, 0))
    out_shape = (
        jax.ShapeDtypeStruct((m, GLA_VAL_DIM), BF16),
        jax.ShapeDtypeStruct((m, POOL_WIDTH), F32),
        jax.ShapeDtypeStruct((m, SG_WIDTH), BF16),
        jax.ShapeDtypeStruct((m, SG_WIDTH), BF16),
        jax.ShapeDtypeStruct((m, N_BRANCHES * D_MODEL), BF16),
        jax.ShapeDtypeStruct((n_seq, GLA_HEADS, GLA_HEAD_K, GLA_HEAD_V), F32),
    )
    code, sgn, sel = _score_codes(c, True), _level_signs(c), _block_sum_matrix(c)
    return pl.pallas_call(
        functools.partial(_inproj_gla_body, tiles_per_seq=tiles_per_seq),
        grid=(m // tm,),
        in_specs=[row(D_MODEL), _resident((1, D_MODEL), layer), _resident((D_MODEL, IN_COLS_R), layer),
                  _resident((LANES, GLA_KEY_DIM), layer), _resident((1, GLA_KEY_DIM), layer),
                  _resident(code.shape), _resident(sgn.shape), _resident(sel.shape),
                  _resident((1, GLA_HEAD_V), layer)],
        out_specs=[row(GLA_VAL_DIM), row(POOL_WIDTH), row(SG_WIDTH), row(SG_WIDTH), row(N_BRANCHES * D_MODEL),
                   pl.BlockSpec((1, GLA_HEADS, GLA_HEAD_K, GLA_HEAD_V), lambda i: (i // tiles_per_seq, 0, 0, 0))],
        out_shape=out_shape,
        scratch_shapes=[pltpu.VMEM((GLA_HEADS, GLA_HEAD_K, GLA_HEAD_V), F32),
                        pltpu.VMEM((tm, GLA_KEY_DIM), F32), pltpu.VMEM((tm, GLA_KEY_DIM), F32),
                        pltpu.VMEM((tm, GLA_VAL_DIM), BF16), pltpu.VMEM((tm, GLA_VAL_DIM), BF16),
                        pltpu.VMEM((tm, GLA_KEY_DIM), F32)],
        compiler_params=pltpu.CompilerParams(
            dimension_semantics=("arbitrary",), vmem_limit_bytes=VMEM_LIMIT_BYTES),
        name="inproj_gla",
    )(h, norm_w, w_r, wgk, bgk, code, sgn, sel, gn)


def _gla_sample_body(qk_ref, vg_ref, b_ref, s0_ref, code_ref, sel_ref, gn_ref, *rest, seq_len, n_blk):
    og_ref, snew_ref, q_scr, k_scr = rest[-4:]
    nb = SAMPLE_SEQS_PER_STEP
    c = nb * seq_len

    @pl.when(pl.program_id(0) >= n_blk)
    def _():
        snew_ref[...] = jnp.zeros_like(snew_ref)

    @pl.when(pl.program_id(0) < n_blk)
    def _():
        q_scr[...] = qk_ref[:, :GLA_KEY_DIM].astype(F32) * (GLA_HEAD_K ** -0.5)
        k_scr[...] = qk_ref[:, GLA_KEY_DIM:].astype(F32)
        gn = gn_ref[...]
        lane_seq = lax.broadcasted_iota(jnp.int32, (GLA_HEAD_K, c), 1) // seq_len
        last_rows = [(n + 1) * seq_len - 1 for n in range(nb)]
        for h in range(GLA_HEADS):
            ks, kv = h * GLA_HEAD_K, h * GLA_HEAD_V
            k = k_scr[:, ks:ks + GLA_HEAD_K]
            b = b_ref[:, ks:ks + GLA_HEAD_K]
            v_bf = vg_ref[:, kv:kv + GLA_HEAD_V]
            gate = vg_ref[:, GLA_VAL_DIM + kv:GLA_VAL_DIM + kv + GLA_HEAD_V].astype(F32)
            qi = q_scr[:, ks:ks + GLA_HEAD_K] * jnp.exp2(b)
            inter = [
                _dot(qi[n * seq_len:(n + 1) * seq_len, :].astype(BF16), s0_ref[n, h].astype(BF16))
                for n in range(nb)
            ]
            att = _run(_gla_scores_steps(q_scr, k_scr, b_ref, sel_ref, code_ref, 0, ks, c))
            o = jnp.concatenate(inter, axis=0) + _dot(att.astype(BF16), v_bf)
            og_ref[:, kv:kv + GLA_HEAD_V] = (_rms(o, gn) * gate).astype(BF16)

            b_last_rows = _rows_bcast(b_ref, last_rows, ks, 1)
            kd_t = (k * jnp.exp2(b_last_rows - b)).T
            b_t = b.T
            for n in range(nb):
                col = last_rows[n]
                decay = jnp.exp2(b_t[:, col:col + 1])
                kd_n = jnp.where(lane_seq == n, kd_t, 0.0).astype(BF16)
                snew_ref[n, h] = decay * s0_ref[n, h] + _dot(kd_n, v_bf)


def _gla_sample(qk, vg, b2, state_all, gn, layer, prev_new_state, *, seq_len):
    assert seq_len == GLA_BLOCK
    nb = SAMPLE_SEQS_PER_STEP
    depth, n_seq = state_all.shape[:2]
    n_blk = n_seq // nb
    c = nb * seq_len
    first = prev_new_state is None
    assert first == (layer == 0)
    blk = (lambda i: jnp.minimum(i, n_blk - 1)) if first else (lambda i: i)
    row = lambda w: pl.BlockSpec((c, w), lambda i: (blk(i), 0))
    st_shape = (None, nb, GLA_HEADS, GLA_HEAD_K, GLA_HEAD_V)
    st_in = pl.BlockSpec(st_shape, lambda i: (layer, blk(i), 0, 0, 0))
    st_out = pl.BlockSpec(st_shape, lambda i: (layer + i // n_blk, i % n_blk, 0, 0, 0))
    in_specs = [row(2 * GLA_KEY_DIM), row(2 * GLA_VAL_DIM), row(GLA_KEY_DIM), st_in, _resident((c, c)),
                _resident((GLA_BLOCK * GLA_HEAD_K, c)), _resident((1, GLA_HEAD_V), layer)]
    args = [qk, vg, b2, state_all, _score_codes(c, False), _block_sum_matrix(c), gn]
    aliases = {}
    if not first:
        in_specs.append(pl.BlockSpec(memory_space=pl.ANY))
        args.append(prev_new_state)
        aliases = {len(args) - 1: 1}
    return pl.pallas_call(
        functools.partial(_gla_sample_body, seq_len=seq_len, n_blk=n_blk),
        grid=(n_blk * depth if first else n_blk,),
        in_specs=in_specs,
        out_specs=[row(GLA_VAL_DIM), st_out],
        out_shape=(jax.ShapeDtypeStruct((n_seq * seq_len, GLA_VAL_DIM), BF16),
                   jax.ShapeDtypeStruct(state_all.shape, F32)),
        scratch_shapes=[pltpu.VMEM((c, GLA_KEY_DIM), F32), pltpu.VMEM((c, GLA_KEY_DIM), F32)],
        input_output_aliases=aliases,
        compiler_params=pltpu.CompilerParams(
            dimension_semantics=("arbitrary",), vmem_limit_bytes=VMEM_LIMIT_BYTES),
        name="gla_sample",
    )(*args)


def _window_sums(full):
    outs = []
    s = full
    w = 1
    for gi, win in enumerate(POOL_WINDOWS):
        while w < win:
            s = s + pltpu.roll(s, w, 0)
            w *= 2
        outs.append(s[:, gi * POOL_GROUP_DIM:(gi + 1) * POOL_GROUP_DIM])
    return jnp.concatenate(outs, axis=1)


def _pool_counts(pos, n_prev):
    grp = lax.broadcasted_iota(jnp.int32, pos.shape, pos.ndim - 1) // POOL_GROUP_DIM
    win = jnp.zeros(pos.shape, jnp.int32)
    for gi, w in enumerate(POOL_WINDOWS):
        win = jnp.where(grp == gi, w, win)
    return jnp.minimum(win, pos + 1 + n_prev).astype(F32)


POOL_PREV_ROWS = 2 * SUBLANES


def _pool_prompt_tile(x_ref, prev_ref, tile_in_seq):
    tp = x_ref.shape[0]
    x = x_ref[...]
    prev = jnp.where(tile_in_seq == 0, 0.0, prev_ref[...])
    sums = _window_sums(jnp.concatenate([prev, x], axis=0))[POOL_PREV_ROWS:, :]
    pos = _row_iota(tp, POOL_WIDTH) + tile_in_seq * tp
    return (sums / _pool_counts(pos, 0) - x).astype(BF16)


def _pool_sample_body(x_ref, hist_ref, o_ref, nh_ref, *, n_prev):
    nb, t, _ = x_ref.shape
    hr = hist_ref.shape[1]
    x = x_ref[...]
    full = jnp.concatenate([hist_ref[...], x], axis=1)
    sums = _window_sums(full.reshape(nb * (hr + t), POOL_WIDTH)).reshape(nb, hr + t, POOL_WIDTH)
    pos = lax.broadcasted_iota(jnp.int32, (nb, t, POOL_WIDTH), 1)
    o_ref[...] = (sums[:, hr:, :] / _pool_counts(pos, n_prev) - x).astype(BF16)
    nh_ref[...] = full[:, t:, :]


def _pool_sample(xp3, hist16, *, n_prev, nb):
    n, t, _ = xp3.shape
    hr = hist16.shape[1]
    blk = lambda r: pl.BlockSpec((nb, r, POOL_WIDTH), lambda i: (i, 0, 0))
    return pl.pallas_call(
        functools.partial(_pool_sample_body, n_prev=n_prev),
        grid=(n // nb,),
        in_specs=[blk(t), blk(hr)],
        out_specs=[blk(t), blk(hr)],
        out_shape=(jax.ShapeDtypeStruct((n, t, POOL_WIDTH), BF16),
                   jax.ShapeDtypeStruct((n, hr, POOL_WIDTH), F32)),
        compiler_params=pltpu.CompilerParams(dimension_semantics=("arbitrary",)),
        name="pool_sample",
    )(xp3, hist16)


def _merge_ffn_body(h_ref, og_ref, u_ref, vv_ref, gt_ref, *rest, final_norm, pool_tiles_per_seq):
    if pool_tiles_per_seq is None:
        pl_ref, rest = rest[0], rest[1:]
        pooled = lambda g: pl_ref[:, g * POOL_GROUP_DIM:(g + 1) * POOL_GROUP_DIM]
    else:
        xp_ref, prev_ref, rest = rest[0], rest[1], rest[2:]
        pooled_all = _pool_prompt_tile(xp_ref, prev_ref, pl.program_id(0) % pool_tiles_per_seq)
        pooled = lambda g: pooled_all[:, g * POOL_GROUP_DIM:(g + 1) * POOL_GROUP_DIM]
    (wmix_ref, pscale_ref, wsg_ref, bsg_ref, wa_ref, wb_ref, wc_ref, wo_ref,
     nf_ref, w1_ref, w2_ref, nfin_ref, o_ref) = rest
    tm = h_ref.shape[0]
    br = gt_ref[:, 0:D_MODEL].astype(F32) * _dot(og_ref[...], wa_ref[...])
    mixed = jnp.concatenate([_dot(pooled(g), wmix_ref[g]) for g in range(POOL_GROUPS)], axis=1)
    pool_out = (mixed * pscale_ref[...]).astype(BF16)
    br = br + gt_ref[:, D_MODEL:2 * D_MODEL].astype(F32) * _dot(pool_out, wb_ref[...])
    rows = []
    for c in range(tm // SG_CHUNK):
        r0 = c * SG_CHUNK
        vv = vv_ref[r0:r0 + SG_CHUNK, :].astype(BF16)
        sg = jnp.concatenate(
            [_dot(wsg_ref[g], vv[:, g * SG_GROUP_DIM:(g + 1) * SG_GROUP_DIM]) for g in range(SG_GROUPS)],
            axis=1) + bsg_ref[...]
        rows.append((u_ref[r0:r0 + SG_CHUNK, :].astype(F32) * sg).astype(BF16))
    sg_out = jnp.concatenate(rows, axis=0)
    br = br + gt_ref[:, 2 * D_MODEL:3 * D_MODEL].astype(F32) * _dot(sg_out, wc_ref[...])
    h = h_ref[...] + _dot(br.astype(BF16), wo_ref[...])
    a = jnp.maximum(_dot(_rms(h, nf_ref[...]).astype(BF16), w1_ref[...]), 0.0)
    h = h + _dot((a * a).astype(BF16), w2_ref[...])
    if final_norm:
        h = _rms(h, nfin_ref[...])
    o_ref[...] = h


def _merge_ffn(h, og, pool_in, u, vv, gates, layer_wts, sg_wts, nfin, layer, *, tm, final_norm, pool_seq_len=None):
    m = h.shape[0]
    row = lambda w: pl.BlockSpec((tm, w), lambda i: (i, 0))
    (wmix, pscale, wa, wb, wc, wo, nf, w1, w2) = layer_wts
    wsg, bsg = sg_wts
    lw = lambda w: _resident(w.shape[1:], layer)
    if pool_seq_len is None:
        pool_specs, pool_args, tiles_per_seq = [row(POOL_WIDTH)], [pool_in], None
    else:
        ratio = tm // POOL_PREV_ROWS
        pool_specs = [row(POOL_WIDTH),
                      pl.BlockSpec((POOL_PREV_ROWS, POOL_WIDTH), lambda i: (jnp.maximum(i * ratio - 1, 0), 0))]
        pool_args, tiles_per_seq = [pool_in, pool_in], pool_seq_len // tm
    return pl.pallas_call(
        functools.partial(_merge_ffn_body, final_norm=final_norm, pool_tiles_per_seq=tiles_per_seq),
        grid=(m // tm,),
        in_specs=[row(D_MODEL), row(GLA_VAL_DIM), row(SG_WIDTH), row(SG_WIDTH), row(N_BRANCHES * D_MODEL)]
        + pool_specs
        + [lw(wmix), lw(pscale), lw(wsg), lw(bsg), lw(wa), lw(wb), lw(wc), lw(wo), lw(nf), lw(w1), lw(w2),
           _resident(nfin.shape)],
        out_specs=row(D_MODEL),
        out_shape=jax.ShapeDtypeStruct((m, D_MODEL), F32),
        compiler_params=pltpu.CompilerParams(
            dimension_semantics=("arbitrary",), vmem_limit_bytes=VMEM_LIMIT_BYTES),
        name="merge_ffn",
    )(h, og, u, vv, gates, *pool_args, wmix, pscale, wsg, bsg, wa, wb, wc, wo, nf, w1, w2, nfin)


def _reorder_w_in_body(wt_ref, o_ref):
    glr0 = 2 * GLA_KEY_DIM + 2 * GLA_VAL_DIM
    glr1 = glr0 + GLA_GATE_RANK
    tk = wt_ref.shape[1]
    o_ref[:, :glr0] = wt_ref[:glr0, :].T.astype(BF16)
    o_ref[:, glr0:C_GLR[0]] = wt_ref[glr1:, :].T.astype(BF16)
    glr = jnp.concatenate([wt_ref[glr0:glr1, :], jnp.zeros((LANES - GLA_GATE_RANK, tk), F32)], axis=0)
    o_ref[:, C_GLR[0]:] = glr.T.astype(BF16)


def _reorder_w_in(w):
    depth, rows, cols = w.shape
    tk = 256
    return pl.pallas_call(
        _reorder_w_in_body,
        grid=(depth, rows // tk),
        in_specs=[pl.BlockSpec((None, cols, tk), lambda l, i: (l, 0, i))],
        out_specs=pl.BlockSpec((None, tk, IN_COLS_R), lambda l, i: (l, i, 0)),
        out_shape=jax.ShapeDtypeStruct((depth, rows, IN_COLS_R), BF16),
        compiler_params=pltpu.CompilerParams(dimension_semantics=("arbitrary", "arbitrary")),
        name="reorder_w_in",
    )(jnp.swapaxes(w, 1, 2))


def _sg_operands(w_spatial, b_spatial, length):
    reps = SG_CHUNK // length
    t = jnp.arange(SG_CHUNK)
    w = jnp.tile(w_spatial[:, :, :length, :length], (1, 1, reps, reps))
    keep = ((t[:, None] // length) == (t[None, :] // length)) & ((t[None, :] % length) <= (t[:, None] % length))
    w = jnp.where(keep, w, 0.0).astype(BF16)
    bias = jnp.repeat(jnp.swapaxes(jnp.tile(b_spatial[:, :, :length], (1, 1, reps)), 1, 2), SG_GROUP_DIM, axis=2)
    return w, bias.astype(F32)


def kernel(x_prompt, x_sample, state_gla, state_pool, norm_mix, w_in, w_gk2, b_gk, gla_norm, w_pool_mix,
           pool_scale, w_spatial, b_spatial, w_br_a, w_br_b, w_br_c, w_out, norm_ffn, w_ff1, w_ff2,
           norm_final):
    depth = w_in.shape[0]
    n_p, t_p, _ = x_prompt.shape
    n_s, t_s, _ = x_sample.shape
    hp = x_prompt.reshape(n_p * t_p, D_MODEL)
    hs = x_sample.reshape(n_s * t_s, D_MODEL)
    n_prev_s = POOL_HIST
    rows = lambda a: a.reshape(depth, 1, -1).astype(F32)
    bf = lambda a: a.astype(BF16)

    w_r = _reorder_w_in(w_in)
    wgk = bf(jnp.pad(w_gk2, ((0, 0), (0, LANES - GLA_GATE_RANK), (0, 0))))
    bgk, gn, nmix = rows(b_gk), rows(gla_norm), rows(norm_mix)
    layer_wts = (bf(w_pool_mix), rows(pool_scale), bf(w_br_a), bf(w_br_b), bf(w_br_c), bf(w_out),
                 rows(norm_ffn), bf(w_ff1), bf(w_ff2))
    sg_p = _sg_operands(w_spatial, b_spatial, min(t_p, SG_CHUNK))
    sg_s = _sg_operands(w_spatial, b_spatial, min(t_s, SG_CHUNK))
    nfin = norm_final.reshape(1, -1).astype(F32)
    hist16 = jnp.pad(state_pool, ((0, 0), (0, 0), (1, 0), (0, 0)))

    gla_p, pool_p, pool_s, sgv_s = [], [], [], []
    gla_s = None
    for l in range(depth):
        final = l == depth - 1
        og, xp, u, vv, gates, s_fin = _inproj_gla(hp, nmix, w_r, wgk, bgk, gn, l, tm=TM_PROJ, n_seq=n_p,
                                                  seq_len=t_p)
        hp = _merge_ffn(hp, og, xp, u, vv, gates, layer_wts, sg_p, nfin, l, tm=TM_MERGE, final_norm=final,
                        pool_seq_len=t_p)
        gla_p.append(s_fin)
        pool_p.append(xp.reshape(n_p, t_p, POOL_WIDTH)[:, t_p - POOL_HIST:, :])

        qk, vg, xp, u, vv, gates, b2 = _inproj(hs, nmix, w_r, wgk, bgk, l, tm=TM_PROJ, seg=t_s, vv_dtype=F32)
        og, gla_s = _gla_sample(qk, vg, b2, state_gla, gn, l, gla_s, seq_len=t_s)
        pooled, new_hist = _pool_sample(xp.reshape(n_s, t_s, POOL_WIDTH), hist16[l], n_prev=n_prev_s, nb=16)
        hs = _merge_ffn(hs, og, pooled.reshape(n_s * t_s, POOL_WIDTH), u, vv, gates, layer_wts, sg_s, nfin, l,
                        tm=TM_MERGE, final_norm=final)
        pool_s.append(new_hist[:, 1:, :])
        sgv_s.append(vv.reshape(n_s, t_s, SG_WIDTH))

    return (hp.reshape(n_p, t_p, D_MODEL), hs.reshape(n_s, t_s, D_MODEL), jnp.stack(gla_p), gla_s,
            jnp.stack(pool_p), jnp.stack(pool_s), jnp.stack(sgv_s))
```

```python
import functools
import math

import jax
import jax.numpy as jnp
import numpy as np
from jax import lax
from jax.experimental import pallas as pl
from jax.experimental.pallas import tpu as pltpu

F32 = jnp.float32
BF16 = jnp.bfloat16

D_MODEL = 1024
GLA_HEADS = 4
GLA_KEY_DIM = 512
GLA_VAL_DIM = 1024
GLA_HEAD_K = 128
GLA_HEAD_V = 256
GLA_GATE_RANK = 16
GLA_GATE_NORMALIZER = 16.0
POOL_WIDTH = 512
POOL_WINDOWS = (2, 4, 8, 16)
POOL_GROUPS = 4
POOL_GROUP_DIM = 128
POOL_HIST = 15
SG_WIDTH = 512
SG_GROUPS = 4
SG_GROUP_DIM = 128
SG_CHUNK = 128
N_BRANCHES = 3
D_FF = 4096
EPS = 1e-6

SUBLANES = 8
LANES = 128
VMEM_LIMIT_BYTES = 56 * 1024 * 1024

C_QK = (0, 1024)
C_V = (1024, 2048)
C_GOUT = (2048, 3072)
C_XP = (3072, 3584)
C_U = (3584, 4096)
C_VV = (4096, 4608)
C_GATES = (4608, 7680)
C_GLR = (7680, 7808)
IN_COLS_R = 7808

TM_PROJ = 512
TM_MERGE = 512
PROJ_SLICE = 256
HEADS_IN_FLIGHT = 1
GLA_CHUNK = 128
GLA_BLOCK = SUBLANES
SAMPLE_SEQS_PER_STEP = 16
LOG2_DECAY_SCALE = math.log2(math.e) / GLA_GATE_NORMALIZER


def _resident(shape, layer=None):
    nd = len(shape)
    if layer is None:
        return pl.BlockSpec(shape, lambda *_: (0,) * nd, pipeline_mode=pl.Buffered(1))
    return pl.BlockSpec((None,) + tuple(shape), lambda *_: (layer,) + (0,) * nd,
                        pipeline_mode=pl.Buffered(1))


def _rms(x, w):
    return x * lax.rsqrt(jnp.mean(x * x, axis=-1, keepdims=True) + EPS) * w


def _dot(a, b):
    return jnp.dot(a, b, preferred_element_type=F32)


def _dot_nt(a, b):
    return lax.dot_general(a, b, (((1,), (1,)), ((), ())), preferred_element_type=F32)


def _row_iota(c, w):
    return lax.broadcasted_iota(jnp.int32, (c, w), 0)


def _bcast_row(x, blk, r):
    c, w = x.shape
    x3 = x.reshape(c // blk, blk, w)
    return jnp.broadcast_to(x3[:, r:r + 1, :], x3.shape).reshape(c, w)


def _seg_cumsum(g, seg):
    c, w = g.shape
    tpos = _row_iota(c, w) % SUBLANES
    for sh in (1, 2, 4):
        g = g + jnp.where(tpos >= sh, pltpu.roll(g, sh, 0), 0.0)
    if seg > SUBLANES:
        nb = seg // SUBLANES
        y = _bcast_row(g, SUBLANES, SUBLANES - 1).reshape(c // seg, nb, SUBLANES, w)
        sh = 1
        while sh < nb:
            y = y + jnp.concatenate([jnp.zeros((c // seg, sh, SUBLANES, w), F32), y[:, :nb - sh]], axis=1)
            sh *= 2
        excl = jnp.concatenate([jnp.zeros((c // seg, 1, SUBLANES, w), F32), y[:, :nb - 1]], axis=1)
        g = g + excl.reshape(c, w)
    return g


def _inproj_body(h_ref, nw_ref, w_ref, wgk_ref, bgk_ref,
                 qk_ref, vg_ref, xp_ref, u_ref, vv_ref, gt_ref, b2_ref, *, seg):
    xn = _rms(h_ref[...], nw_ref[...]).astype(BF16)

    def proj(lo, hi):
        return _dot(xn, w_ref[:, lo:hi])

    g2 = jax.nn.log_sigmoid(_dot(proj(*C_GLR).astype(BF16), wgk_ref[...]) + bgk_ref[...]) * LOG2_DECAY_SCALE
    b2_ref[...] = _seg_cumsum(g2, seg)
    vg_ref[:, GLA_VAL_DIM:] = jax.nn.silu(proj(*C_GOUT)).astype(BF16)
    u_ref[...] = jax.nn.gelu(proj(*C_U)).astype(u_ref.dtype)
    vv_ref[...] = jax.nn.gelu(proj(*C_VV)).astype(vv_ref.dtype)
    for j in range(N_BRANCHES):
        lo = C_GATES[0] + j * D_MODEL
        gt_ref[:, j * D_MODEL:(j + 1) * D_MODEL] = jax.nn.sigmoid(proj(lo, lo + D_MODEL)).astype(BF16)
    xp_ref[...] = proj(*C_XP)
    vg_ref[:, :GLA_VAL_DIM] = proj(*C_V).astype(BF16)
    qk_ref[...] = proj(*C_QK).astype(BF16)


def _inproj(h, norm_w, w_r, wgk, bgk, layer, *, tm, seg, vv_dtype):
    m = h.shape[0]
    row = lambda w: pl.BlockSpec((tm, w), lambda i: (i, 0))
    out_shape = (
        jax.ShapeDtypeStruct((m, 2 * GLA_KEY_DIM), BF16),
        jax.ShapeDtypeStruct((m, 2 * GLA_VAL_DIM), BF16),
        jax.ShapeDtypeStruct((m, POOL_WIDTH), F32),
        jax.ShapeDtypeStruct((m, SG_WIDTH), BF16),
        jax.ShapeDtypeStruct((m, SG_WIDTH), vv_dtype),
        jax.ShapeDtypeStruct((m, N_BRANCHES * D_MODEL), BF16),
        jax.ShapeDtypeStruct((m, GLA_KEY_DIM), F32),
    )
    return pl.pallas_call(
        functools.partial(_inproj_body, seg=seg),
        grid=(m // tm,),
        in_specs=[row(D_MODEL), _resident((1, D_MODEL), layer), _resident((D_MODEL, IN_COLS_R), layer),
                  _resident((LANES, GLA_KEY_DIM), layer), _resident((1, GLA_KEY_DIM), layer)],
        out_specs=[row(2 * GLA_KEY_DIM), row(2 * GLA_VAL_DIM), row(POOL_WIDTH), row(SG_WIDTH), row(SG_WIDTH),
                   row(N_BRANCHES * D_MODEL), row(GLA_KEY_DIM)],
        out_shape=out_shape,
        compiler_params=pltpu.CompilerParams(
            dimension_semantics=("arbitrary",), vmem_limit_bytes=VMEM_LIMIT_BYTES),
        name="inproj",
    )(h, norm_w, w_r, wgk, bgk)


CODE_BLOCK = 1


def _score_codes(c, levels):
    t = np.arange(c)[:, None]
    s = np.arange(c)[None, :]
    code = np.zeros((c, c), np.int32)
    code = np.where((t // GLA_BLOCK == s // GLA_BLOCK) & (s <= t), CODE_BLOCK, code)
    if levels:
        lvl, j = 2 * GLA_BLOCK, 0
        while lvl <= c:
            hit = (t // lvl == s // lvl) & (t // (lvl // 2) != s // (lvl // 2)) & (s < t)
            code = np.where(hit, CODE_BLOCK + 1 + j, code)
            lvl, j = lvl * 2, j + 1
    return jnp.asarray(code, jnp.int32)


def _block_expand_matrix(c):
    r = np.arange(c)[:, None]
    s = np.arange(c)[None, :]
    return jnp.asarray(r == s % GLA_BLOCK, BF16)


def _level_signs(c):
    t = np.arange(c)[:, None]
    out = []
    lvl = 2 * GLA_BLOCK
    while lvl <= c:
        out.append(np.broadcast_to(np.where(t % lvl >= lvl // 2, 1.0, -1.0), (c, GLA_HEAD_K)))
        lvl *= 2
    return jnp.asarray(np.stack(out), F32)


def _rows_bcast(ref, rows, lo, reps):
    parts = []
    for r in rows:
        tile = jnp.broadcast_to(ref[pl.ds(r, 1), lo:lo + GLA_HEAD_K], (SUBLANES, GLA_HEAD_K))
        parts.extend([tile] * reps)
    return jnp.concatenate(parts, axis=0)


def _run(steps):
    try:
        while True:
            next(steps)
    except StopIteration as stop:
        return stop.value


def _gla_scores_steps(q_ref, k_ref, b_ref, sel_ref, code_ref, r0, ks, c, sgn_ref=None):
    tile = lambda ref: ref[r0:r0 + c, ks:ks + GLA_HEAD_K]
    att = jnp.zeros((c, c), F32)
    if sgn_ref is not None:
        lvl, j = 2 * GLA_BLOCK, 0
        while lvl <= c:
            half = lvl // 2
            bmid = _rows_bcast(b_ref, [r0 + i * lvl + half - 1 for i in range(c // lvl)], ks, lvl // SUBLANES)
            e = jnp.exp2((tile(b_ref) - bmid) * sgn_ref[j])
            att_l = _dot_nt((tile(q_ref) * e).astype(BF16), (tile(k_ref) * e).astype(BF16))
            att = jnp.where(code_ref[...] == CODE_BLOCK + 1 + j, att_l, att)
            lvl, j = lvl * 2, j + 1
            yield
    nblk = c // GLA_BLOCK
    lane = lax.broadcasted_iota(jnp.int32, (SUBLANES, c), 1)
    in_block = jnp.zeros((c, c), F32)
    for r in range(GLA_BLOCK):
        rows = [r0 + i * GLA_BLOCK + r for i in range(nblk)]
        br = _rows_bcast(b_ref, rows, ks, 1)
        kr = _rows_bcast(k_ref, rows, ks, 1)
        a = jnp.sum(tile(q_ref) * kr * jnp.exp2(jnp.minimum(tile(b_ref) - br, 0.0)), axis=-1, keepdims=True)
        in_block = jnp.where(jnp.concatenate([lane == r] * nblk, axis=0), a, in_block)
        yield
    same_block = _dot(in_block.astype(BF16), sel_ref[...])
    return jnp.where(code_ref[...] == CODE_BLOCK, same_block, att)


def _inproj_gla_body(h_ref, nw_ref, w_ref, wgk_ref, bgk_ref, code_ref, sgn_ref, sel_ref, gn_ref,
                     og_ref, xp_ref, u_ref, vv_ref, gt_ref, sfin_ref,
                     s_scr, q_scr, k_scr, v_scr, g_scr, b_scr, *, tiles_per_seq):
    i = pl.program_id(0)
    tm = h_ref.shape[0]
    c = GLA_CHUNK

    @pl.when(i % tiles_per_seq == 0)
    def _():
        s_scr[...] = jnp.zeros_like(s_scr)

    xn = _rms(h_ref[...], nw_ref[...]).astype(BF16)

    def proj(lo, hi):
        return _dot(xn, w_ref[:, lo:hi])

    def slice_task(lo, dst_ref, dst_lo, act):
        def run():
            dst_ref[:, dst_lo:dst_lo + PROJ_SLICE] = act(proj(lo, lo + PROJ_SLICE)).astype(dst_ref.dtype)
        return run

    def slice_tasks(cols, dst_ref, act=lambda x: x):
        return [slice_task(lo, dst_ref, lo - cols[0], act) for lo in range(cols[0], cols[1], PROJ_SLICE)]

    q_scale = GLA_HEAD_K ** -0.5
    head_tasks = (slice_tasks(C_GOUT, g_scr, jax.nn.silu) + slice_tasks(C_V, v_scr)
                  + slice_tasks((C_QK[0], C_QK[0] + GLA_KEY_DIM), q_scr, lambda x: x * q_scale)
                  + slice_tasks((C_QK[0] + GLA_KEY_DIM, C_QK[1]), k_scr))
    tails = (slice_tasks(C_U, u_ref, jax.nn.gelu) + slice_tasks(C_VV, vv_ref, jax.nn.gelu)
             + slice_tasks(C_GATES, gt_ref, jax.nn.sigmoid) + slice_tasks(C_XP, xp_ref))

    glr = proj(*C_GLR).astype(BF16)
    g2 = jax.nn.log_sigmoid(_dot(glr, wgk_ref[...]) + bgk_ref[...]) * LOG2_DECAY_SCALE
    per_head = len(head_tasks) // GLA_HEADS
    for h in range(GLA_HEADS):
        ks = h * GLA_HEAD_K
        b_scr[:, ks:ks + GLA_HEAD_K] = _seg_cumsum(g2[:, ks:ks + GLA_HEAD_K], c)
        for t in head_tasks[h * per_head:(h + 1) * per_head]:
            t()

    def pump(n=1):
        for _ in range(min(n, len(tails))):
            tails.pop(0)()

    def head_steps(r0, h):
        ks, kv = h * GLA_HEAD_K, h * GLA_HEAD_V
        tile = lambda ref: ref[r0:r0 + c, ks:ks + GLA_HEAD_K]
        v_bf = lambda: v_scr[r0:r0 + c, kv:kv + GLA_HEAD_V]
        s = s_scr[h]
        o = _dot((tile(q_scr) * jnp.exp2(tile(b_scr))).astype(BF16), s.astype(BF16))
        b_t = tile(b_scr).T
        b_last = b_t[:, c - 1:c]
        kd_t = (tile(k_scr).T * jnp.exp2(b_last - b_t)).astype(BF16)
        s_scr[h] = jnp.exp2(b_last) * s + _dot(kd_t, v_bf())
        yield
        att = yield from _gla_scores_steps(q_scr, k_scr, b_scr, sel_ref, code_ref, r0, ks, c, sgn_ref)
        o = o + _dot(att.astype(BF16), v_bf())
        yield
        gate = g_scr[r0:r0 + c, kv:kv + GLA_HEAD_V].astype(F32)
        og_ref[r0:r0 + c, kv:kv + GLA_HEAD_V] = (_rms(o, gn_ref[...]) * gate).astype(BF16)

    for ch, h0 in [(ch, h0) for ch in range(tm // c) for h0 in range(0, GLA_HEADS, HEADS_IN_FLIGHT)]:
        active = [head_steps(ch * c, h) for h in range(h0, h0 + HEADS_IN_FLIGHT)]
        while active:
            for steps in list(active):
                try:
                    next(steps)
                except StopIteration:
                    active.remove(steps)
            pump()
    pump(len(tails))

    @pl.when(i % tiles_per_seq == tiles_per_seq - 1)
    def _():
        sfin_ref[0] = s_scr[...]


def _inproj_gla(h, norm_w, w_r, wgk, bgk, gn, layer, *, tm, n_seq, seq_len):
    m = h.shape[0]
    c = GLA_CHUNK
    tiles_per_seq = seq_len // tm
    row = lambda w: pl.BlockSpec((tm, w), lambda i: (i, 0))
    out_shape = (
        jax.ShapeDtypeStruct((m, GLA_VAL_DIM), BF16),
        jax.ShapeDtypeStruct((m, POOL_WIDTH), F32),
        jax.ShapeDtypeStruct((m, SG_WIDTH), BF16),
        jax.ShapeDtypeStruct((m, SG_WIDTH), BF16),
        jax.ShapeDtypeStruct((m, N_BRANCHES * D_MODEL), BF16),
        jax.ShapeDtypeStruct((n_seq, GLA_HEADS, GLA_HEAD_K, GLA_HEAD_V), F32),
    )
    code, sgn, sel = _score_codes(c, True), _level_signs(c), _block_expand_matrix(c)
    return pl.pallas_call(
        functools.partial(_inproj_gla_body, tiles_per_seq=tiles_per_seq),
        grid=(m // tm,),
        in_specs=[row(D_MODEL), _resident((1, D_MODEL), layer), _resident((D_MODEL, IN_COLS_R), layer),
                  _resident((LANES, GLA_KEY_DIM), layer), _resident((1, GLA_KEY_DIM), layer),
                  _resident(code.shape), _resident(sgn.shape), _resident(sel.shape),
                  _resident((1, GLA_HEAD_V), layer)],
        out_specs=[row(GLA_VAL_DIM), row(POOL_WIDTH), row(SG_WIDTH), row(SG_WIDTH), row(N_BRANCHES * D_MODEL),
                   pl.BlockSpec((1, GLA_HEADS, GLA_HEAD_K, GLA_HEAD_V), lambda i: (i // tiles_per_seq, 0, 0, 0))],
        out_shape=out_shape,
        scratch_shapes=[pltpu.VMEM((GLA_HEADS, GLA_HEAD_K, GLA_HEAD_V), F32),
                        pltpu.VMEM((tm, GLA_KEY_DIM), F32), pltpu.VMEM((tm, GLA_KEY_DIM), F32),
                        pltpu.VMEM((tm, GLA_VAL_DIM), BF16), pltpu.VMEM((tm, GLA_VAL_DIM), BF16),
                        pltpu.VMEM((tm, GLA_KEY_DIM), F32)],
        compiler_params=pltpu.CompilerParams(
            dimension_semantics=("arbitrary",), vmem_limit_bytes=VMEM_LIMIT_BYTES),
        name="inproj_gla",
    )(h, norm_w, w_r, wgk, bgk, code, sgn, sel, gn)


def _gla_sample_body(qk_ref, vg_ref, b_ref, s0_ref, code_ref, sel_ref, gn_ref, *rest, seq_len, n_blk):
    og_ref, snew_ref, q_scr, k_scr = rest[-4:]
    nb = SAMPLE_SEQS_PER_STEP
    c = nb * seq_len

    @pl.when(pl.program_id(0) >= n_blk)
    def _():
        snew_ref[...] = jnp.zeros_like(snew_ref)

    @pl.when(pl.program_id(0) < n_blk)
    def _():
        q_scr[...] = qk_ref[:, :GLA_KEY_DIM].astype(F32) * (GLA_HEAD_K ** -0.5)
        k_scr[...] = qk_ref[:, GLA_KEY_DIM:].astype(F32)
        gn = gn_ref[...]
        lane_seq = lax.broadcasted_iota(jnp.int32, (GLA_HEAD_K, c), 1) // seq_len
        last_rows = [(n + 1) * seq_len - 1 for n in range(nb)]
        for h in range(GLA_HEADS):
            ks, kv = h * GLA_HEAD_K, h * GLA_HEAD_V
            k = k_scr[:, ks:ks + GLA_HEAD_K]
            b = b_ref[:, ks:ks + GLA_HEAD_K]
            v_bf = vg_ref[:, kv:kv + GLA_HEAD_V]
            gate = vg_ref[:, GLA_VAL_DIM + kv:GLA_VAL_DIM + kv + GLA_HEAD_V].astype(F32)
            qi = q_scr[:, ks:ks + GLA_HEAD_K] * jnp.exp2(b)
            inter = [
                _dot(qi[n * seq_len:(n + 1) * seq_len, :].astype(BF16), s0_ref[n, h].astype(BF16))
                for n in range(nb)
            ]
            att = _run(_gla_scores_steps(q_scr, k_scr, b_ref, sel_ref, code_ref, 0, ks, c))
            o = jnp.concatenate(inter, axis=0) + _dot(att.astype(BF16), v_bf)
            og_ref[:, kv:kv + GLA_HEAD_V] = (_rms(o, gn) * gate).astype(BF16)

            b_last_rows = _rows_bcast(b_ref, last_rows, ks, 1)
            kd_t = (k * jnp.exp2(b_last_rows - b)).T
            b_t = b.T
            for n in range(nb):
                col = last_rows[n]
                decay = jnp.exp2(b_t[:, col:col + 1])
                kd_n = jnp.where(lane_seq == n, kd_t, 0.0).astype(BF16)
                snew_ref[n, h] = decay * s0_ref[n, h] + _dot(kd_n, v_bf)


def _gla_sample(qk, vg, b2, state_all, gn, layer, prev_new_state, *, seq_len):
    assert seq_len == GLA_BLOCK
    nb = SAMPLE_SEQS_PER_STEP
    depth, n_seq = state_all.shape[:2]
    n_blk = n_seq // nb
    c = nb * seq_len
    first = prev_new_state is None
    assert first == (layer == 0)
    blk = (lambda i: jnp.minimum(i, n_blk - 1)) if first else (lambda i: i)
    row = lambda w: pl.BlockSpec((c, w), lambda i: (blk(i), 0))
    st_shape = (None, nb, GLA_HEADS, GLA_HEAD_K, GLA_HEAD_V)
    st_in = pl.BlockSpec(st_shape, lambda i: (layer, blk(i), 0, 0, 0))
    st_out = pl.BlockSpec(st_shape, lambda i: (layer + i // n_blk, i % n_blk, 0, 0, 0))
    in_specs = [row(2 * GLA_KEY_DIM), row(2 * GLA_VAL_DIM), row(GLA_KEY_DIM), st_in, _resident((c, c)),
                _resident((c, c)), _resident((1, GLA_HEAD_V), layer)]
    args = [qk, vg, b2, state_all, _score_codes(c, False), _block_expand_matrix(c), gn]
    aliases = {}
    if not first:
        in_specs.append(pl.BlockSpec(memory_space=pl.ANY))
        args.append(prev_new_state)
        aliases = {len(args) - 1: 1}
    return pl.pallas_call(
        functools.partial(_gla_sample_body, seq_len=seq_len, n_blk=n_blk),
        grid=(n_blk * depth if first else n_blk,),
        in_specs=in_specs,
        out_specs=[row(GLA_VAL_DIM), st_out],
        out_shape=(jax.ShapeDtypeStruct((n_seq * seq_len, GLA_VAL_DIM), BF16),
                   jax.ShapeDtypeStruct(state_all.shape, F32)),
        scratch_shapes=[pltpu.VMEM((c, GLA_KEY_DIM), F32), pltpu.VMEM((c, GLA_KEY_DIM), F32)],
        input_output_aliases=aliases,
        compiler_params=pltpu.CompilerParams(
            dimension_semantics=("arbitrary",), vmem_limit_bytes=VMEM_LIMIT_BYTES),
        name="gla_sample",
    )(*args)


def _window_sums(full):
    outs = []
    s = full
    w = 1
    for gi, win in enumerate(POOL_WINDOWS):
        while w < win:
            s = s + pltpu.roll(s, w, 0)
            w *= 2
        outs.append(s[:, gi * POOL_GROUP_DIM:(gi + 1) * POOL_GROUP_DIM])
    return jnp.concatenate(outs, axis=1)


def _pool_counts(pos, n_prev):
    grp = lax.broadcasted_iota(jnp.int32, pos.shape, pos.ndim - 1) // POOL_GROUP_DIM
    win = jnp.zeros(pos.shape, jnp.int32)
    for gi, w in enumerate(POOL_WINDOWS):
        win = jnp.where(grp == gi, w, win)
    return jnp.minimum(win, pos + 1 + n_prev).astype(F32)


POOL_PREV_ROWS = 2 * SUBLANES


def _pool_prompt_tile(x_ref, prev_ref, tile_in_seq):
    tp = x_ref.shape[0]
    x = x_ref[...]
    prev = jnp.where(tile_in_seq == 0, 0.0, prev_ref[...])
    sums = _window_sums(jnp.concatenate([prev, x], axis=0))[POOL_PREV_ROWS:, :]
    pos = _row_iota(tp, POOL_WIDTH) + tile_in_seq * tp
    return (sums / _pool_counts(pos, 0) - x).astype(BF16)


def _pool_sample_body(x_ref, hist_ref, o_ref, nh_ref, *, n_prev):
    nb, t, _ = x_ref.shape
    hr = hist_ref.shape[1]
    x = x_ref[...]
    full = jnp.concatenate([hist_ref[...], x], axis=1)
    sums = _window_sums(full.reshape(nb * (hr + t), POOL_WIDTH)).reshape(nb, hr + t, POOL_WIDTH)
    pos = lax.broadcasted_iota(jnp.int32, (nb, t, POOL_WIDTH), 1)
    o_ref[...] = (sums[:, hr:, :] / _pool_counts(pos, n_prev) - x).astype(BF16)
    nh_ref[...] = full[:, t:, :]


def _pool_sample(xp3, hist16, *, n_prev, nb):
    n, t, _ = xp3.shape
    hr = hist16.shape[1]
    blk = lambda r: pl.BlockSpec((nb, r, POOL_WIDTH), lambda i: (i, 0, 0))
    return pl.pallas_call(
        functools.partial(_pool_sample_body, n_prev=n_prev),
        grid=(n // nb,),
        in_specs=[blk(t), blk(hr)],
        out_specs=[blk(t), blk(hr)],
        out_shape=(jax.ShapeDtypeStruct((n, t, POOL_WIDTH), BF16),
                   jax.ShapeDtypeStruct((n, hr, POOL_WIDTH), F32)),
        compiler_params=pltpu.CompilerParams(dimension_semantics=("arbitrary",)),
        name="pool_sample",
    )(xp3, hist16)


def _merge_ffn_body(h_ref, og_ref, u_ref, vv_ref, gt_ref, *rest, final_norm, pool_tiles_per_seq):
    if pool_tiles_per_seq is None:
        pl_ref, rest = rest[0], rest[1:]
        pooled = lambda g: pl_ref[:, g * POOL_GROUP_DIM:(g + 1) * POOL_GROUP_DIM]
    else:
        xp_ref, prev_ref, rest = rest[0], rest[1], rest[2:]
        pooled_all = _pool_prompt_tile(xp_ref, prev_ref, pl.program_id(0) % pool_tiles_per_seq)
        pooled = lambda g: pooled_all[:, g * POOL_GROUP_DIM:(g + 1) * POOL_GROUP_DIM]
    (wmix_ref, pscale_ref, wsg_ref, bsg_ref, wa_ref, wb_ref, wc_ref, wo_ref,
     nf_ref, w1_ref, w2_ref, nfin_ref, o_ref) = rest
    tm = h_ref.shape[0]
    br = gt_ref[:, 0:D_MODEL].astype(F32) * _dot(og_ref[...], wa_ref[...])
    mixed = jnp.concatenate([_dot(pooled(g), wmix_ref[g]) for g in range(POOL_GROUPS)], axis=1)
    pool_out = (mixed * pscale_ref[...]).astype(BF16)
    br = br + gt_ref[:, D_MODEL:2 * D_MODEL].astype(F32) * _dot(pool_out, wb_ref[...])
    rows = []
    for c in range(tm // SG_CHUNK):
        r0 = c * SG_CHUNK
        vv = vv_ref[r0:r0 + SG_CHUNK, :].astype(BF16)
        sg = jnp.concatenate(
            [_dot(wsg_ref[g], vv[:, g * SG_GROUP_DIM:(g + 1) * SG_GROUP_DIM]) for g in range(SG_GROUPS)],
            axis=1) + bsg_ref[...]
        rows.append((u_ref[r0:r0 + SG_CHUNK, :].astype(F32) * sg).astype(BF16))
    sg_out = jnp.concatenate(rows, axis=0)
    br = br + gt_ref[:, 2 * D_MODEL:3 * D_MODEL].astype(F32) * _dot(sg_out, wc_ref[...])
    h = h_ref[...] + _dot(br.astype(BF16), wo_ref[...])
    a = jnp.maximum(_dot(_rms(h, nf_ref[...]).astype(BF16), w1_ref[...]), 0.0)
    h = h + _dot((a * a).astype(BF16), w2_ref[...])
    if final_norm:
        h = _rms(h, nfin_ref[...])
    o_ref[...] = h


def _merge_ffn(h, og, pool_in, u, vv, gates, layer_wts, sg_wts, nfin, layer, *, tm, final_norm, pool_seq_len=None):
    m = h.shape[0]
    row = lambda w: pl.BlockSpec((tm, w), lambda i: (i, 0))
    (wmix, pscale, wa, wb, wc, wo, nf, w1, w2) = layer_wts
    wsg, bsg = sg_wts
    lw = lambda w: _resident(w.shape[1:], layer)
    if pool_seq_len is None:
        pool_specs, pool_args, tiles_per_seq = [row(POOL_WIDTH)], [pool_in], None
    else:
        ratio = tm // POOL_PREV_ROWS
        pool_specs = [row(POOL_WIDTH),
                      pl.BlockSpec((POOL_PREV_ROWS, POOL_WIDTH), lambda i: (jnp.maximum(i * ratio - 1, 0), 0))]
        pool_args, tiles_per_seq = [pool_in, pool_in], pool_seq_len // tm
    return pl.pallas_call(
        functools.partial(_merge_ffn_body, final_norm=final_norm, pool_tiles_per_seq=tiles_per_seq),
        grid=(m // tm,),
        in_specs=[row(D_MODEL), row(GLA_VAL_DIM), row(SG_WIDTH), row(SG_WIDTH), row(N_BRANCHES * D_MODEL)]
        + pool_specs
        + [lw(wmix), lw(pscale), lw(wsg), lw(bsg), lw(wa), lw(wb), lw(wc), lw(wo), lw(nf), lw(w1), lw(w2),
           _resident(nfin.shape)],
        out_specs=row(D_MODEL),
        out_shape=jax.ShapeDtypeStruct((m, D_MODEL), F32),
        compiler_params=pltpu.CompilerParams(
            dimension_semantics=("arbitrary",), vmem_limit_bytes=VMEM_LIMIT_BYTES),
        name="merge_ffn",
    )(h, og, u, vv, gates, *pool_args, wmix, pscale, wsg, bsg, wa, wb, wc, wo, nf, w1, w2, nfin)


def _reorder_w_in_body(wt_ref, o_ref):
    glr0 = 2 * GLA_KEY_DIM + 2 * GLA_VAL_DIM
    glr1 = glr0 + GLA_GATE_RANK
    tk = wt_ref.shape[1]
    o_ref[:, :glr0] = wt_ref[:glr0, :].T.astype(BF16)
    o_ref[:, glr0:C_GLR[0]] = wt_ref[glr1:, :].T.astype(BF16)
    glr = jnp.concatenate([wt_ref[glr0:glr1, :], jnp.zeros((LANES - GLA_GATE_RANK, tk), F32)], axis=0)
    o_ref[:, C_GLR[0]:] = glr.T.astype(BF16)


def _reorder_w_in(w):
    depth, rows, cols = w.shape
    tk = 256
    return pl.pallas_call(
        _reorder_w_in_body,
        grid=(depth, rows // tk),
        in_specs=[pl.BlockSpec((None, cols, tk), lambda l, i: (l, 0, i))],
        out_specs=pl.BlockSpec((None, tk, IN_COLS_R), lambda l, i: (l, i, 0)),
        out_shape=jax.ShapeDtypeStruct((depth, rows, IN_COLS_R), BF16),
        compiler_params=pltpu.CompilerParams(dimension_semantics=("arbitrary", "arbitrary")),
        name="reorder_w_in",
    )(jnp.swapaxes(w, 1, 2))


def _sg_operands(w_spatial, b_spatial, length):
    reps = SG_CHUNK // length
    t = jnp.arange(SG_CHUNK)
    w = jnp.tile(w_spatial[:, :, :length, :length], (1, 1, reps, reps))
    keep = ((t[:, None] // length) == (t[None, :] // length)) & ((t[None, :] % length) <= (t[:, None] % length))
    w = jnp.where(keep, w, 0.0).astype(BF16)
    bias = jnp.repeat(jnp.swapaxes(jnp.tile(b_spatial[:, :, :length], (1, 1, reps)), 1, 2), SG_GROUP_DIM, axis=2)
    return w, bias.astype(F32)


def kernel(x_prompt, x_sample, state_gla, state_pool, norm_mix, w_in, w_gk2, b_gk, gla_norm, w_pool_mix,
           pool_scale, w_spatial, b_spatial, w_br_a, w_br_b, w_br_c, w_out, norm_ffn, w_ff1, w_ff2,
           norm_final):
    depth = w_in.shape[0]
    n_p, t_p, _ = x_prompt.shape
    n_s, t_s, _ = x_sample.shape
    hp = x_prompt.reshape(n_p * t_p, D_MODEL)
    hs = x_sample.reshape(n_s * t_s, D_MODEL)
    n_prev_s = POOL_HIST
    rows = lambda a: a.reshape(depth, 1, -1).astype(F32)
    bf = lambda a: a.astype(BF16)

    w_r = _reorder_w_in(w_in)
    wgk = bf(jnp.pad(w_gk2, ((0, 0), (0, LANES - GLA_GATE_RANK), (0, 0))))
    bgk, gn, nmix = rows(b_gk), rows(gla_norm), rows(norm_mix)
    layer_wts = (bf(w_pool_mix), rows(pool_scale), bf(w_br_a), bf(w_br_b), bf(w_br_c), bf(w_out),
                 rows(norm_ffn), bf(w_ff1), bf(w_ff2))
    sg_p = _sg_operands(w_spatial, b_spatial, min(t_p, SG_CHUNK))
    sg_s = _sg_operands(w_spatial, b_spatial, min(t_s, SG_CHUNK))
    nfin = norm_final.reshape(1, -1).astype(F32)
    hist16 = jnp.pad(state_pool, ((0, 0), (0, 0), (1, 0), (0, 0)))

    gla_p, pool_p, pool_s, sgv_s = [], [], [], []
    gla_s = None
    for l in range(depth):
        final = l == depth - 1
        og, xp, u, vv, gates, s_fin = _inproj_gla(hp, nmix, w_r, wgk, bgk, gn, l, tm=TM_PROJ, n_seq=n_p,
                                                  seq_len=t_p)
        hp = _merge_ffn(hp, og, xp, u, vv, gates, layer_wts, sg_p, nfin, l, tm=TM_MERGE, final_norm=final,
                        pool_seq_len=t_p)
        gla_p.append(s_fin)
        pool_p.append(xp.reshape(n_p, t_p, POOL_WIDTH)[:, t_p - POOL_HIST:, :])

        qk, vg, xp, u, vv, gates, b2 = _inproj(hs, nmix, w_r, wgk, bgk, l, tm=TM_PROJ, seg=t_s, vv_dtype=F32)
        og, gla_s = _gla_sample(qk, vg, b2, state_gla, gn, l, gla_s, seq_len=t_s)
        pooled, new_hist = _pool_sample(xp.reshape(n_s, t_s, POOL_WIDTH), hist16[l], n_prev=n_prev_s, nb=16)
        hs = _merge_ffn(hs, og, pooled.reshape(n_s * t_s, POOL_WIDTH), u, vv, gates, layer_wts, sg_s, nfin, l,
                        tm=TM_MERGE, final_norm=final)
        pool_s.append(new_hist[:, 1:, :])
        sgv_s.append(vv.reshape(n_s, t_s, SG_WIDTH))

    return (hp.reshape(n_p, t_p, D_MODEL), hs.reshape(n_s, t_s, D_MODEL), jnp.stack(gla_p), gla_s,
            jnp.stack(pool_p), jnp.stack(pool_s), jnp.stack(sgv_s))
```

```python
import functools
import math

import jax
import jax.numpy as jnp
import numpy as np
from jax import lax
from jax.experimental import pallas as pl
from jax.experimental.pallas import tpu as pltpu

F32 = jnp.float32
BF16 = jnp.bfloat16

D_MODEL = 1024
GLA_HEADS = 4
GLA_KEY_DIM = 512
GLA_VAL_DIM = 1024
GLA_HEAD_K = 128
GLA_HEAD_V = 256
GLA_GATE_RANK = 16
GLA_GATE_NORMALIZER = 16.0
POOL_WIDTH = 512
POOL_WINDOWS = (2, 4, 8, 16)
POOL_GROUPS = 4
POOL_GROUP_DIM = 128
POOL_HIST = 15
SG_WIDTH = 512
SG_GROUPS = 4
SG_GROUP_DIM = 128
SG_CHUNK = 128
N_BRANCHES = 3
D_FF = 4096
EPS = 1e-6

SUBLANES = 8
LANES = 128
VMEM_LIMIT_BYTES = 56 * 1024 * 1024

C_QK = (0, 1024)
C_V = (1024, 2048)
C_GOUT = (2048, 3072)
C_XP = (3072, 3584)
C_U = (3584, 4096)
C_VV = (4096, 4608)
C_GATES = (4608, 7680)
C_GLR = (7680, 7808)
IN_COLS_R = 7808

TM_PROJ = 512
TM_MERGE = 512
PROJ_SLICE = 256
HEADS_IN_FLIGHT = 1
GLA_CHUNK = 128
GLA_BLOCK = SUBLANES
SAMPLE_SEQS_PER_STEP = 16
LOG2_DECAY_SCALE = math.log2(math.e) / GLA_GATE_NORMALIZER


def _resident(shape, layer=None):
    nd = len(shape)
    if layer is None:
        return pl.BlockSpec(shape, lambda *_: (0,) * nd, pipeline_mode=pl.Buffered(1))
    return pl.BlockSpec((None,) + tuple(shape), lambda *_: (layer,) + (0,) * nd,
                        pipeline_mode=pl.Buffered(1))


def _rms(x, w):
    return x * lax.rsqrt(jnp.mean(x * x, axis=-1, keepdims=True) + EPS) * w


def _dot(a, b):
    return jnp.dot(a, b, preferred_element_type=F32)


def _dot_nt(a, b):
    return lax.dot_general(a, b, (((1,), (1,)), ((), ())), preferred_element_type=F32)


def _row_iota(c, w):
    return lax.broadcasted_iota(jnp.int32, (c, w), 0)


def _bcast_row(x, blk, r):
    c, w = x.shape
    x3 = x.reshape(c // blk, blk, w)
    return jnp.broadcast_to(x3[:, r:r + 1, :], x3.shape).reshape(c, w)


def _seg_cumsum(g, seg):
    c, w = g.shape
    tpos = _row_iota(c, w) % SUBLANES
    for sh in (1, 2, 4):
        g = g + jnp.where(tpos >= sh, pltpu.roll(g, sh, 0), 0.0)
    if seg > SUBLANES:
        nb = seg // SUBLANES
        y = _bcast_row(g, SUBLANES, SUBLANES - 1).reshape(c // seg, nb, SUBLANES, w)
        sh = 1
        while sh < nb:
            y = y + jnp.concatenate([jnp.zeros((c // seg, sh, SUBLANES, w), F32), y[:, :nb - sh]], axis=1)
            sh *= 2
        excl = jnp.concatenate([jnp.zeros((c // seg, 1, SUBLANES, w), F32), y[:, :nb - 1]], axis=1)
        g = g + excl.reshape(c, w)
    return g


def _inproj_body(h_ref, nw_ref, w_ref, wgk_ref, bgk_ref,
                 qk_ref, vg_ref, xp_ref, u_ref, vv_ref, gt_ref, b2_ref, *, seg):
    xn = _rms(h_ref[...], nw_ref[...]).astype(BF16)

    def proj(lo, hi):
        return _dot(xn, w_ref[:, lo:hi])

    g2 = jax.nn.log_sigmoid(_dot(proj(*C_GLR).astype(BF16), wgk_ref[...]) + bgk_ref[...]) * LOG2_DECAY_SCALE
    b2_ref[...] = _seg_cumsum(g2, seg)
    vg_ref[:, GLA_VAL_DIM:] = jax.nn.silu(proj(*C_GOUT)).astype(BF16)
    u_ref[...] = jax.nn.gelu(proj(*C_U)).astype(u_ref.dtype)
    vv_ref[...] = jax.nn.gelu(proj(*C_VV)).astype(vv_ref.dtype)
    for j in range(N_BRANCHES):
        lo = C_GATES[0] + j * D_MODEL
        gt_ref[:, j * D_MODEL:(j + 1) * D_MODEL] = jax.nn.sigmoid(proj(lo, lo + D_MODEL)).astype(BF16)
    xp_ref[...] = proj(*C_XP)
    vg_ref[:, :GLA_VAL_DIM] = proj(*C_V).astype(BF16)
    qk_ref[...] = proj(*C_QK).astype(BF16)


def _inproj(h, norm_w, w_r, wgk, bgk, layer, *, tm, seg, vv_dtype):
    m = h.shape[0]
    row = lambda w: pl.BlockSpec((tm, w), lambda i: (i, 0))
    out_shape = (
        jax.ShapeDtypeStruct((m, 2 * GLA_KEY_DIM), BF16),
        jax.ShapeDtypeStruct((m, 2 * GLA_VAL_DIM), BF16),
        jax.ShapeDtypeStruct((m, POOL_WIDTH), F32),
        jax.ShapeDtypeStruct((m, SG_WIDTH), BF16),
        jax.ShapeDtypeStruct((m, SG_WIDTH), vv_dtype),
        jax.ShapeDtypeStruct((m, N_BRANCHES * D_MODEL), BF16),
        jax.ShapeDtypeStruct((m, GLA_KEY_DIM), F32),
    )
    return pl.pallas_call(
        functools.partial(_inproj_body, seg=seg),
        grid=(m // tm,),
        in_specs=[row(D_MODEL), _resident((1, D_MODEL), layer), _resident((D_MODEL, IN_COLS_R), layer),
                  _resident((LANES, GLA_KEY_DIM), layer), _resident((1, GLA_KEY_DIM), layer)],
        out_specs=[row(2 * GLA_KEY_DIM), row(2 * GLA_VAL_DIM), row(POOL_WIDTH), row(SG_WIDTH), row(SG_WIDTH),
                   row(N_BRANCHES * D_MODEL), row(GLA_KEY_DIM)],
        out_shape=out_shape,
        compiler_params=pltpu.CompilerParams(
            dimension_semantics=("arbitrary",), vmem_limit_bytes=VMEM_LIMIT_BYTES),
        name="inproj",
    )(h, norm_w, w_r, wgk, bgk)


CODE_BLOCK = 1


def _score_codes(c, levels):
    t = np.arange(c)[:, None]
    s = np.arange(c)[None, :]
    code = np.zeros((c, c), np.int32)
    code = np.where((t // GLA_BLOCK == s // GLA_BLOCK) & (s <= t), CODE_BLOCK, code)
    if levels:
        lvl, j = 2 * GLA_BLOCK, 0
        while lvl <= c:
            hit = (t // lvl == s // lvl) & (t // (lvl // 2) != s // (lvl // 2)) & (s < t)
            code = np.where(hit, CODE_BLOCK + 1 + j, code)
            lvl, j = lvl * 2, j + 1
    return jnp.asarray(code, jnp.int32)


def _block_expand_matrix(c):
    r = np.arange(c)[:, None]
    s = np.arange(c)[None, :]
    return jnp.asarray(r == s % GLA_BLOCK, BF16)


def _level_signs(c):
    t = np.arange(c)[:, None]
    out = []
    lvl = 2 * GLA_BLOCK
    while lvl <= c:
        out.append(np.broadcast_to(np.where(t % lvl >= lvl // 2, 1.0, -1.0), (c, GLA_HEAD_K)))
        lvl *= 2
    return jnp.asarray(np.stack(out), F32)


def _rows_bcast(ref, rows, lo, reps):
    parts = []
    for r in rows:
        tile = jnp.broadcast_to(ref[pl.ds(r, 1), lo:lo + GLA_HEAD_K], (SUBLANES, GLA_HEAD_K))
        parts.extend([tile] * reps)
    return jnp.concatenate(parts, axis=0)


def _run(steps):
    try:
        while True:
            next(steps)
    except StopIteration as stop:
        return stop.value


def _gla_scores_steps(q_ref, k_ref, b_ref, sel_ref, code_ref, r0, ks, c, sgn_ref=None):
    tile = lambda ref: ref[r0:r0 + c, ks:ks + GLA_HEAD_K]
    att = jnp.zeros((c, c), F32)
    if sgn_ref is not None:
        lvl, j = 2 * GLA_BLOCK, 0
        while lvl <= c:
            half = lvl // 2
            bmid = _rows_bcast(b_ref, [r0 + i * lvl + half - 1 for i in range(c // lvl)], ks, lvl // SUBLANES)
            e = jnp.exp2((tile(b_ref) - bmid) * sgn_ref[j])
            firsts = [slice(i * lvl, i * lvl + half) for i in range(c // lvl)]
            seconds = [slice(i * lvl + half, (i + 1) * lvl) for i in range(c // lvl)]
            q, k = tile(q_ref), tile(k_ref)
            ql = jnp.concatenate([q[rs] * e[rs] for rs in seconds], axis=0).astype(BF16)
            zero = jnp.zeros((half, GLA_HEAD_K), F32)
            kl = jnp.concatenate([x for rs in firsts for x in (k[rs] * e[rs], zero)], axis=0).astype(BF16)
            att_l = _dot_nt(ql, kl)
            pieces = []
            for i, (fs, ss) in enumerate(zip(firsts, seconds)):
                hit = code_ref[ss, :] == CODE_BLOCK + 1 + j
                pieces += [att[fs], jnp.where(hit, att_l[i * half:(i + 1) * half], att[ss])]
            att = jnp.concatenate(pieces, axis=0)
            lvl, j = lvl * 2, j + 1
            yield
    nblk = c // GLA_BLOCK
    lane = lax.broadcasted_iota(jnp.int32, (SUBLANES, c), 1)
    in_block = jnp.zeros((c, c), F32)
    for r in range(GLA_BLOCK):
        rows = [r0 + i * GLA_BLOCK + r for i in range(nblk)]
        br = _rows_bcast(b_ref, rows, ks, 1)
        kr = _rows_bcast(k_ref, rows, ks, 1)
        a = jnp.sum(tile(q_ref) * kr * jnp.exp2(jnp.minimum(tile(b_ref) - br, 0.0)), axis=-1, keepdims=True)
        in_block = jnp.where(jnp.concatenate([lane == r] * nblk, axis=0), a, in_block)
        yield
    same_block = _dot(in_block.astype(BF16), sel_ref[...])
    return jnp.where(code_ref[...] == CODE_BLOCK, same_block, att)


def _inproj_gla_body(h_ref, nw_ref, w_ref, wgk_ref, bgk_ref, code_ref, sgn_ref, sel_ref, gn_ref,
                     og_ref, xp_ref, u_ref, vv_ref, gt_ref, sfin_ref,
                     s_scr, q_scr, k_scr, v_scr, g_scr, b_scr, *, tiles_per_seq):
    i = pl.program_id(0)
    tm = h_ref.shape[0]
    c = GLA_CHUNK

    @pl.when(i % tiles_per_seq == 0)
    def _():
        s_scr[...] = jnp.zeros_like(s_scr)

    xn = _rms(h_ref[...], nw_ref[...]).astype(BF16)

    def proj(lo, hi):
        return _dot(xn, w_ref[:, lo:hi])

    def slice_task(lo, dst_ref, dst_lo, act):
        def run():
            dst_ref[:, dst_lo:dst_lo + PROJ_SLICE] = act(proj(lo, lo + PROJ_SLICE)).astype(dst_ref.dtype)
        return run

    def slice_tasks(cols, dst_ref, act=lambda x: x):
        return [slice_task(lo, dst_ref, lo - cols[0], act) for lo in range(cols[0], cols[1], PROJ_SLICE)]

    q_scale = GLA_HEAD_K ** -0.5
    head_tasks = (slice_tasks(C_GOUT, g_scr, jax.nn.silu) + slice_tasks(C_V, v_scr)
                  + slice_tasks((C_QK[0], C_QK[0] + GLA_KEY_DIM), q_scr, lambda x: x * q_scale)
                  + slice_tasks((C_QK[0] + GLA_KEY_DIM, C_QK[1]), k_scr))
    tails = (slice_tasks(C_U, u_ref, jax.nn.gelu) + slice_tasks(C_VV, vv_ref, jax.nn.gelu)
             + slice_tasks(C_GATES, gt_ref, jax.nn.sigmoid) + slice_tasks(C_XP, xp_ref))

    glr = proj(*C_GLR).astype(BF16)
    g2 = jax.nn.log_sigmoid(_dot(glr, wgk_ref[...]) + bgk_ref[...]) * LOG2_DECAY_SCALE
    per_head = len(head_tasks) // GLA_HEADS
    for h in range(GLA_HEADS):
        ks = h * GLA_HEAD_K
        b_scr[:, ks:ks + GLA_HEAD_K] = _seg_cumsum(g2[:, ks:ks + GLA_HEAD_K], c)
        for t in head_tasks[h * per_head:(h + 1) * per_head]:
            t()

    def pump(n=1):
        for _ in range(min(n, len(tails))):
            tails.pop(0)()

    def head_steps(r0, h):
        ks, kv = h * GLA_HEAD_K, h * GLA_HEAD_V
        tile = lambda ref: ref[r0:r0 + c, ks:ks + GLA_HEAD_K]
        v_bf = lambda: v_scr[r0:r0 + c, kv:kv + GLA_HEAD_V]
        s = s_scr[h]
        o = _dot((tile(q_scr) * jnp.exp2(tile(b_scr))).astype(BF16), s.astype(BF16))
        b_t = tile(b_scr).T
        b_last = b_t[:, c - 1:c]
        kd_t = (tile(k_scr).T * jnp.exp2(b_last - b_t)).astype(BF16)
        s_scr[h] = jnp.exp2(b_last) * s + _dot(kd_t, v_bf())
        yield
        att = yield from _gla_scores_steps(q_scr, k_scr, b_scr, sel_ref, code_ref, r0, ks, c, sgn_ref)
        o = o + _dot(att.astype(BF16), v_bf())
        yield
        gate = g_scr[r0:r0 + c, kv:kv + GLA_HEAD_V].astype(F32)
        og_ref[r0:r0 + c, kv:kv + GLA_HEAD_V] = (_rms(o, gn_ref[...]) * gate).astype(BF16)

    for ch, h0 in [(ch, h0) for ch in range(tm // c) for h0 in range(0, GLA_HEADS, HEADS_IN_FLIGHT)]:
        active = [head_steps(ch * c, h) for h in range(h0, h0 + HEADS_IN_FLIGHT)]
        while active:
            for steps in list(active):
                try:
                    next(steps)
                except StopIteration:
                    active.remove(steps)
            pump()
    pump(len(tails))

    @pl.when(i % tiles_per_seq == tiles_per_seq - 1)
    def _():
        sfin_ref[0] = s_scr[...]


def _inproj_gla(h, norm_w, w_r, wgk, bgk, gn, layer, *, tm, n_seq, seq_len):
    m = h.shape[0]
    c = GLA_CHUNK
    tiles_per_seq = seq_len // tm
    row = lambda w: pl.BlockSpec((tm, w), lambda i: (i, 0))
    out_shape = (
        jax.ShapeDtypeStruct((m, GLA_VAL_DIM), BF16),
        jax.ShapeDtypeStruct((m, POOL_WIDTH), F32),
        jax.ShapeDtypeStruct((m, SG_WIDTH), BF16),
        jax.ShapeDtypeStruct((m, SG_WIDTH), BF16),
        jax.ShapeDtypeStruct((m, N_BRANCHES * D_MODEL), BF16),
        jax.ShapeDtypeStruct((n_seq, GLA_HEADS, GLA_HEAD_K, GLA_HEAD_V), F32),
    )
    code, sgn, sel = _score_codes(c, True), _level_signs(c), _block_expand_matrix(c)
    return pl.pallas_call(
        functools.partial(_inproj_gla_body, tiles_per_seq=tiles_per_seq),
        grid=(m // tm,),
        in_specs=[row(D_MODEL), _resident((1, D_MODEL), layer), _resident((D_MODEL, IN_COLS_R), layer),
                  _resident((LANES, GLA_KEY_DIM), layer), _resident((1, GLA_KEY_DIM), layer),
                  _resident(code.shape), _resident(sgn.shape), _resident(sel.shape),
                  _resident((1, GLA_HEAD_V), layer)],
        out_specs=[row(GLA_VAL_DIM), row(POOL_WIDTH), row(SG_WIDTH), row(SG_WIDTH), row(N_BRANCHES * D_MODEL),
                   pl.BlockSpec((1, GLA_HEADS, GLA_HEAD_K, GLA_HEAD_V), lambda i: (i // tiles_per_seq, 0, 0, 0))],
        out_shape=out_shape,
        scratch_shapes=[pltpu.VMEM((GLA_HEADS, GLA_HEAD_K, GLA_HEAD_V), F32),
                        pltpu.VMEM((tm, GLA_KEY_DIM), F32), pltpu.VMEM((tm, GLA_KEY_DIM), F32),
                        pltpu.VMEM((tm, GLA_VAL_DIM), BF16), pltpu.VMEM((tm, GLA_VAL_DIM), BF16),
                        pltpu.VMEM((tm, GLA_KEY_DIM), F32)],
        compiler_params=pltpu.CompilerParams(
            dimension_semantics=("arbitrary",), vmem_limit_bytes=VMEM_LIMIT_BYTES),
        name="inproj_gla",
    )(h, norm_w, w_r, wgk, bgk, code, sgn, sel, gn)


def _gla_sample_body(qk_ref, vg_ref, b_ref, s0_ref, code_ref, sel_ref, gn_ref, *rest, seq_len, n_blk):
    og_ref, snew_ref, q_scr, k_scr = rest[-4:]
    nb = SAMPLE_SEQS_PER_STEP
    c = nb * seq_len

    @pl.when(pl.program_id(0) >= n_blk)
    def _():
        snew_ref[...] = jnp.zeros_like(snew_ref)

    @pl.when(pl.program_id(0) < n_blk)
    def _():
        q_scr[...] = qk_ref[:, :GLA_KEY_DIM].astype(F32) * (GLA_HEAD_K ** -0.5)
        k_scr[...] = qk_ref[:, GLA_KEY_DIM:].astype(F32)
        gn = gn_ref[...]
        lane_seq = lax.broadcasted_iota(jnp.int32, (GLA_HEAD_K, c), 1) // seq_len
        last_rows = [(n + 1) * seq_len - 1 for n in range(nb)]
        for h in range(GLA_HEADS):
            ks, kv = h * GLA_HEAD_K, h * GLA_HEAD_V
            k = k_scr[:, ks:ks + GLA_HEAD_K]
            b = b_ref[:, ks:ks + GLA_HEAD_K]
            v_bf = vg_ref[:, kv:kv + GLA_HEAD_V]
            gate = vg_ref[:, GLA_VAL_DIM + kv:GLA_VAL_DIM + kv + GLA_HEAD_V].astype(F32)
            qi = q_scr[:, ks:ks + GLA_HEAD_K] * jnp.exp2(b)
            inter = [
                _dot(qi[n * seq_len:(n + 1) * seq_len, :].astype(BF16), s0_ref[n, h].astype(BF16))
                for n in range(nb)
            ]
            att = _run(_gla_scores_steps(q_scr, k_scr, b_ref, sel_ref, code_ref, 0, ks, c))
            o = jnp.concatenate(inter, axis=0) + _dot(att.astype(BF16), v_bf)
            og_ref[:, kv:kv + GLA_HEAD_V] = (_rms(o, gn) * gate).astype(BF16)

            b_last_rows = _rows_bcast(b_ref, last_rows, ks, 1)
            kd_t = (k * jnp.exp2(b_last_rows - b)).T
            b_t = b.T
            for n in range(nb):
                col = last_rows[n]
                decay = jnp.exp2(b_t[:, col:col + 1])
                kd_n = jnp.where(lane_seq == n, kd_t, 0.0).astype(BF16)
                snew_ref[n, h] = decay * s0_ref[n, h] + _dot(kd_n, v_bf)


def _gla_sample(qk, vg, b2, state_all, gn, layer, prev_new_state, *, seq_len):
    assert seq_len == GLA_BLOCK
    nb = SAMPLE_SEQS_PER_STEP
    depth, n_seq = state_all.shape[:2]
    n_blk = n_seq // nb
    c = nb * seq_len
    first = prev_new_state is None
    assert first == (layer == 0)
    blk = (lambda i: jnp.minimum(i, n_blk - 1)) if first else (lambda i: i)
    row = lambda w: pl.BlockSpec((c, w), lambda i: (blk(i), 0))
    st_shape = (None, nb, GLA_HEADS, GLA_HEAD_K, GLA_HEAD_V)
    st_in = pl.BlockSpec(st_shape, lambda i: (layer, blk(i), 0, 0, 0))
    st_out = pl.BlockSpec(st_shape, lambda i: (layer + i // n_blk, i % n_blk, 0, 0, 0))
    in_specs = [row(2 * GLA_KEY_DIM), row(2 * GLA_VAL_DIM), row(GLA_KEY_DIM), st_in, _resident((c, c)),
                _resident((c, c)), _resident((1, GLA_HEAD_V), layer)]
    args = [qk, vg, b2, state_all, _score_codes(c, False), _block_expand_matrix(c), gn]
    aliases = {}
    if not first:
        in_specs.append(pl.BlockSpec(memory_space=pl.ANY))
        args.append(prev_new_state)
        aliases = {len(args) - 1: 1}
    return pl.pallas_call(
        functools.partial(_gla_sample_body, seq_len=seq_len, n_blk=n_blk),
        grid=(n_blk * depth if first else n_blk,),
        in_specs=in_specs,
        out_specs=[row(GLA_VAL_DIM), st_out],
        out_shape=(jax.ShapeDtypeStruct((n_seq * seq_len, GLA_VAL_DIM), BF16),
                   jax.ShapeDtypeStruct(state_all.shape, F32)),
        scratch_shapes=[pltpu.VMEM((c, GLA_KEY_DIM), F32), pltpu.VMEM((c, GLA_KEY_DIM), F32)],
        input_output_aliases=aliases,
        compiler_params=pltpu.CompilerParams(
            dimension_semantics=("arbitrary",), vmem_limit_bytes=VMEM_LIMIT_BYTES),
        name="gla_sample",
    )(*args)


def _window_sums(full):
    outs = []
    s = full
    w = 1
    for gi, win in enumerate(POOL_WINDOWS):
        while w < win:
            s = s + pltpu.roll(s, w, 0)
            w *= 2
        outs.append(s[:, gi * POOL_GROUP_DIM:(gi + 1) * POOL_GROUP_DIM])
    return jnp.concatenate(outs, axis=1)


def _pool_counts(pos, n_prev):
    grp = lax.broadcasted_iota(jnp.int32, pos.shape, pos.ndim - 1) // POOL_GROUP_DIM
    win = jnp.zeros(pos.shape, jnp.int32)
    for gi, w in enumerate(POOL_WINDOWS):
        win = jnp.where(grp == gi, w, win)
    return jnp.minimum(win, pos + 1 + n_prev).astype(F32)


POOL_PREV_ROWS = 2 * SUBLANES


def _pool_prompt_tile(x_ref, prev_ref, tile_in_seq):
    tp = x_ref.shape[0]
    x = x_ref[...]
    prev = jnp.where(tile_in_seq == 0, 0.0, prev_ref[...])
    sums = _window_sums(jnp.concatenate([prev, x], axis=0))[POOL_PREV_ROWS:, :]
    pos = _row_iota(tp, POOL_WIDTH) + tile_in_seq * tp
    return (sums / _pool_counts(pos, 0) - x).astype(BF16)


def _pool_sample_body(x_ref, hist_ref, o_ref, nh_ref, *, n_prev):
    nb, t, _ = x_ref.shape
    hr = hist_ref.shape[1]
    x = x_ref[...]
    full = jnp.concatenate([hist_ref[...], x], axis=1)
    sums = _window_sums(full.reshape(nb * (hr + t), POOL_WIDTH)).reshape(nb, hr + t, POOL_WIDTH)
    pos = lax.broadcasted_iota(jnp.int32, (nb, t, POOL_WIDTH), 1)
    o_ref[...] = (sums[:, hr:, :] / _pool_counts(pos, n_prev) - x).astype(BF16)
    nh_ref[...] = full[:, t:, :]


def _pool_sample(xp3, hist16, *, n_prev, nb):
    n, t, _ = xp3.shape
    hr = hist16.shape[1]
    blk = lambda r: pl.BlockSpec((nb, r, POOL_WIDTH), lambda i: (i, 0, 0))
    return pl.pallas_call(
        functools.partial(_pool_sample_body, n_prev=n_prev),
        grid=(n // nb,),
        in_specs=[blk(t), blk(hr)],
        out_specs=[blk(t), blk(hr)],
        out_shape=(jax.ShapeDtypeStruct((n, t, POOL_WIDTH), BF16),
                   jax.ShapeDtypeStruct((n, hr, POOL_WIDTH), F32)),
        compiler_params=pltpu.CompilerParams(dimension_semantics=("arbitrary",)),
        name="pool_sample",
    )(xp3, hist16)


def _merge_ffn_body(h_ref, og_ref, u_ref, vv_ref, gt_ref, *rest, final_norm, pool_tiles_per_seq):
    if pool_tiles_per_seq is None:
        pl_ref, rest = rest[0], rest[1:]
        pooled = lambda g: pl_ref[:, g * POOL_GROUP_DIM:(g + 1) * POOL_GROUP_DIM]
    else:
        xp_ref, prev_ref, rest = rest[0], rest[1], rest[2:]
        pooled_all = _pool_prompt_tile(xp_ref, prev_ref, pl.program_id(0) % pool_tiles_per_seq)
        pooled = lambda g: pooled_all[:, g * POOL_GROUP_DIM:(g + 1) * POOL_GROUP_DIM]
    (wmix_ref, pscale_ref, wsg_ref, bsg_ref, wa_ref, wb_ref, wc_ref, wo_ref,
     nf_ref, w1_ref, w2_ref, nfin_ref, o_ref) = rest
    tm = h_ref.shape[0]
    br = gt_ref[:, 0:D_MODEL].astype(F32) * _dot(og_ref[...], wa_ref[...])
    mixed = jnp.concatenate([_dot(pooled(g), wmix_ref[g]) for g in range(POOL_GROUPS)], axis=1)
    pool_out = (mixed * pscale_ref[...]).astype(BF16)
    br = br + gt_ref[:, D_MODEL:2 * D_MODEL].astype(F32) * _dot(pool_out, wb_ref[...])
    rows = []
    for c in range(tm // SG_CHUNK):
        r0 = c * SG_CHUNK
        vv = vv_ref[r0:r0 + SG_CHUNK, :].astype(BF16)
        sg = jnp.concatenate(
            [_dot(wsg_ref[g], vv[:, g * SG_GROUP_DIM:(g + 1) * SG_GROUP_DIM]) for g in range(SG_GROUPS)],
            axis=1) + bsg_ref[...]
        rows.append((u_ref[r0:r0 + SG_CHUNK, :].astype(F32) * sg).astype(BF16))
    sg_out = jnp.concatenate(rows, axis=0)
    br = br + gt_ref[:, 2 * D_MODEL:3 * D_MODEL].astype(F32) * _dot(sg_out, wc_ref[...])
    h = h_ref[...] + _dot(br.astype(BF16), wo_ref[...])
    a = jnp.maximum(_dot(_rms(h, nf_ref[...]).astype(BF16), w1_ref[...]), 0.0)
    h = h + _dot((a * a).astype(BF16), w2_ref[...])
    if final_norm:
        h = _rms(h, nfin_ref[...])
    o_ref[...] = h


def _merge_ffn(h, og, pool_in, u, vv, gates, layer_wts, sg_wts, nfin, layer, *, tm, final_norm, pool_seq_len=None):
    m = h.shape[0]
    row = lambda w: pl.BlockSpec((tm, w), lambda i: (i, 0))
    (wmix, pscale, wa, wb, wc, wo, nf, w1, w2) = layer_wts
    wsg, bsg = sg_wts
    lw = lambda w: _resident(w.shape[1:], layer)
    if pool_seq_len is None:
        pool_specs, pool_args, tiles_per_seq = [row(POOL_WIDTH)], [pool_in], None
    else:
        ratio = tm // POOL_PREV_ROWS
        pool_specs = [row(POOL_WIDTH),
                      pl.BlockSpec((POOL_PREV_ROWS, POOL_WIDTH), lambda i: (jnp.maximum(i * ratio - 1, 0), 0))]
        pool_args, tiles_per_seq = [pool_in, pool_in], pool_seq_len // tm
    return pl.pallas_call(
        functools.partial(_merge_ffn_body, final_norm=final_norm, pool_tiles_per_seq=tiles_per_seq),
        grid=(m // tm,),
        in_specs=[row(D_MODEL), row(GLA_VAL_DIM), row(SG_WIDTH), row(SG_WIDTH), row(N_BRANCHES * D_MODEL)]
        + pool_specs
        + [lw(wmix), lw(pscale), lw(wsg), lw(bsg), lw(wa), lw(wb), lw(wc), lw(wo), lw(nf), lw(w1), lw(w2),
           _resident(nfin.shape)],
        out_specs=row(D_MODEL),
        out_shape=jax.ShapeDtypeStruct((m, D_MODEL), F32),
        compiler_params=pltpu.CompilerParams(
            dimension_semantics=("arbitrary",), vmem_limit_bytes=VMEM_LIMIT_BYTES),
        name="merge_ffn",
    )(h, og, u, vv, gates, *pool_args, wmix, pscale, wsg, bsg, wa, wb, wc, wo, nf, w1, w2, nfin)


def _reorder_w_in_body(wt_ref, o_ref):
    glr0 = 2 * GLA_KEY_DIM + 2 * GLA_VAL_DIM
    glr1 = glr0 + GLA_GATE_RANK
    tk = wt_ref.shape[1]
    o_ref[:, :glr0] = wt_ref[:glr0, :].T.astype(BF16)
    o_ref[:, glr0:C_GLR[0]] = wt_ref[glr1:, :].T.astype(BF16)
    glr = jnp.concatenate([wt_ref[glr0:glr1, :], jnp.zeros((LANES - GLA_GATE_RANK, tk), F32)], axis=0)
    o_ref[:, C_GLR[0]:] = glr.T.astype(BF16)


def _reorder_w_in(w):
    depth, rows, cols = w.shape
    tk = 256
    return pl.pallas_call(
        _reorder_w_in_body,
        grid=(depth, rows // tk),
        in_specs=[pl.BlockSpec((None, cols, tk), lambda l, i: (l, 0, i))],
        out_specs=pl.BlockSpec((None, tk, IN_COLS_R), lambda l, i: (l, i, 0)),
        out_shape=jax.ShapeDtypeStruct((depth, rows, IN_COLS_R), BF16),
        compiler_params=pltpu.CompilerParams(dimension_semantics=("arbitrary", "arbitrary")),
        name="reorder_w_in",
    )(jnp.swapaxes(w, 1, 2))


def _sg_operands(w_spatial, b_spatial, length):
    reps = SG_CHUNK // length
    t = jnp.arange(SG_CHUNK)
    w = jnp.tile(w_spatial[:, :, :length, :length], (1, 1, reps, reps))
    keep = ((t[:, None] // length) == (t[None, :] // length)) & ((t[None, :] % length) <= (t[:, None] % length))
    w = jnp.where(keep, w, 0.0).astype(BF16)
    bias = jnp.repeat(jnp.swapaxes(jnp.tile(b_spatial[:, :, :length], (1, 1, reps)), 1, 2), SG_GROUP_DIM, axis=2)
    return w, bias.astype(F32)


def kernel(x_prompt, x_sample, state_gla, state_pool, norm_mix, w_in, w_gk2, b_gk, gla_norm, w_pool_mix,
           pool_scale, w_spatial, b_spatial, w_br_a, w_br_b, w_br_c, w_out, norm_ffn, w_ff1, w_ff2,
           norm_final):
    depth = w_in.shape[0]
    n_p, t_p, _ = x_prompt.shape
    n_s, t_s, _ = x_sample.shape
    hp = x_prompt.reshape(n_p * t_p, D_MODEL)
    hs = x_sample.reshape(n_s * t_s, D_MODEL)
    n_prev_s = POOL_HIST
    rows = lambda a: a.reshape(depth, 1, -1).astype(F32)
    bf = lambda a: a.astype(BF16)

    w_r = _reorder_w_in(w_in)
    wgk = bf(jnp.pad(w_gk2, ((0, 0), (0, LANES - GLA_GATE_RANK), (0, 0))))
    bgk, gn, nmix = rows(b_gk), rows(gla_norm), rows(norm_mix)
    layer_wts = (bf(w_pool_mix), rows(pool_scale), bf(w_br_a), bf(w_br_b), bf(w_br_c), bf(w_out),
                 rows(norm_ffn), bf(w_ff1), bf(w_ff2))
    sg_p = _sg_operands(w_spatial, b_spatial, min(t_p, SG_CHUNK))
    sg_s = _sg_operands(w_spatial, b_spatial, min(t_s, SG_CHUNK))
    nfin = norm_final.reshape(1, -1).astype(F32)
    hist16 = jnp.pad(state_pool, ((0, 0), (0, 0), (1, 0), (0, 0)))

    gla_p, pool_p, pool_s, sgv_s = [], [], [], []
    gla_s = None
    for l in range(depth):
        final = l == depth - 1
        og, xp, u, vv, gates, s_fin = _inproj_gla(hp, nmix, w_r, wgk, bgk, gn, l, tm=TM_PROJ, n_seq=n_p,
                                                  seq_len=t_p)
        hp = _merge_ffn(hp, og, xp, u, vv, gates, layer_wts, sg_p, nfin, l, tm=TM_MERGE, final_norm=final,
                        pool_seq_len=t_p)
        gla_p.append(s_fin)
        pool_p.append(xp.reshape(n_p, t_p, POOL_WIDTH)[:, t_p - POOL_HIST:, :])

        qk, vg, xp, u, vv, gates, b2 = _inproj(hs, nmix, w_r, wgk, bgk, l, tm=TM_PROJ, seg=t_s, vv_dtype=F32)
        og, gla_s = _gla_sample(qk, vg, b2, state_gla, gn, l, gla_s, seq_len=t_s)
        pooled, new_hist = _pool_sample(xp.reshape(n_s, t_s, POOL_WIDTH), hist16[l], n_prev=n_prev_s, nb=16)
        hs = _merge_ffn(hs, og, pooled.reshape(n_s * t_s, POOL_WIDTH), u, vv, gates, layer_wts, sg_s, nfin, l,
                        tm=TM_MERGE, final_norm=final)
        pool_s.append(new_hist[:, 1:, :])
        sgv_s.append(vv.reshape(n_s, t_s, SG_WIDTH))

    return (hp.reshape(n_p, t_p, D_MODEL), hs.reshape(n_s, t_s, D_MODEL), jnp.stack(gla_p), gla_s,
            jnp.stack(pool_p), jnp.stack(pool_s), jnp.stack(sgv_s))
```

```python
import functools
import math

import jax
import jax.numpy as jnp
import numpy as np
from jax import lax
from jax.experimental import pallas as pl
from jax.experimental.pallas import tpu as pltpu

F32 = jnp.float32
BF16 = jnp.bfloat16

D_MODEL = 1024
GLA_HEADS = 4
GLA_KEY_DIM = 512
GLA_VAL_DIM = 1024
GLA_HEAD_K = 128
GLA_HEAD_V = 256
GLA_GATE_RANK = 16
GLA_GATE_NORMALIZER = 16.0
POOL_WIDTH = 512
POOL_WINDOWS = (2, 4, 8, 16)
POOL_GROUPS = 4
POOL_GROUP_DIM = 128
POOL_HIST = 15
SG_WIDTH = 512
SG_GROUPS = 4
SG_GROUP_DIM = 128
SG_CHUNK = 128
N_BRANCHES = 3
D_FF = 4096
EPS = 1e-6

SUBLANES = 8
LANES = 128
VMEM_LIMIT_BYTES = 56 * 1024 * 1024

C_QK = (0, 1024)
C_V = (1024, 2048)
C_GOUT = (2048, 3072)
C_XP = (3072, 3584)
C_U = (3584, 4096)
C_VV = (4096, 4608)
C_GATES = (4608, 7680)
C_GLR = (7680, 7808)
IN_COLS_R = 7808

TM_PROJ = 512
TM_MERGE = 512
PROJ_SLICE = 256
HEADS_IN_FLIGHT = 1
GLA_CHUNK = 128
GLA_BLOCK = SUBLANES
SAMPLE_SEQS_PER_STEP = 16
LOG2_DECAY_SCALE = math.log2(math.e) / GLA_GATE_NORMALIZER


def _resident(shape, layer=None):
    nd = len(shape)
    if layer is None:
        return pl.BlockSpec(shape, lambda *_: (0,) * nd, pipeline_mode=pl.Buffered(1))
    return pl.BlockSpec((None,) + tuple(shape), lambda *_: (layer,) + (0,) * nd,
                        pipeline_mode=pl.Buffered(1))


def _rms(x, w):
    return x * lax.rsqrt(jnp.mean(x * x, axis=-1, keepdims=True) + EPS) * w


def _dot(a, b):
    return jnp.dot(a, b, preferred_element_type=F32)


def _dot_nt(a, b):
    return lax.dot_general(a, b, (((1,), (1,)), ((), ())), preferred_element_type=F32)


def _row_iota(c, w):
    return lax.broadcasted_iota(jnp.int32, (c, w), 0)


def _bcast_row(x, blk, r):
    c, w = x.shape
    x3 = x.reshape(c // blk, blk, w)
    return jnp.broadcast_to(x3[:, r:r + 1, :], x3.shape).reshape(c, w)


def _seg_cumsum(g, seg):
    c, w = g.shape
    tpos = _row_iota(c, w) % SUBLANES
    for sh in (1, 2, 4):
        g = g + jnp.where(tpos >= sh, pltpu.roll(g, sh, 0), 0.0)
    if seg > SUBLANES:
        nb = seg // SUBLANES
        y = _bcast_row(g, SUBLANES, SUBLANES - 1).reshape(c // seg, nb, SUBLANES, w)
        sh = 1
        while sh < nb:
            y = y + jnp.concatenate([jnp.zeros((c // seg, sh, SUBLANES, w), F32), y[:, :nb - sh]], axis=1)
            sh *= 2
        excl = jnp.concatenate([jnp.zeros((c // seg, 1, SUBLANES, w), F32), y[:, :nb - 1]], axis=1)
        g = g + excl.reshape(c, w)
    return g


def _inproj_body(h_ref, nw_ref, w_ref, wgk_ref, bgk_ref,
                 qk_ref, vg_ref, xp_ref, u_ref, vv_ref, gt_ref, b2_ref, *, seg):
    xn = _rms(h_ref[...], nw_ref[...]).astype(BF16)

    def proj(lo, hi):
        return _dot(xn, w_ref[:, lo:hi])

    g2 = jax.nn.log_sigmoid(_dot(proj(*C_GLR).astype(BF16), wgk_ref[...]) + bgk_ref[...]) * LOG2_DECAY_SCALE
    b2_ref[...] = _seg_cumsum(g2, seg)
    vg_ref[:, GLA_VAL_DIM:] = jax.nn.silu(proj(*C_GOUT)).astype(BF16)
    u_ref[...] = jax.nn.gelu(proj(*C_U)).astype(u_ref.dtype)
    vv_ref[...] = jax.nn.gelu(proj(*C_VV)).astype(vv_ref.dtype)
    for j in range(N_BRANCHES):
        lo = C_GATES[0] + j * D_MODEL
        gt_ref[:, j * D_MODEL:(j + 1) * D_MODEL] = jax.nn.sigmoid(proj(lo, lo + D_MODEL)).astype(BF16)
    xp_ref[...] = proj(*C_XP)
    vg_ref[:, :GLA_VAL_DIM] = proj(*C_V).astype(BF16)
    qk_ref[...] = proj(*C_QK).astype(BF16)


def _inproj(h, norm_w, w_r, wgk, bgk, layer, *, tm, seg, vv_dtype):
    m = h.shape[0]
    row = lambda w: pl.BlockSpec((tm, w), lambda i: (i, 0))
    out_shape = (
        jax.ShapeDtypeStruct((m, 2 * GLA_KEY_DIM), BF16),
        jax.ShapeDtypeStruct((m, 2 * GLA_VAL_DIM), BF16),
        jax.ShapeDtypeStruct((m, POOL_WIDTH), F32),
        jax.ShapeDtypeStruct((m, SG_WIDTH), BF16),
        jax.ShapeDtypeStruct((m, SG_WIDTH), vv_dtype),
        jax.ShapeDtypeStruct((m, N_BRANCHES * D_MODEL), BF16),
        jax.ShapeDtypeStruct((m, GLA_KEY_DIM), F32),
    )
    return pl.pallas_call(
        functools.partial(_inproj_body, seg=seg),
        grid=(m // tm,),
        in_specs=[row(D_MODEL), _resident((1, D_MODEL), layer), _resident((D_MODEL, IN_COLS_R), layer),
                  _resident((LANES, GLA_KEY_DIM), layer), _resident((1, GLA_KEY_DIM), layer)],
        out_specs=[row(2 * GLA_KEY_DIM), row(2 * GLA_VAL_DIM), row(POOL_WIDTH), row(SG_WIDTH), row(SG_WIDTH),
                   row(N_BRANCHES * D_MODEL), row(GLA_KEY_DIM)],
        out_shape=out_shape,
        compiler_params=pltpu.CompilerParams(
            dimension_semantics=("arbitrary",), vmem_limit_bytes=VMEM_LIMIT_BYTES),
        name="inproj",
    )(h, norm_w, w_r, wgk, bgk)


CODE_BLOCK = 1
CODE_LEVEL = CODE_BLOCK + GLA_BLOCK


def _score_codes(c, levels):
    t = np.arange(c)[:, None]
    s = np.arange(c)[None, :]
    code = np.zeros((c, c), np.int32)
    code = np.where((t // GLA_BLOCK == s // GLA_BLOCK) & (s <= t), CODE_BLOCK + s % GLA_BLOCK, code)
    if levels:
        lvl, j = 2 * GLA_BLOCK, 0
        while lvl <= c:
            hit = (t // lvl == s // lvl) & (t // (lvl // 2) != s // (lvl // 2)) & (s < t)
            code = np.where(hit, CODE_LEVEL + j, code)
            lvl, j = lvl * 2, j + 1
    return jnp.asarray(code, jnp.int32)


def _level_signs(c):
    t = np.arange(c)[:, None]
    out = []
    lvl = 2 * GLA_BLOCK
    while lvl <= c:
        out.append(np.broadcast_to(np.where(t % lvl >= lvl // 2, 1.0, -1.0), (c, GLA_HEAD_K)))
        lvl *= 2
    return jnp.asarray(np.stack(out), F32)


def _rows_bcast(ref, rows, lo, reps):
    parts = []
    for r in rows:
        tile = jnp.broadcast_to(ref[pl.ds(r, 1), lo:lo + GLA_HEAD_K], (SUBLANES, GLA_HEAD_K))
        parts.extend([tile] * reps)
    return jnp.concatenate(parts, axis=0)


def _run(steps):
    try:
        while True:
            next(steps)
    except StopIteration as stop:
        return stop.value


def _gla_scores_steps(q_ref, k_ref, b_ref, code_ref, r0, ks, c, sgn_ref=None):
    tile = lambda ref: ref[r0:r0 + c, ks:ks + GLA_HEAD_K]
    att = jnp.zeros((c, c), F32)
    if sgn_ref is not None:
        lvl, j = 2 * GLA_BLOCK, 0
        while lvl <= c:
            half = lvl // 2
            bmid = _rows_bcast(b_ref, [r0 + i * lvl + half - 1 for i in range(c // lvl)], ks, lvl // SUBLANES)
            e = jnp.exp2((tile(b_ref) - bmid) * sgn_ref[j])
            firsts = [slice(i * lvl, i * lvl + half) for i in range(c // lvl)]
            seconds = [slice(i * lvl + half, (i + 1) * lvl) for i in range(c // lvl)]
            q, k = tile(q_ref), tile(k_ref)
            ql = jnp.concatenate([q[rs] * e[rs] for rs in seconds], axis=0).astype(BF16)
            zero = jnp.zeros((half, GLA_HEAD_K), F32)
            kl = jnp.concatenate([x for rs in firsts for x in (k[rs] * e[rs], zero)], axis=0).astype(BF16)
            att_l = _dot_nt(ql, kl)
            pieces = []
            for i, (fs, ss) in enumerate(zip(firsts, seconds)):
                hit = code_ref[ss, :] == CODE_LEVEL + j
                pieces += [att[fs], jnp.where(hit, att_l[i * half:(i + 1) * half], att[ss])]
            att = jnp.concatenate(pieces, axis=0)
            lvl, j = lvl * 2, j + 1
            yield
    nblk = c // GLA_BLOCK
    for r in range(GLA_BLOCK):
        rows = [r0 + i * GLA_BLOCK + r for i in range(nblk)]
        br = _rows_bcast(b_ref, rows, ks, 1)
        kr = _rows_bcast(k_ref, rows, ks, 1)
        a = jnp.sum(tile(q_ref) * kr * jnp.exp2(tile(b_ref) - br), axis=-1, keepdims=True)
        att = jnp.where(code_ref[...] == CODE_BLOCK + r, a, att)
        yield
    return att


def _inproj_gla_body(h_ref, nw_ref, w_ref, wgk_ref, bgk_ref, code_ref, sgn_ref, gn_ref,
                     og_ref, xp_ref, u_ref, vv_ref, gt_ref, sfin_ref,
                     s_scr, q_scr, k_scr, v_scr, g_scr, b_scr, *, tiles_per_seq):
    i = pl.program_id(0)
    tm = h_ref.shape[0]
    c = GLA_CHUNK

    @pl.when(i % tiles_per_seq == 0)
    def _():
        s_scr[...] = jnp.zeros_like(s_scr)

    xn = _rms(h_ref[...], nw_ref[...]).astype(BF16)

    def proj(lo, hi):
        return _dot(xn, w_ref[:, lo:hi])

    def slice_task(lo, dst_ref, dst_lo, act):
        def run():
            dst_ref[:, dst_lo:dst_lo + PROJ_SLICE] = act(proj(lo, lo + PROJ_SLICE)).astype(dst_ref.dtype)
        return run

    def slice_tasks(cols, dst_ref, act=lambda x: x):
        return [slice_task(lo, dst_ref, lo - cols[0], act) for lo in range(cols[0], cols[1], PROJ_SLICE)]

    q_scale = GLA_HEAD_K ** -0.5
    head_tasks = (slice_tasks(C_GOUT, g_scr, jax.nn.silu) + slice_tasks(C_V, v_scr)
                  + slice_tasks((C_QK[0], C_QK[0] + GLA_KEY_DIM), q_scr, lambda x: x * q_scale)
                  + slice_tasks((C_QK[0] + GLA_KEY_DIM, C_QK[1]), k_scr))
    tails = (slice_tasks(C_U, u_ref, jax.nn.gelu) + slice_tasks(C_VV, vv_ref, jax.nn.gelu)
             + slice_tasks(C_GATES, gt_ref, jax.nn.sigmoid) + slice_tasks(C_XP, xp_ref))

    glr = proj(*C_GLR).astype(BF16)
    g2 = jax.nn.log_sigmoid(_dot(glr, wgk_ref[...]) + bgk_ref[...]) * LOG2_DECAY_SCALE
    per_head = len(head_tasks) // GLA_HEADS
    for h in range(GLA_HEADS):
        ks = h * GLA_HEAD_K
        b_scr[:, ks:ks + GLA_HEAD_K] = _seg_cumsum(g2[:, ks:ks + GLA_HEAD_K], c)
        for t in head_tasks[h * per_head:(h + 1) * per_head]:
            t()

    def pump(n=1):
        for _ in range(min(n, len(tails))):
            tails.pop(0)()

    def head_steps(r0, h):
        ks, kv = h * GLA_HEAD_K, h * GLA_HEAD_V
        tile = lambda ref: ref[r0:r0 + c, ks:ks + GLA_HEAD_K]
        v_bf = lambda: v_scr[r0:r0 + c, kv:kv + GLA_HEAD_V]
        s = s_scr[h]
        o = _dot((tile(q_scr) * jnp.exp2(tile(b_scr))).astype(BF16), s.astype(BF16))
        b_t = tile(b_scr).T
        b_last = b_t[:, c - 1:c]
        kd_t = (tile(k_scr).T * jnp.exp2(b_last - b_t)).astype(BF16)
        s_scr[h] = jnp.exp2(b_last) * s + _dot(kd_t, v_bf())
        yield
        att = yield from _gla_scores_steps(q_scr, k_scr, b_scr, code_ref, r0, ks, c, sgn_ref)
        o = o + _dot(att.astype(BF16), v_bf())
        yield
        gate = g_scr[r0:r0 + c, kv:kv + GLA_HEAD_V].astype(F32)
        og_ref[r0:r0 + c, kv:kv + GLA_HEAD_V] = (_rms(o, gn_ref[...]) * gate).astype(BF16)

    for ch, h0 in [(ch, h0) for ch in range(tm // c) for h0 in range(0, GLA_HEADS, HEADS_IN_FLIGHT)]:
        active = [head_steps(ch * c, h) for h in range(h0, h0 + HEADS_IN_FLIGHT)]
        while active:
            for steps in list(active):
                try:
                    next(steps)
                except StopIteration:
                    active.remove(steps)
            pump()
    pump(len(tails))

    @pl.when(i % tiles_per_seq == tiles_per_seq - 1)
    def _():
        sfin_ref[0] = s_scr[...]


def _inproj_gla(h, norm_w, w_r, wgk, bgk, gn, layer, *, tm, n_seq, seq_len):
    m = h.shape[0]
    c = GLA_CHUNK
    tiles_per_seq = seq_len // tm
    row = lambda w: pl.BlockSpec((tm, w), lambda i: (i, 0))
    out_shape = (
        jax.ShapeDtypeStruct((m, GLA_VAL_DIM), BF16),
        jax.ShapeDtypeStruct((m, POOL_WIDTH), F32),
        jax.ShapeDtypeStruct((m, SG_WIDTH), BF16),
        jax.ShapeDtypeStruct((m, SG_WIDTH), BF16),
        jax.ShapeDtypeStruct((m, N_BRANCHES * D_MODEL), BF16),
        jax.ShapeDtypeStruct((n_seq, GLA_HEADS, GLA_HEAD_K, GLA_HEAD_V), F32),
    )
    code, sgn = _score_codes(c, True), _level_signs(c)
    return pl.pallas_call(
        functools.partial(_inproj_gla_body, tiles_per_seq=tiles_per_seq),
        grid=(m // tm,),
        in_specs=[row(D_MODEL), _resident((1, D_MODEL), layer), _resident((D_MODEL, IN_COLS_R), layer),
                  _resident((LANES, GLA_KEY_DIM), layer), _resident((1, GLA_KEY_DIM), layer),
                  _resident(code.shape), _resident(sgn.shape), _resident((1, GLA_HEAD_V), layer)],
        out_specs=[row(GLA_VAL_DIM), row(POOL_WIDTH), row(SG_WIDTH), row(SG_WIDTH), row(N_BRANCHES * D_MODEL),
                   pl.BlockSpec((1, GLA_HEADS, GLA_HEAD_K, GLA_HEAD_V), lambda i: (i // tiles_per_seq, 0, 0, 0))],
        out_shape=out_shape,
        scratch_shapes=[pltpu.VMEM((GLA_HEADS, GLA_HEAD_K, GLA_HEAD_V), F32),
                        pltpu.VMEM((tm, GLA_KEY_DIM), F32), pltpu.VMEM((tm, GLA_KEY_DIM), F32),
                        pltpu.VMEM((tm, GLA_VAL_DIM), BF16), pltpu.VMEM((tm, GLA_VAL_DIM), BF16),
                        pltpu.VMEM((tm, GLA_KEY_DIM), F32)],
        compiler_params=pltpu.CompilerParams(
            dimension_semantics=("arbitrary",), vmem_limit_bytes=VMEM_LIMIT_BYTES),
        name="inproj_gla",
    )(h, norm_w, w_r, wgk, bgk, code, sgn, gn)


def _gla_sample_body(qk_ref, vg_ref, b_ref, s0_ref, code_ref, gn_ref, *rest, seq_len, n_blk):
    og_ref, snew_ref, q_scr, k_scr = rest[-4:]
    nb = SAMPLE_SEQS_PER_STEP
    c = nb * seq_len

    @pl.when(pl.program_id(0) >= n_blk)
    def _():
        snew_ref[...] = jnp.zeros_like(snew_ref)

    @pl.when(pl.program_id(0) < n_blk)
    def _():
        q_scr[...] = qk_ref[:, :GLA_KEY_DIM].astype(F32) * (GLA_HEAD_K ** -0.5)
        k_scr[...] = qk_ref[:, GLA_KEY_DIM:].astype(F32)
        gn = gn_ref[...]
        lane_seq = lax.broadcasted_iota(jnp.int32, (GLA_HEAD_K, c), 1) // seq_len
        last_rows = [(n + 1) * seq_len - 1 for n in range(nb)]
        for h in range(GLA_HEADS):
            ks, kv = h * GLA_HEAD_K, h * GLA_HEAD_V
            k = k_scr[:, ks:ks + GLA_HEAD_K]
            b = b_ref[:, ks:ks + GLA_HEAD_K]
            v_bf = vg_ref[:, kv:kv + GLA_HEAD_V]
            gate = vg_ref[:, GLA_VAL_DIM + kv:GLA_VAL_DIM + kv + GLA_HEAD_V].astype(F32)
            qi = q_scr[:, ks:ks + GLA_HEAD_K] * jnp.exp2(b)
            inter = [
                _dot(qi[n * seq_len:(n + 1) * seq_len, :].astype(BF16), s0_ref[n, h].astype(BF16))
                for n in range(nb)
            ]
            att = _run(_gla_scores_steps(q_scr, k_scr, b_ref, code_ref, 0, ks, c))
            o = jnp.concatenate(inter, axis=0) + _dot(att.astype(BF16), v_bf)
            og_ref[:, kv:kv + GLA_HEAD_V] = (_rms(o, gn) * gate).astype(BF16)

            b_last_rows = _rows_bcast(b_ref, last_rows, ks, 1)
            kd_t = (k * jnp.exp2(b_last_rows - b)).T
            b_t = b.T
            for n in range(nb):
                col = last_rows[n]
                decay = jnp.exp2(b_t[:, col:col + 1])
                kd_n = jnp.where(lane_seq == n, kd_t, 0.0).astype(BF16)
                snew_ref[n, h] = decay * s0_ref[n, h] + _dot(kd_n, v_bf)


def _gla_sample(qk, vg, b2, state_all, gn, layer, prev_new_state, *, seq_len):
    assert seq_len == GLA_BLOCK
    nb = SAMPLE_SEQS_PER_STEP
    depth, n_seq = state_all.shape[:2]
    n_blk = n_seq // nb
    c = nb * seq_len
    first = prev_new_state is None
    assert first == (layer == 0)
    blk = (lambda i: jnp.minimum(i, n_blk - 1)) if first else (lambda i: i)
    row = lambda w: pl.BlockSpec((c, w), lambda i: (blk(i), 0))
    st_shape = (None, nb, GLA_HEADS, GLA_HEAD_K, GLA_HEAD_V)
    st_in = pl.BlockSpec(st_shape, lambda i: (layer, blk(i), 0, 0, 0))
    st_out = pl.BlockSpec(st_shape, lambda i: (layer + i // n_blk, i % n_blk, 0, 0, 0))
    in_specs = [row(2 * GLA_KEY_DIM), row(2 * GLA_VAL_DIM), row(GLA_KEY_DIM), st_in, _resident((c, c)),
                _resident((1, GLA_HEAD_V), layer)]
    args = [qk, vg, b2, state_all, _score_codes(c, False), gn]
    aliases = {}
    if not first:
        in_specs.append(pl.BlockSpec(memory_space=pl.ANY))
        args.append(prev_new_state)
        aliases = {len(args) - 1: 1}
    return pl.pallas_call(
        functools.partial(_gla_sample_body, seq_len=seq_len, n_blk=n_blk),
        grid=(n_blk * depth if first else n_blk,),
        in_specs=in_specs,
        out_specs=[row(GLA_VAL_DIM), st_out],
        out_shape=(jax.ShapeDtypeStruct((n_seq * seq_len, GLA_VAL_DIM), BF16),
                   jax.ShapeDtypeStruct(state_all.shape, F32)),
        scratch_shapes=[pltpu.VMEM((c, GLA_KEY_DIM), F32), pltpu.VMEM((c, GLA_KEY_DIM), F32)],
        input_output_aliases=aliases,
        compiler_params=pltpu.CompilerParams(
            dimension_semantics=("arbitrary",), vmem_limit_bytes=VMEM_LIMIT_BYTES),
        name="gla_sample",
    )(*args)


def _window_sums(full):
    outs = []
    s = full
    w = 1
    for gi, win in enumerate(POOL_WINDOWS):
        while w < win:
            s = s + pltpu.roll(s, w, 0)
            w *= 2
        outs.append(s[:, gi * POOL_GROUP_DIM:(gi + 1) * POOL_GROUP_DIM])
    return jnp.concatenate(outs, axis=1)


def _pool_counts(pos, n_prev):
    grp = lax.broadcasted_iota(jnp.int32, pos.shape, pos.ndim - 1) // POOL_GROUP_DIM
    win = jnp.zeros(pos.shape, jnp.int32)
    for gi, w in enumerate(POOL_WINDOWS):
        win = jnp.where(grp == gi, w, win)
    return jnp.minimum(win, pos + 1 + n_prev).astype(F32)


POOL_PREV_ROWS = 2 * SUBLANES


def _pool_prompt_tile(x_ref, prev_ref, tile_in_seq):
    tp = x_ref.shape[0]
    x = x_ref[...]
    prev = jnp.where(tile_in_seq == 0, 0.0, prev_ref[...])
    sums = _window_sums(jnp.concatenate([prev, x], axis=0))[POOL_PREV_ROWS:, :]
    pos = _row_iota(tp, POOL_WIDTH) + tile_in_seq * tp
    return (sums / _pool_counts(pos, 0) - x).astype(BF16)


def _pool_sample_body(x_ref, hist_ref, o_ref, nh_ref, *, n_prev):
    nb, t, _ = x_ref.shape
    hr = hist_ref.shape[1]
    x = x_ref[...]
    full = jnp.concatenate([hist_ref[...], x], axis=1)
    sums = _window_sums(full.reshape(nb * (hr + t), POOL_WIDTH)).reshape(nb, hr + t, POOL_WIDTH)
    pos = lax.broadcasted_iota(jnp.int32, (nb, t, POOL_WIDTH), 1)
    o_ref[...] = (sums[:, hr:, :] / _pool_counts(pos, n_prev) - x).astype(BF16)
    nh_ref[...] = full[:, t:, :]


def _pool_sample(xp3, hist16, *, n_prev, nb):
    n, t, _ = xp3.shape
    hr = hist16.shape[1]
    blk = lambda r: pl.BlockSpec((nb, r, POOL_WIDTH), lambda i: (i, 0, 0))
    return pl.pallas_call(
        functools.partial(_pool_sample_body, n_prev=n_prev),
        grid=(n // nb,),
        in_specs=[blk(t), blk(hr)],
        out_specs=[blk(t), blk(hr)],
        out_shape=(jax.ShapeDtypeStruct((n, t, POOL_WIDTH), BF16),
                   jax.ShapeDtypeStruct((n, hr, POOL_WIDTH), F32)),
        compiler_params=pltpu.CompilerParams(dimension_semantics=("arbitrary",)),
        name="pool_sample",
    )(xp3, hist16)


def _merge_ffn_body(h_ref, og_ref, u_ref, vv_ref, gt_ref, *rest, final_norm, pool_tiles_per_seq):
    if pool_tiles_per_seq is None:
        pl_ref, rest = rest[0], rest[1:]
        pooled = lambda g: pl_ref[:, g * POOL_GROUP_DIM:(g + 1) * POOL_GROUP_DIM]
    else:
        xp_ref, prev_ref, rest = rest[0], rest[1], rest[2:]
        pooled_all = _pool_prompt_tile(xp_ref, prev_ref, pl.program_id(0) % pool_tiles_per_seq)
        pooled = lambda g: pooled_all[:, g * POOL_GROUP_DIM:(g + 1) * POOL_GROUP_DIM]
    (wmix_ref, pscale_ref, wsg_ref, bsg_ref, wa_ref, wb_ref, wc_ref, wo_ref,
     nf_ref, w1_ref, w2_ref, nfin_ref, o_ref) = rest
    tm = h_ref.shape[0]
    br = gt_ref[:, 0:D_MODEL].astype(F32) * _dot(og_ref[...], wa_ref[...])
    mixed = jnp.concatenate([_dot(pooled(g), wmix_ref[g]) for g in range(POOL_GROUPS)], axis=1)
    pool_out = (mixed * pscale_ref[...]).astype(BF16)
    br = br + gt_ref[:, D_MODEL:2 * D_MODEL].astype(F32) * _dot(pool_out, wb_ref[...])
    rows = []
    for c in range(tm // SG_CHUNK):
        r0 = c * SG_CHUNK
        vv = vv_ref[r0:r0 + SG_CHUNK, :].astype(BF16)
        sg = jnp.concatenate(
            [_dot(wsg_ref[g], vv[:, g * SG_GROUP_DIM:(g + 1) * SG_GROUP_DIM]) for g in range(SG_GROUPS)],
            axis=1) + bsg_ref[...]
        rows.append((u_ref[r0:r0 + SG_CHUNK, :].astype(F32) * sg).astype(BF16))
    sg_out = jnp.concatenate(rows, axis=0)
    br = br + gt_ref[:, 2 * D_MODEL:3 * D_MODEL].astype(F32) * _dot(sg_out, wc_ref[...])
    h = h_ref[...] + _dot(br.astype(BF16), wo_ref[...])
    a = jnp.maximum(_dot(_rms(h, nf_ref[...]).astype(BF16), w1_ref[...]), 0.0)
    h = h + _dot((a * a).astype(BF16), w2_ref[...])
    if final_norm:
        h = _rms(h, nfin_ref[...])
    o_ref[...] = h


def _merge_ffn(h, og, pool_in, u, vv, gates, layer_wts, sg_wts, nfin, layer, *, tm, final_norm, pool_seq_len=None):
    m = h.shape[0]
    row = lambda w: pl.BlockSpec((tm, w), lambda i: (i, 0))
    (wmix, pscale, wa, wb, wc, wo, nf, w1, w2) = layer_wts
    wsg, bsg = sg_wts
    lw = lambda w: _resident(w.shape[1:], layer)
    if pool_seq_len is None:
        pool_specs, pool_args, tiles_per_seq = [row(POOL_WIDTH)], [pool_in], None
    else:
        ratio = tm // POOL_PREV_ROWS
        pool_specs = [row(POOL_WIDTH),
                      pl.BlockSpec((POOL_PREV_ROWS, POOL_WIDTH), lambda i: (jnp.maximum(i * ratio - 1, 0), 0))]
        pool_args, tiles_per_seq = [pool_in, pool_in], pool_seq_len // tm
    return pl.pallas_call(
        functools.partial(_merge_ffn_body, final_norm=final_norm, pool_tiles_per_seq=tiles_per_seq),
        grid=(m // tm,),
        in_specs=[row(D_MODEL), row(GLA_VAL_DIM), row(SG_WIDTH), row(SG_WIDTH), row(N_BRANCHES * D_MODEL)]
        + pool_specs
        + [lw(wmix), lw(pscale), lw(wsg), lw(bsg), lw(wa), lw(wb), lw(wc), lw(wo), lw(nf), lw(w1), lw(w2),
           _resident(nfin.shape)],
        out_specs=row(D_MODEL),
        out_shape=jax.ShapeDtypeStruct((m, D_MODEL), F32),
        compiler_params=pltpu.CompilerParams(
            dimension_semantics=("arbitrary",), vmem_limit_bytes=VMEM_LIMIT_BYTES),
        name="merge_ffn",
    )(h, og, u, vv, gates, *pool_args, wmix, pscale, wsg, bsg, wa, wb, wc, wo, nf, w1, w2, nfin)


def _reorder_w_in_body(wt_ref, o_ref):
    glr0 = 2 * GLA_KEY_DIM + 2 * GLA_VAL_DIM
    glr1 = glr0 + GLA_GATE_RANK
    tk = wt_ref.shape[1]
    o_ref[:, :glr0] = wt_ref[:glr0, :].T.astype(BF16)
    o_ref[:, glr0:C_GLR[0]] = wt_ref[glr1:, :].T.astype(BF16)
    glr = jnp.concatenate([wt_ref[glr0:glr1, :], jnp.zeros((LANES - GLA_GATE_RANK, tk), F32)], axis=0)
    o_ref[:, C_GLR[0]:] = glr.T.astype(BF16)


def _reorder_w_in(w):
    depth, rows, cols = w.shape
    tk = 256
    return pl.pallas_call(
        _reorder_w_in_body,
        grid=(depth, rows // tk),
        in_specs=[pl.BlockSpec((None, cols, tk), lambda l, i: (l, 0, i))],
        out_specs=pl.BlockSpec((None, tk, IN_COLS_R), lambda l, i: (l, i, 0)),
        out_shape=jax.ShapeDtypeStruct((depth, rows, IN_COLS_R), BF16),
        compiler_params=pltpu.CompilerParams(dimension_semantics=("arbitrary", "arbitrary")),
        name="reorder_w_in",
    )(jnp.swapaxes(w, 1, 2))


def _sg_operands(w_spatial, b_spatial, length):
    reps = SG_CHUNK // length
    t = jnp.arange(SG_CHUNK)
    w = jnp.tile(w_spatial[:, :, :length, :length], (1, 1, reps, reps))
    keep = ((t[:, None] // length) == (t[None, :] // length)) & ((t[None, :] % length) <= (t[:, None] % length))
    w = jnp.where(keep, w, 0.0).astype(BF16)
    bias = jnp.repeat(jnp.swapaxes(jnp.tile(b_spatial[:, :, :length], (1, 1, reps)), 1, 2), SG_GROUP_DIM, axis=2)
    return w, bias.astype(F32)


def kernel(x_prompt, x_sample, state_gla, state_pool, norm_mix, w_in, w_gk2, b_gk, gla_norm, w_pool_mix,
           pool_scale, w_spatial, b_spatial, w_br_a, w_br_b, w_br_c, w_out, norm_ffn, w_ff1, w_ff2,
           norm_final):
    depth = w_in.shape[0]
    n_p, t_p, _ = x_prompt.shape
    n_s, t_s, _ = x_sample.shape
    hp = x_prompt.reshape(n_p * t_p, D_MODEL)
    hs = x_sample.reshape(n_s * t_s, D_MODEL)
    n_prev_s = POOL_HIST
    rows = lambda a: a.reshape(depth, 1, -1).astype(F32)
    bf = lambda a: a.astype(BF16)

    w_r = _reorder_w_in(w_in)
    wgk = bf(jnp.pad(w_gk2, ((0, 0), (0, LANES - GLA_GATE_RANK), (0, 0))))
    bgk, gn, nmix = rows(b_gk), rows(gla_norm), rows(norm_mix)
    layer_wts = (bf(w_pool_mix), rows(pool_scale), bf(w_br_a), bf(w_br_b), bf(w_br_c), bf(w_out),
                 rows(norm_ffn), bf(w_ff1), bf(w_ff2))
    sg_p = _sg_operands(w_spatial, b_spatial, min(t_p, SG_CHUNK))
    sg_s = _sg_operands(w_spatial, b_spatial, min(t_s, SG_CHUNK))
    nfin = norm_final.reshape(1, -1).astype(F32)
    hist16 = jnp.pad(state_pool, ((0, 0), (0, 0), (1, 0), (0, 0)))

    gla_p, pool_p, pool_s, sgv_s = [], [], [], []
    gla_s = None
    for l in range(depth):
        final = l == depth - 1
        og, xp, u, vv, gates, s_fin = _inproj_gla(hp, nmix, w_r, wgk, bgk, gn, l, tm=TM_PROJ, n_seq=n_p,
                                                  seq_len=t_p)
        hp = _merge_ffn(hp, og, xp, u, vv, gates, layer_wts, sg_p, nfin, l, tm=TM_MERGE, final_norm=final,
                        pool_seq_len=t_p)
        gla_p.append(s_fin)
        pool_p.append(xp.reshape(n_p, t_p, POOL_WIDTH)[:, t_p - POOL_HIST:, :])

        qk, vg, xp, u, vv, gates, b2 = _inproj(hs, nmix, w_r, wgk, bgk, l, tm=TM_PROJ, seg=t_s, vv_dtype=F32)
        og, gla_s = _gla_sample(qk, vg, b2, state_gla, gn, l, gla_s, seq_len=t_s)
        pooled, new_hist = _pool_sample(xp.reshape(n_s, t_s, POOL_WIDTH), hist16[l], n_prev=n_prev_s, nb=16)
        hs = _merge_ffn(hs, og, pooled.reshape(n_s * t_s, POOL_WIDTH), u, vv, gates, layer_wts, sg_s, nfin, l,
                        tm=TM_MERGE, final_norm=final)
        pool_s.append(new_hist[:, 1:, :])
        sgv_s.append(vv.reshape(n_s, t_s, SG_WIDTH))

    return (hp.reshape(n_p, t_p, D_MODEL), hs.reshape(n_s, t_s, D_MODEL), jnp.stack(gla_p), gla_s,
            jnp.stack(pool_p), jnp.stack(pool_s), jnp.stack(sgv_s))
```

```python
import functools
import math

import jax
import jax.numpy as jnp
import numpy as np
from jax import lax
from jax.experimental import pallas as pl
from jax.experimental.pallas import tpu as pltpu

F32 = jnp.float32
BF16 = jnp.bfloat16

D_MODEL = 1024
GLA_HEADS = 4
GLA_KEY_DIM = 512
GLA_VAL_DIM = 1024
GLA_HEAD_K = 128
GLA_HEAD_V = 256
GLA_GATE_RANK = 16
GLA_GATE_NORMALIZER = 16.0
POOL_WIDTH = 512
POOL_WINDOWS = (2, 4, 8, 16)
POOL_GROUPS = 4
POOL_GROUP_DIM = 128
POOL_HIST = 15
SG_WIDTH = 512
SG_GROUPS = 4
SG_GROUP_DIM = 128
SG_CHUNK = 128
N_BRANCHES = 3
D_FF = 4096
EPS = 1e-6

SUBLANES = 8
LANES = 128
VMEM_LIMIT_BYTES = 56 * 1024 * 1024

C_QK = (0, 1024)
C_V = (1024, 2048)
C_GOUT = (2048, 3072)
C_XP = (3072, 3584)
C_U = (3584, 4096)
C_VV = (4096, 4608)
C_GATES = (4608, 7680)
C_GLR = (7680, 7808)
IN_COLS_R = 7808

TM_PROJ = 512
TM_MERGE = 512
PROJ_SLICE = 256
HEADS_IN_FLIGHT = 1
GLA_CHUNK = 128
GLA_BLOCK = SUBLANES
SAMPLE_SEQS_PER_STEP = 16
LOG2_DECAY_SCALE = math.log2(math.e) / GLA_GATE_NORMALIZER


def _resident(shape, layer=None):
    nd = len(shape)
    if layer is None:
        return pl.BlockSpec(shape, lambda *_: (0,) * nd, pipeline_mode=pl.Buffered(1))
    return pl.BlockSpec((None,) + tuple(shape), lambda *_: (layer,) + (0,) * nd,
                        pipeline_mode=pl.Buffered(1))


def _rms(x, w):
    return x * lax.rsqrt(jnp.mean(x * x, axis=-1, keepdims=True) + EPS) * w


def _dot(a, b):
    return jnp.dot(a, b, preferred_element_type=F32)


def _dot_nt(a, b):
    return lax.dot_general(a, b, (((1,), (1,)), ((), ())), preferred_element_type=F32)


def _row_iota(c, w):
    return lax.broadcasted_iota(jnp.int32, (c, w), 0)


def _bcast_row(x, blk, r):
    c, w = x.shape
    x3 = x.reshape(c // blk, blk, w)
    return jnp.broadcast_to(x3[:, r:r + 1, :], x3.shape).reshape(c, w)


def _seg_cumsum(g, seg):
    c, w = g.shape
    tpos = _row_iota(c, w) % SUBLANES
    for sh in (1, 2, 4):
        g = g + jnp.where(tpos >= sh, pltpu.roll(g, sh, 0), 0.0)
    if seg > SUBLANES:
        nb = seg // SUBLANES
        x = g.reshape(c // seg, nb, SUBLANES, w)
        tot = _bcast_row(g, SUBLANES, SUBLANES - 1).reshape(c // seg, nb, SUBLANES, w)
        outs, carry = [x[:, 0]], tot[:, 0]
        for i in range(1, nb):
            outs.append(x[:, i] + carry)
            carry = carry + tot[:, i]
        g = jnp.stack(outs, axis=1).reshape(c, w)
    return g


def _log2_decay(x):
    y = jnp.exp2(jnp.abs(x) * -math.log2(math.e))
    return jnp.minimum(x, 0.0) * LOG2_DECAY_SCALE - jnp.log2(1.0 + y) * (math.log(2.0) * LOG2_DECAY_SCALE)


def _inproj_body(h_ref, nw_ref, w_ref, wgk_ref, bgk_ref,
                 qk_ref, vg_ref, xp_ref, u_ref, vv_ref, gt_ref, b2_ref, *, seg):
    xn = _rms(h_ref[...], nw_ref[...]).astype(BF16)

    def proj(lo, hi):
        return _dot(xn, w_ref[:, lo:hi])

    g2 = _log2_decay(_dot(proj(*C_GLR).astype(BF16), wgk_ref[...]) + bgk_ref[...])
    b2_ref[...] = _seg_cumsum(g2, seg)
    vg_ref[:, GLA_VAL_DIM:] = jax.nn.silu(proj(*C_GOUT)).astype(BF16)
    u_ref[...] = jax.nn.gelu(proj(*C_U)).astype(u_ref.dtype)
    vv_ref[...] = jax.nn.gelu(proj(*C_VV)).astype(vv_ref.dtype)
    for j in range(N_BRANCHES):
        lo = C_GATES[0] + j * D_MODEL
        gt_ref[:, j * D_MODEL:(j + 1) * D_MODEL] = jax.nn.sigmoid(proj(lo, lo + D_MODEL)).astype(BF16)
    xp_ref[...] = proj(*C_XP)
    vg_ref[:, :GLA_VAL_DIM] = proj(*C_V).astype(BF16)
    qk_ref[...] = proj(*C_QK).astype(BF16)


def _inproj(h, norm_w, w_r, wgk, bgk, layer, *, tm, seg, vv_dtype):
    m = h.shape[0]
    row = lambda w: pl.BlockSpec((tm, w), lambda i: (i, 0))
    out_shape = (
        jax.ShapeDtypeStruct((m, 2 * GLA_KEY_DIM), BF16),
        jax.ShapeDtypeStruct((m, 2 * GLA_VAL_DIM), BF16),
        jax.ShapeDtypeStruct((m, POOL_WIDTH), F32),
        jax.ShapeDtypeStruct((m, SG_WIDTH), BF16),
        jax.ShapeDtypeStruct((m, SG_WIDTH), vv_dtype),
        jax.ShapeDtypeStruct((m, N_BRANCHES * D_MODEL), BF16),
        jax.ShapeDtypeStruct((m, GLA_KEY_DIM), F32),
    )
    return pl.pallas_call(
        functools.partial(_inproj_body, seg=seg),
        grid=(m // tm,),
        in_specs=[row(D_MODEL), _resident((1, D_MODEL), layer), _resident((D_MODEL, IN_COLS_R), layer),
                  _resident((LANES, GLA_KEY_DIM), layer), _resident((1, GLA_KEY_DIM), layer)],
        out_specs=[row(2 * GLA_KEY_DIM), row(2 * GLA_VAL_DIM), row(POOL_WIDTH), row(SG_WIDTH), row(SG_WIDTH),
                   row(N_BRANCHES * D_MODEL), row(GLA_KEY_DIM)],
        out_shape=out_shape,
        compiler_params=pltpu.CompilerParams(
            dimension_semantics=("arbitrary",), vmem_limit_bytes=VMEM_LIMIT_BYTES),
        name="inproj",
    )(h, norm_w, w_r, wgk, bgk)


CODE_BLOCK = 1
CODE_LEVEL = CODE_BLOCK + GLA_BLOCK


def _score_codes(c, levels):
    t = np.arange(c)[:, None]
    s = np.arange(c)[None, :]
    code = np.zeros((c, c), np.int32)
    code = np.where((t // GLA_BLOCK == s // GLA_BLOCK) & (s <= t), CODE_BLOCK + s % GLA_BLOCK, code)
    if levels:
        lvl, j = 2 * GLA_BLOCK, 0
        while lvl <= c:
            hit = (t // lvl == s // lvl) & (t // (lvl // 2) != s // (lvl // 2)) & (s < t)
            code = np.where(hit, CODE_LEVEL + j, code)
            lvl, j = lvl * 2, j + 1
    return jnp.asarray(code, jnp.int32)


def _level_signs(c):
    t = np.arange(c)[:, None]
    out = []
    lvl = 2 * GLA_BLOCK
    while lvl <= c:
        out.append(np.broadcast_to(np.where(t % lvl >= lvl // 2, 1.0, -1.0), (c, GLA_HEAD_K)))
        lvl *= 2
    return jnp.asarray(np.stack(out), F32)


def _rows_bcast(ref, rows, lo, reps):
    parts = []
    for r in rows:
        tile = jnp.broadcast_to(ref[pl.ds(r, 1), lo:lo + GLA_HEAD_K], (SUBLANES, GLA_HEAD_K))
        parts.extend([tile] * reps)
    return jnp.concatenate(parts, axis=0)


def _run(steps):
    try:
        while True:
            next(steps)
    except StopIteration as stop:
        return stop.value


def _gla_scores_steps(q_ref, k_ref, b_ref, code_ref, r0, ks, c, sgn_ref=None):
    tile = lambda ref: ref[r0:r0 + c, ks:ks + GLA_HEAD_K]
    att = jnp.zeros((c, c), F32)
    if sgn_ref is not None:
        lvl, j = 2 * GLA_BLOCK, 0
        while lvl <= c:
            half = lvl // 2
            bmid = _rows_bcast(b_ref, [r0 + i * lvl + half - 1 for i in range(c // lvl)], ks, lvl // SUBLANES)
            e = jnp.exp2((tile(b_ref) - bmid) * sgn_ref[j])
            firsts = [slice(i * lvl, i * lvl + half) for i in range(c // lvl)]
            seconds = [slice(i * lvl + half, (i + 1) * lvl) for i in range(c // lvl)]
            q, k = tile(q_ref), tile(k_ref)
            ql = jnp.concatenate([q[rs] * e[rs] for rs in seconds], axis=0).astype(BF16)
            zero = jnp.zeros((half, GLA_HEAD_K), F32)
            kl = jnp.concatenate([x for rs in firsts for x in (k[rs] * e[rs], zero)], axis=0).astype(BF16)
            att_l = _dot_nt(ql, kl)
            pieces = []
            for i, (fs, ss) in enumerate(zip(firsts, seconds)):
                hit = code_ref[ss, :] == CODE_LEVEL + j
                pieces += [att[fs], jnp.where(hit, att_l[i * half:(i + 1) * half], att[ss])]
            att = jnp.concatenate(pieces, axis=0)
            lvl, j = lvl * 2, j + 1
            yield
    nblk = c // GLA_BLOCK
    for r in range(GLA_BLOCK):
        rows = [r0 + i * GLA_BLOCK + r for i in range(nblk)]
        br = _rows_bcast(b_ref, rows, ks, 1)
        kr = _rows_bcast(k_ref, rows, ks, 1)
        a = jnp.sum(tile(q_ref) * kr * jnp.exp2(tile(b_ref) - br), axis=-1, keepdims=True)
        att = jnp.where(code_ref[...] == CODE_BLOCK + r, a, att)
        yield
    return att


def _inproj_gla_body(h_ref, nw_ref, w_ref, wgk_ref, bgk_ref, code_ref, sgn_ref, gn_ref,
                     og_ref, xp_ref, u_ref, vv_ref, gt_ref, sfin_ref,
                     s_scr, q_scr, k_scr, v_scr, g_scr, b_scr, *, tiles_per_seq):
    i = pl.program_id(0)
    tm = h_ref.shape[0]
    c = GLA_CHUNK

    @pl.when(i % tiles_per_seq == 0)
    def _():
        s_scr[...] = jnp.zeros_like(s_scr)

    xn = _rms(h_ref[...], nw_ref[...]).astype(BF16)

    def proj(lo, hi):
        return _dot(xn, w_ref[:, lo:hi])

    def slice_task(lo, dst_ref, dst_lo, act):
        def run():
            dst_ref[:, dst_lo:dst_lo + PROJ_SLICE] = act(proj(lo, lo + PROJ_SLICE)).astype(dst_ref.dtype)
        return run

    def slice_tasks(cols, dst_ref, act=lambda x: x):
        return [slice_task(lo, dst_ref, lo - cols[0], act) for lo in range(cols[0], cols[1], PROJ_SLICE)]

    q_scale = GLA_HEAD_K ** -0.5
    head_tasks = (slice_tasks(C_GOUT, g_scr, jax.nn.silu) + slice_tasks(C_V, v_scr)
                  + slice_tasks((C_QK[0], C_QK[0] + GLA_KEY_DIM), q_scr, lambda x: x * q_scale)
                  + slice_tasks((C_QK[0] + GLA_KEY_DIM, C_QK[1]), k_scr))
    tails = (slice_tasks(C_U, u_ref, jax.nn.gelu) + slice_tasks(C_VV, vv_ref, jax.nn.gelu)
             + slice_tasks(C_GATES, gt_ref, jax.nn.sigmoid) + slice_tasks(C_XP, xp_ref))

    glr = proj(*C_GLR).astype(BF16)
    g2 = _log2_decay(_dot(glr, wgk_ref[...]) + bgk_ref[...])
    per_head = len(head_tasks) // GLA_HEADS
    for h in range(GLA_HEADS):
        ks = h * GLA_HEAD_K
        b_scr[:, ks:ks + GLA_HEAD_K] = _seg_cumsum(g2[:, ks:ks + GLA_HEAD_K], c)
        for t in head_tasks[h * per_head:(h + 1) * per_head]:
            t()

    def pump(n=1):
        for _ in range(min(n, len(tails))):
            tails.pop(0)()

    def head_steps(r0, h):
        ks, kv = h * GLA_HEAD_K, h * GLA_HEAD_V
        tile = lambda ref: ref[r0:r0 + c, ks:ks + GLA_HEAD_K]
        v_bf = lambda: v_scr[r0:r0 + c, kv:kv + GLA_HEAD_V]
        s = s_scr[h]
        o = _dot((tile(q_scr) * jnp.exp2(tile(b_scr))).astype(BF16), s.astype(BF16))
        b_t = tile(b_scr).T
        b_last = b_t[:, c - 1:c]
        kd_t = (tile(k_scr).T * jnp.exp2(b_last - b_t)).astype(BF16)
        s_scr[h] = jnp.exp2(b_last) * s + _dot(kd_t, v_bf())
        yield
        att = yield from _gla_scores_steps(q_scr, k_scr, b_scr, code_ref, r0, ks, c, sgn_ref)
        o = o + _dot(att.astype(BF16), v_bf())
        yield
        gate = g_scr[r0:r0 + c, kv:kv + GLA_HEAD_V].astype(F32)
        og_ref[r0:r0 + c, kv:kv + GLA_HEAD_V] = (_rms(o, gn_ref[...]) * gate).astype(BF16)

    for ch, h0 in [(ch, h0) for ch in range(tm // c) for h0 in range(0, GLA_HEADS, HEADS_IN_FLIGHT)]:
        active = [head_steps(ch * c, h) for h in range(h0, h0 + HEADS_IN_FLIGHT)]
        while active:
            for steps in list(active):
                try:
                    next(steps)
                except StopIteration:
                    active.remove(steps)
            pump()
    pump(len(tails))

    @pl.when(i % tiles_per_seq == tiles_per_seq - 1)
    def _():
        sfin_ref[0] = s_scr[...]


def _inproj_gla(h, norm_w, w_r, wgk, bgk, gn, layer, *, tm, n_seq, seq_len):
    m = h.shape[0]
    c = GLA_CHUNK
    tiles_per_seq = seq_len // tm
    row = lambda w: pl.BlockSpec((tm, w), lambda i: (i, 0))
    out_shape = (
        jax.ShapeDtypeStruct((m, GLA_VAL_DIM), BF16),
        jax.ShapeDtypeStruct((m, POOL_WIDTH), F32),
        jax.ShapeDtypeStruct((m, SG_WIDTH), BF16),
        jax.ShapeDtypeStruct((m, SG_WIDTH), BF16),
        jax.ShapeDtypeStruct((m, N_BRANCHES * D_MODEL), BF16),
        jax.ShapeDtypeStruct((n_seq, GLA_HEADS, GLA_HEAD_K, GLA_HEAD_V), F32),
    )
    code, sgn = _score_codes(c, True), _level_signs(c)
    return pl.pallas_call(
        functools.partial(_inproj_gla_body, tiles_per_seq=tiles_per_seq),
        grid=(m // tm,),
        in_specs=[row(D_MODEL), _resident((1, D_MODEL), layer), _resident((D_MODEL, IN_COLS_R), layer),
                  _resident((LANES, GLA_KEY_DIM), layer), _resident((1, GLA_KEY_DIM), layer),
                  _resident(code.shape), _resident(sgn.shape), _resident((1, GLA_HEAD_V), layer)],
        out_specs=[row(GLA_VAL_DIM), row(POOL_WIDTH), row(SG_WIDTH), row(SG_WIDTH), row(N_BRANCHES * D_MODEL),
                   pl.BlockSpec((1, GLA_HEADS, GLA_HEAD_K, GLA_HEAD_V), lambda i: (i // tiles_per_seq, 0, 0, 0))],
        out_shape=out_shape,
        scratch_shapes=[pltpu.VMEM((GLA_HEADS, GLA_HEAD_K, GLA_HEAD_V), F32),
                        pltpu.VMEM((tm, GLA_KEY_DIM), F32), pltpu.VMEM((tm, GLA_KEY_DIM), F32),
                        pltpu.VMEM((tm, GLA_VAL_DIM), BF16), pltpu.VMEM((tm, GLA_VAL_DIM), BF16),
                        pltpu.VMEM((tm, GLA_KEY_DIM), F32)],
        compiler_params=pltpu.CompilerParams(
            dimension_semantics=("arbitrary",), vmem_limit_bytes=VMEM_LIMIT_BYTES),
        name="inproj_gla",
    )(h, norm_w, w_r, wgk, bgk, code, sgn, gn)


def _gla_sample_body(qk_ref, vg_ref, b_ref, s0_ref, code_ref, gn_ref, *rest, seq_len, n_blk):
    og_ref, snew_ref, q_scr, k_scr = rest[-4:]
    nb = SAMPLE_SEQS_PER_STEP
    c = nb * seq_len

    @pl.when(pl.program_id(0) >= n_blk)
    def _():
        snew_ref[...] = jnp.zeros_like(snew_ref)

    @pl.when(pl.program_id(0) < n_blk)
    def _():
        q_scr[...] = qk_ref[:, :GLA_KEY_DIM].astype(F32) * (GLA_HEAD_K ** -0.5)
        k_scr[...] = qk_ref[:, GLA_KEY_DIM:].astype(F32)
        gn = gn_ref[...]
        lane_seq = lax.broadcasted_iota(jnp.int32, (GLA_HEAD_K, c), 1) // seq_len
        last_rows = [(n + 1) * seq_len - 1 for n in range(nb)]
        for h in range(GLA_HEADS):
            ks, kv = h * GLA_HEAD_K, h * GLA_HEAD_V
            k = k_scr[:, ks:ks + GLA_HEAD_K]
            b = b_ref[:, ks:ks + GLA_HEAD_K]
            v_bf = vg_ref[:, kv:kv + GLA_HEAD_V]
            gate = vg_ref[:, GLA_VAL_DIM + kv:GLA_VAL_DIM + kv + GLA_HEAD_V].astype(F32)
            qi = q_scr[:, ks:ks + GLA_HEAD_K] * jnp.exp2(b)
            inter = [
                _dot(qi[n * seq_len:(n + 1) * seq_len, :].astype(BF16), s0_ref[n, h].astype(BF16))
                for n in range(nb)
            ]
            att = _run(_gla_scores_steps(q_scr, k_scr, b_ref, code_ref, 0, ks, c))
            o = jnp.concatenate(inter, axis=0) + _dot(att.astype(BF16), v_bf)
            og_ref[:, kv:kv + GLA_HEAD_V] = (_rms(o, gn) * gate).astype(BF16)

            b_last_rows = _rows_bcast(b_ref, last_rows, ks, 1)
            kd_t = (k * jnp.exp2(b_last_rows - b)).T
            b_t = b.T
            for n in range(nb):
                col = last_rows[n]
                decay = jnp.exp2(b_t[:, col:col + 1])
                kd_n = jnp.where(lane_seq == n, kd_t, 0.0).astype(BF16)
                snew_ref[n, h] = decay * s0_ref[n, h] + _dot(kd_n, v_bf)


def _gla_sample(qk, vg, b2, state_all, gn, layer, prev_new_state, *, seq_len):
    assert seq_len == GLA_BLOCK
    nb = SAMPLE_SEQS_PER_STEP
    depth, n_seq = state_all.shape[:2]
    n_blk = n_seq // nb
    c = nb * seq_len
    first = prev_new_state is None
    assert first == (layer == 0)
    blk = (lambda i: jnp.minimum(i, n_blk - 1)) if first else (lambda i: i)
    row = lambda w: pl.BlockSpec((c, w), lambda i: (blk(i), 0))
    st_shape = (None, nb, GLA_HEADS, GLA_HEAD_K, GLA_HEAD_V)
    st_in = pl.BlockSpec(st_shape, lambda i: (layer, blk(i), 0, 0, 0))
    st_out = pl.BlockSpec(st_shape, lambda i: (layer + i // n_blk, i % n_blk, 0, 0, 0))
    in_specs = [row(2 * GLA_KEY_DIM), row(2 * GLA_VAL_DIM), row(GLA_KEY_DIM), st_in, _resident((c, c)),
                _resident((1, GLA_HEAD_V), layer)]
    args = [qk, vg, b2, state_all, _score_codes(c, False), gn]
    aliases = {}
    if not first:
        in_specs.append(pl.BlockSpec(memory_space=pl.ANY))
        args.append(prev_new_state)
        aliases = {len(args) - 1: 1}
    return pl.pallas_call(
        functools.partial(_gla_sample_body, seq_len=seq_len, n_blk=n_blk),
        grid=(n_blk * depth if first else n_blk,),
        in_specs=in_specs,
        out_specs=[row(GLA_VAL_DIM), st_out],
        out_shape=(jax.ShapeDtypeStruct((n_seq * seq_len, GLA_VAL_DIM), BF16),
                   jax.ShapeDtypeStruct(state_all.shape, F32)),
        scratch_shapes=[pltpu.VMEM((c, GLA_KEY_DIM), F32), pltpu.VMEM((c, GLA_KEY_DIM), F32)],
        input_output_aliases=aliases,
        compiler_params=pltpu.CompilerParams(
            dimension_semantics=("arbitrary",), vmem_limit_bytes=VMEM_LIMIT_BYTES),
        name="gla_sample",
    )(*args)


def _window_sums(full):
    outs = []
    s = full
    w = 1
    for gi, win in enumerate(POOL_WINDOWS):
        while w < win:
            s = s + pltpu.roll(s, w, 0)
            w *= 2
        outs.append(s[:, gi * POOL_GROUP_DIM:(gi + 1) * POOL_GROUP_DIM])
    return jnp.concatenate(outs, axis=1)


def _pool_counts(pos, n_prev):
    grp = lax.broadcasted_iota(jnp.int32, pos.shape, pos.ndim - 1) // POOL_GROUP_DIM
    win = jnp.zeros(pos.shape, jnp.int32)
    for gi, w in enumerate(POOL_WINDOWS):
        win = jnp.where(grp == gi, w, win)
    return jnp.minimum(win, pos + 1 + n_prev).astype(F32)


POOL_PREV_ROWS = 2 * SUBLANES


def _pool_prompt_tile(x_ref, prev_ref, tile_in_seq):
    tp = x_ref.shape[0]
    x = x_ref[...]
    prev = jnp.where(tile_in_seq == 0, 0.0, prev_ref[...])
    sums = _window_sums(jnp.concatenate([prev, x], axis=0))[POOL_PREV_ROWS:, :]
    pos = _row_iota(tp, POOL_WIDTH) + tile_in_seq * tp
    return (sums / _pool_counts(pos, 0) - x).astype(BF16)


def _pool_sample_body(x_ref, hist_ref, o_ref, nh_ref, *, n_prev):
    nb, t, _ = x_ref.shape
    hr = hist_ref.shape[1]
    x = x_ref[...]
    full = jnp.concatenate([hist_ref[...], x], axis=1)
    sums = _window_sums(full.reshape(nb * (hr + t), POOL_WIDTH)).reshape(nb, hr + t, POOL_WIDTH)
    pos = lax.broadcasted_iota(jnp.int32, (nb, t, POOL_WIDTH), 1)
    o_ref[...] = (sums[:, hr:, :] / _pool_counts(pos, n_prev) - x).astype(BF16)
    nh_ref[...] = full[:, t:, :]


def _pool_sample(xp3, hist16, *, n_prev, nb):
    n, t, _ = xp3.shape
    hr = hist16.shape[1]
    blk = lambda r: pl.BlockSpec((nb, r, POOL_WIDTH), lambda i: (i, 0, 0))
    return pl.pallas_call(
        functools.partial(_pool_sample_body, n_prev=n_prev),
        grid=(n // nb,),
        in_specs=[blk(t), blk(hr)],
        out_specs=[blk(t), blk(hr)],
        out_shape=(jax.ShapeDtypeStruct((n, t, POOL_WIDTH), BF16),
                   jax.ShapeDtypeStruct((n, hr, POOL_WIDTH), F32)),
        compiler_params=pltpu.CompilerParams(dimension_semantics=("arbitrary",)),
        name="pool_sample",
    )(xp3, hist16)


def _merge_ffn_body(h_ref, og_ref, u_ref, vv_ref, gt_ref, *rest, final_norm, pool_tiles_per_seq):
    if pool_tiles_per_seq is None:
        pl_ref, rest = rest[0], rest[1:]
        pooled = lambda g: pl_ref[:, g * POOL_GROUP_DIM:(g + 1) * POOL_GROUP_DIM]
    else:
        xp_ref, prev_ref, rest = rest[0], rest[1], rest[2:]
        pooled_all = _pool_prompt_tile(xp_ref, prev_ref, pl.program_id(0) % pool_tiles_per_seq)
        pooled = lambda g: pooled_all[:, g * POOL_GROUP_DIM:(g + 1) * POOL_GROUP_DIM]
    (wmix_ref, pscale_ref, wsg_ref, bsg_ref, wa_ref, wb_ref, wc_ref, wo_ref,
     nf_ref, w1_ref, w2_ref, nfin_ref, o_ref) = rest
    tm = h_ref.shape[0]
    br = gt_ref[:, 0:D_MODEL].astype(F32) * _dot(og_ref[...], wa_ref[...])
    mixed = jnp.concatenate([_dot(pooled(g), wmix_ref[g]) for g in range(POOL_GROUPS)], axis=1)
    pool_out = (mixed * pscale_ref[...]).astype(BF16)
    br = br + gt_ref[:, D_MODEL:2 * D_MODEL].astype(F32) * _dot(pool_out, wb_ref[...])
    rows = []
    for c in range(tm // SG_CHUNK):
        r0 = c * SG_CHUNK
        vv = vv_ref[r0:r0 + SG_CHUNK, :].astype(BF16)
        sg = jnp.concatenate(
            [_dot(wsg_ref[g], vv[:, g * SG_GROUP_DIM:(g + 1) * SG_GROUP_DIM]) for g in range(SG_GROUPS)],
            axis=1) + bsg_ref[...]
        rows.append((u_ref[r0:r0 + SG_CHUNK, :].astype(F32) * sg).astype(BF16))
    sg_out = jnp.concatenate(rows, axis=0)
    br = br + gt_ref[:, 2 * D_MODEL:3 * D_MODEL].astype(F32) * _dot(sg_out, wc_ref[...])
    h = h_ref[...] + _dot(br.astype(BF16), wo_ref[...])
    a = jnp.maximum(_dot(_rms(h, nf_ref[...]).astype(BF16), w1_ref[...]), 0.0)
    h = h + _dot((a * a).astype(BF16), w2_ref[...])
    if final_norm:
        h = _rms(h, nfin_ref[...])
    o_ref[...] = h


def _merge_ffn(h, og, pool_in, u, vv, gates, layer_wts, sg_wts, nfin, layer, *, tm, final_norm, pool_seq_len=None):
    m = h.shape[0]
    row = lambda w: pl.BlockSpec((tm, w), lambda i: (i, 0))
    (wmix, pscale, wa, wb, wc, wo, nf, w1, w2) = layer_wts
    wsg, bsg = sg_wts
    lw = lambda w: _resident(w.shape[1:], layer)
    if pool_seq_len is None:
        pool_specs, pool_args, tiles_per_seq = [row(POOL_WIDTH)], [pool_in], None
    else:
        ratio = tm // POOL_PREV_ROWS
        pool_specs = [row(POOL_WIDTH),
                      pl.BlockSpec((POOL_PREV_ROWS, POOL_WIDTH), lambda i: (jnp.maximum(i * ratio - 1, 0), 0))]
        pool_args, tiles_per_seq = [pool_in, pool_in], pool_seq_len // tm
    return pl.pallas_call(
        functools.partial(_merge_ffn_body, final_norm=final_norm, pool_tiles_per_seq=tiles_per_seq),
        grid=(m // tm,),
        in_specs=[row(D_MODEL), row(GLA_VAL_DIM), row(SG_WIDTH), row(SG_WIDTH), row(N_BRANCHES * D_MODEL)]
        + pool_specs
        + [lw(wmix), lw(pscale), lw(wsg), lw(bsg), lw(wa), lw(wb), lw(wc), lw(wo), lw(nf), lw(w1), lw(w2),
           _resident(nfin.shape)],
        out_specs=row(D_MODEL),
        out_shape=jax.ShapeDtypeStruct((m, D_MODEL), F32),
        compiler_params=pltpu.CompilerParams(
            dimension_semantics=("arbitrary",), vmem_limit_bytes=VMEM_LIMIT_BYTES),
        name="merge_ffn",
    )(h, og, u, vv, gates, *pool_args, wmix, pscale, wsg, bsg, wa, wb, wc, wo, nf, w1, w2, nfin)


def _reorder_w_in_body(wt_ref, o_ref):
    glr0 = 2 * GLA_KEY_DIM + 2 * GLA_VAL_DIM
    glr1 = glr0 + GLA_GATE_RANK
    tk = wt_ref.shape[1]
    o_ref[:, :glr0] = wt_ref[:glr0, :].T.astype(BF16)
    o_ref[:, glr0:C_GLR[0]] = wt_ref[glr1:, :].T.astype(BF16)
    glr = jnp.concatenate([wt_ref[glr0:glr1, :], jnp.zeros((LANES - GLA_GATE_RANK, tk), F32)], axis=0)
    o_ref[:, C_GLR[0]:] = glr.T.astype(BF16)


def _reorder_w_in(w):
    depth, rows, cols = w.shape
    tk = 256
    return pl.pallas_call(
        _reorder_w_in_body,
        grid=(depth, rows // tk),
        in_specs=[pl.BlockSpec((None, cols, tk), lambda l, i: (l, 0, i))],
        out_specs=pl.BlockSpec((None, tk, IN_COLS_R), lambda l, i: (l, i, 0)),
        out_shape=jax.ShapeDtypeStruct((depth, rows, IN_COLS_R), BF16),
        compiler_params=pltpu.CompilerParams(dimension_semantics=("arbitrary", "arbitrary")),
        name="reorder_w_in",
    )(jnp.swapaxes(w, 1, 2))


def _sg_operands(w_spatial, b_spatial, length):
    reps = SG_CHUNK // length
    t = jnp.arange(SG_CHUNK)
    w = jnp.tile(w_spatial[:, :, :length, :length], (1, 1, reps, reps))
    keep = ((t[:, None] // length) == (t[None, :] // length)) & ((t[None, :] % length) <= (t[:, None] % length))
    w = jnp.where(keep, w, 0.0).astype(BF16)
    bias = jnp.repeat(jnp.swapaxes(jnp.tile(b_spatial[:, :, :length], (1, 1, reps)), 1, 2), SG_GROUP_DIM, axis=2)
    return w, bias.astype(F32)


def kernel(x_prompt, x_sample, state_gla, state_pool, norm_mix, w_in, w_gk2, b_gk, gla_norm, w_pool_mix,
           pool_scale, w_spatial, b_spatial, w_br_a, w_br_b, w_br_c, w_out, norm_ffn, w_ff1, w_ff2,
           norm_final):
    depth = w_in.shape[0]
    n_p, t_p, _ = x_prompt.shape
    n_s, t_s, _ = x_sample.shape
    hp = x_prompt.reshape(n_p * t_p, D_MODEL)
    hs = x_sample.reshape(n_s * t_s, D_MODEL)
    n_prev_s = POOL_HIST
    rows = lambda a: a.reshape(depth, 1, -1).astype(F32)
    bf = lambda a: a.astype(BF16)

    w_r = _reorder_w_in(w_in)
    wgk = bf(jnp.pad(w_gk2, ((0, 0), (0, LANES - GLA_GATE_RANK), (0, 0))))
    bgk, gn, nmix = rows(b_gk), rows(gla_norm), rows(norm_mix)
    layer_wts = (bf(w_pool_mix), rows(pool_scale), bf(w_br_a), bf(w_br_b), bf(w_br_c), bf(w_out),
                 rows(norm_ffn), bf(w_ff1), bf(w_ff2))
    sg_p = _sg_operands(w_spatial, b_spatial, min(t_p, SG_CHUNK))
    sg_s = _sg_operands(w_spatial, b_spatial, min(t_s, SG_CHUNK))
    nfin = norm_final.reshape(1, -1).astype(F32)
    hist16 = jnp.pad(state_pool, ((0, 0), (0, 0), (1, 0), (0, 0)))

    gla_p, pool_p, pool_s, sgv_s = [], [], [], []
    gla_s = None
    for l in range(depth):
        final = l == depth - 1
        og, xp, u, vv, gates, s_fin = _inproj_gla(hp, nmix, w_r, wgk, bgk, gn, l, tm=TM_PROJ, n_seq=n_p,
                                                  seq_len=t_p)
        hp = _merge_ffn(hp, og, xp, u, vv, gates, layer_wts, sg_p, nfin, l, tm=TM_MERGE, final_norm=final,
                        pool_seq_len=t_p)
        gla_p.append(s_fin)
        pool_p.append(xp.reshape(n_p, t_p, POOL_WIDTH)[:, t_p - POOL_HIST:, :])

        qk, vg, xp, u, vv, gates, b2 = _inproj(hs, nmix, w_r, wgk, bgk, l, tm=TM_PROJ, seg=t_s, vv_dtype=F32)
        og, gla_s = _gla_sample(qk, vg, b2, state_gla, gn, l, gla_s, seq_len=t_s)
        pooled, new_hist = _pool_sample(xp.reshape(n_s, t_s, POOL_WIDTH), hist16[l], n_prev=n_prev_s, nb=16)
        hs = _merge_ffn(hs, og, pooled.reshape(n_s * t_s, POOL_WIDTH), u, vv, gates, layer_wts, sg_s, nfin, l,
                        tm=TM_MERGE, final_norm=final)
        pool_s.append(new_hist[:, 1:, :])
        sgv_s.append(vv.reshape(n_s, t_s, SG_WIDTH))

    return (hp.reshape(n_p, t_p, D_MODEL), hs.reshape(n_s, t_s, D_MODEL), jnp.stack(gla_p), gla_s,
            jnp.stack(pool_p), jnp.stack(pool_s), jnp.stack(sgv_s))
```

```python
import functools
import math

import jax
import jax.numpy as jnp
import numpy as np
from jax import lax
from jax.experimental import pallas as pl
from jax.experimental.pallas import tpu as pltpu

F32 = jnp.float32
BF16 = jnp.bfloat16

D_MODEL = 1024
GLA_HEADS = 4
GLA_KEY_DIM = 512
GLA_VAL_DIM = 1024
GLA_HEAD_K = 128
GLA_HEAD_V = 256
GLA_GATE_RANK = 16
GLA_GATE_NORMALIZER = 16.0
POOL_WIDTH = 512
POOL_WINDOWS = (2, 4, 8, 16)
POOL_GROUPS = 4
POOL_GROUP_DIM = 128
POOL_HIST = 15
SG_WIDTH = 512
SG_GROUPS = 4
SG_GROUP_DIM = 128
SG_CHUNK = 128
N_BRANCHES = 3
D_FF = 4096
EPS = 1e-6

SUBLANES = 8
LANES = 128
VMEM_LIMIT_BYTES = 56 * 1024 * 1024

C_QK = (0, 1024)
C_V = (1024, 2048)
C_GOUT = (2048, 3072)
C_XP = (3072, 3584)
C_U = (3584, 4096)
C_VV = (4096, 4608)
C_GATES = (4608, 7680)
C_GLR = (7680, 7808)
IN_COLS_R = 7808

TM_PROJ = 512
TM_MERGE = 512
PROJ_SLICE = 256
GLA_CHUNK = 128
GLA_BLOCK = SUBLANES
SAMPLE_SEQS_PER_STEP = 16
LOG2_DECAY_SCALE = math.log2(math.e) / GLA_GATE_NORMALIZER


def _resident(shape, layer=None):
    nd = len(shape)
    if layer is None:
        return pl.BlockSpec(shape, lambda *_: (0,) * nd, pipeline_mode=pl.Buffered(1))
    return pl.BlockSpec((None,) + tuple(shape), lambda *_: (layer,) + (0,) * nd,
                        pipeline_mode=pl.Buffered(1))


def _rms(x, w):
    return x * lax.rsqrt(jnp.mean(x * x, axis=-1, keepdims=True) + EPS) * w


def _dot(a, b):
    return jnp.dot(a, b, preferred_element_type=F32)


def _dot_nt(a, b):
    return lax.dot_general(a, b, (((1,), (1,)), ((), ())), preferred_element_type=F32)


def _row_iota(c, w):
    return lax.broadcasted_iota(jnp.int32, (c, w), 0)


def _bcast_row(x, blk, r):
    c, w = x.shape
    x3 = x.reshape(c // blk, blk, w)
    return jnp.broadcast_to(x3[:, r:r + 1, :], x3.shape).reshape(c, w)


def _seg_cumsum(g, seg):
    c, w = g.shape
    tpos = _row_iota(c, w) % SUBLANES
    for sh in (1, 2, 4):
        g = g + jnp.where(tpos >= sh, pltpu.roll(g, sh, 0), 0.0)
    if seg > SUBLANES:
        nb = seg // SUBLANES
        x = g.reshape(c // seg, nb, SUBLANES, w)
        tot = _bcast_row(g, SUBLANES, SUBLANES - 1).reshape(c // seg, nb, SUBLANES, w)
        outs, carry = [x[:, 0]], tot[:, 0]
        for i in range(1, nb):
            outs.append(x[:, i] + carry)
            carry = carry + tot[:, i]
        g = jnp.stack(outs, axis=1).reshape(c, w)
    return g


def _log2_decay(x):
    y = jnp.exp2(jnp.abs(x) * -math.log2(math.e))
    return jnp.minimum(x, 0.0) * LOG2_DECAY_SCALE - jnp.log2(1.0 + y) * (math.log(2.0) * LOG2_DECAY_SCALE)


def _inproj_body(h_ref, nw_ref, w_ref, wgk_ref, bgk_ref,
                 qk_ref, vg_ref, xp_ref, u_ref, vv_ref, gt_ref, b2_ref, *, seg):
    xn = _rms(h_ref[...], nw_ref[...]).astype(BF16)

    def proj(lo, hi):
        return _dot(xn, w_ref[:, lo:hi])

    g2 = _log2_decay(_dot(proj(*C_GLR).astype(BF16), wgk_ref[...]) + bgk_ref[...])
    b2_ref[...] = _seg_cumsum(g2, seg)
    vg_ref[:, GLA_VAL_DIM:] = jax.nn.silu(proj(*C_GOUT)).astype(BF16)
    u_ref[...] = jax.nn.gelu(proj(*C_U)).astype(u_ref.dtype)
    vv_ref[...] = jax.nn.gelu(proj(*C_VV)).astype(vv_ref.dtype)
    for j in range(N_BRANCHES):
        lo = C_GATES[0] + j * D_MODEL
        gt_ref[:, j * D_MODEL:(j + 1) * D_MODEL] = jax.nn.sigmoid(proj(lo, lo + D_MODEL)).astype(BF16)
    xp_ref[...] = proj(*C_XP)
    vg_ref[:, :GLA_VAL_DIM] = proj(*C_V).astype(BF16)
    qk_ref[...] = proj(*C_QK).astype(BF16)


def _inproj(h, norm_w, w_r, wgk, bgk, layer, *, tm, seg, vv_dtype):
    m = h.shape[0]
    row = lambda w: pl.BlockSpec((tm, w), lambda i: (i, 0))
    out_shape = (
        jax.ShapeDtypeStruct((m, 2 * GLA_KEY_DIM), BF16),
        jax.ShapeDtypeStruct((m, 2 * GLA_VAL_DIM), BF16),
        jax.ShapeDtypeStruct((m, POOL_WIDTH), F32),
        jax.ShapeDtypeStruct((m, SG_WIDTH), BF16),
        jax.ShapeDtypeStruct((m, SG_WIDTH), vv_dtype),
        jax.ShapeDtypeStruct((m, N_BRANCHES * D_MODEL), BF16),
        jax.ShapeDtypeStruct((m, GLA_KEY_DIM), F32),
    )
    return pl.pallas_call(
        functools.partial(_inproj_body, seg=seg),
        grid=(m // tm,),
        in_specs=[row(D_MODEL), _resident((1, D_MODEL), layer), _resident((D_MODEL, IN_COLS_R), layer),
                  _resident((LANES, GLA_KEY_DIM), layer), _resident((1, GLA_KEY_DIM), layer)],
        out_specs=[row(2 * GLA_KEY_DIM), row(2 * GLA_VAL_DIM), row(POOL_WIDTH), row(SG_WIDTH), row(SG_WIDTH),
                   row(N_BRANCHES * D_MODEL), row(GLA_KEY_DIM)],
        out_shape=out_shape,
        compiler_params=pltpu.CompilerParams(
            dimension_semantics=("arbitrary",), vmem_limit_bytes=VMEM_LIMIT_BYTES),
        name="inproj",
    )(h, norm_w, w_r, wgk, bgk)


CODE_BLOCK = 1
CODE_LEVEL = CODE_BLOCK + GLA_BLOCK


def _score_codes(c, levels):
    t = np.arange(c)[:, None]
    s = np.arange(c)[None, :]
    code = np.zeros((c, c), np.int32)
    code = np.where((t // GLA_BLOCK == s // GLA_BLOCK) & (s <= t), CODE_BLOCK + s % GLA_BLOCK, code)
    if levels:
        lvl, j = 2 * GLA_BLOCK, 0
        while lvl <= c:
            hit = (t // lvl == s // lvl) & (t // (lvl // 2) != s // (lvl // 2)) & (s < t)
            code = np.where(hit, CODE_LEVEL + j, code)
            lvl, j = lvl * 2, j + 1
    return jnp.asarray(code, jnp.int32)


def _level_signs(c):
    t = np.arange(c)[:, None]
    out = []
    lvl = 2 * GLA_BLOCK
    while lvl <= c:
        out.append(np.broadcast_to(np.where(t % lvl >= lvl // 2, 1.0, -1.0), (c, GLA_HEAD_K)))
        lvl *= 2
    return jnp.asarray(np.stack(out), F32)


def _rows_bcast(ref, rows, lo, reps):
    parts = []
    for r in rows:
        tile = jnp.broadcast_to(ref[pl.ds(r, 1), lo:lo + GLA_HEAD_K], (SUBLANES, GLA_HEAD_K))
        parts.extend([tile] * reps)
    return jnp.concatenate(parts, axis=0)


def _run(steps):
    try:
        while True:
            next(steps)
    except StopIteration as stop:
        return stop.value


def _iter_stages(steps):
    yield from steps
    yield


def _gla_scores_steps(q_ref, k_ref, b_ref, code_ref, r0, ks, c, sgn_ref=None):
    tile = lambda ref: ref[r0:r0 + c, ks:ks + GLA_HEAD_K]
    att = jnp.zeros((c, c), F32)
    if sgn_ref is not None:
        lvl, j = 2 * GLA_BLOCK, 0
        while lvl <= c:
            half = lvl // 2
            bmid = _rows_bcast(b_ref, [r0 + i * lvl + half - 1 for i in range(c // lvl)], ks, lvl // SUBLANES)
            e = jnp.exp2((tile(b_ref) - bmid) * sgn_ref[j])
            firsts = [slice(i * lvl, i * lvl + half) for i in range(c // lvl)]
            seconds = [slice(i * lvl + half, (i + 1) * lvl) for i in range(c // lvl)]
            q, k = tile(q_ref), tile(k_ref)
            ql = jnp.concatenate([q[rs] * e[rs] for rs in seconds], axis=0).astype(BF16)
            zero = jnp.zeros((half, GLA_HEAD_K), F32)
            kl = jnp.concatenate([x for rs in firsts for x in (k[rs] * e[rs], zero)], axis=0).astype(BF16)
            att_l = _dot_nt(ql, kl)
            pieces = []
            for i, (fs, ss) in enumerate(zip(firsts, seconds)):
                hit = code_ref[ss, :] == CODE_LEVEL + j
                pieces += [att[fs], jnp.where(hit, att_l[i * half:(i + 1) * half], att[ss])]
            att = jnp.concatenate(pieces, axis=0)
            lvl, j = lvl * 2, j + 1
            yield
    nblk = c // GLA_BLOCK
    for r in range(GLA_BLOCK):
        rows = [r0 + i * GLA_BLOCK + r for i in range(nblk)]
        br = _rows_bcast(b_ref, rows, ks, 1)
        kr = _rows_bcast(k_ref, rows, ks, 1)
        a = jnp.sum(tile(q_ref) * kr * jnp.exp2(tile(b_ref) - br), axis=-1, keepdims=True)
        att = jnp.where(code_ref[...] == CODE_BLOCK + r, a, att)
        yield
    return att


def _inproj_gla_body(h_ref, nw_ref, w_ref, wgk_ref, bgk_ref, code_ref, sgn_ref, gn_ref,
                     og_ref, xp_ref, u_ref, vv_ref, gt_ref, sfin_ref,
                     s_scr, q_scr, k_scr, v_scr, g_scr, b_scr, *, tiles_per_seq):
    i = pl.program_id(0)
    tm = h_ref.shape[0]
    c = GLA_CHUNK

    @pl.when(i % tiles_per_seq == 0)
    def _():
        s_scr[...] = jnp.zeros_like(s_scr)

    xn = _rms(h_ref[...], nw_ref[...]).astype(BF16)

    def proj(lo, hi):
        return _dot(xn, w_ref[:, lo:hi])

    def slice_task(lo, dst_ref, dst_lo, act):
        def run():
            dst_ref[:, dst_lo:dst_lo + PROJ_SLICE] = act(proj(lo, lo + PROJ_SLICE)).astype(dst_ref.dtype)
        return run

    def slice_tasks(cols, dst_ref, act=lambda x: x):
        return [slice_task(lo, dst_ref, lo - cols[0], act) for lo in range(cols[0], cols[1], PROJ_SLICE)]

    q_scale = GLA_HEAD_K ** -0.5
    head_tasks = (slice_tasks(C_GOUT, g_scr, jax.nn.silu) + slice_tasks(C_V, v_scr)
                  + slice_tasks((C_QK[0], C_QK[0] + GLA_KEY_DIM), q_scr, lambda x: x * q_scale)
                  + slice_tasks((C_QK[0] + GLA_KEY_DIM, C_QK[1]), k_scr))
    tails = (slice_tasks(C_U, u_ref, jax.nn.gelu) + slice_tasks(C_VV, vv_ref, jax.nn.gelu)
             + slice_tasks(C_GATES, gt_ref, jax.nn.sigmoid) + slice_tasks(C_XP, xp_ref))

    glr = proj(*C_GLR).astype(BF16)
    g2 = _log2_decay(_dot(glr, wgk_ref[...]) + bgk_ref[...])
    per_head = len(head_tasks) // GLA_HEADS
    for h in range(GLA_HEADS):
        ks = h * GLA_HEAD_K
        b_scr[:, ks:ks + GLA_HEAD_K] = _seg_cumsum(g2[:, ks:ks + GLA_HEAD_K], c)
        for t in head_tasks[h * per_head:(h + 1) * per_head]:
            t()

    def pump(n=1):
        for _ in range(min(n, len(tails))):
            tails.pop(0)()

    def head_steps(r0, h):
        ks, kv = h * GLA_HEAD_K, h * GLA_HEAD_V
        tile = lambda ref: ref[r0:r0 + c, ks:ks + GLA_HEAD_K]
        v_bf = lambda: v_scr[r0:r0 + c, kv:kv + GLA_HEAD_V]
        s = s_scr[h]
        o = _dot((tile(q_scr) * jnp.exp2(tile(b_scr))).astype(BF16), s.astype(BF16))
        b_t = tile(b_scr).T
        b_last = b_t[:, c - 1:c]
        kd_t = (tile(k_scr).T * jnp.exp2(b_last - b_t)).astype(BF16)
        s_scr[h] = jnp.exp2(b_last) * s + _dot(kd_t, v_bf())
        yield
        att = yield from _gla_scores_steps(q_scr, k_scr, b_scr, code_ref, r0, ks, c, sgn_ref)
        o = o + _dot(att.astype(BF16), v_bf())
        yield
        gate = g_scr[r0:r0 + c, kv:kv + GLA_HEAD_V].astype(F32)
        og_ref[r0:r0 + c, kv:kv + GLA_HEAD_V] = (_rms(o, gn_ref[...]) * gate).astype(BF16)

    n_levels = int(math.log2(c // GLA_BLOCK))
    stages_total = (tm // c) * GLA_HEADS * (n_levels + GLA_BLOCK + 3)
    every = max(1, stages_total // len(tails))
    stage = 0
    for ch in range(tm // c):
        for h in range(GLA_HEADS):
            for _ in _iter_stages(head_steps(ch * c, h)):
                stage += 1
                if stage % every == 0:
                    pump()
    pump(len(tails))

    @pl.when(i % tiles_per_seq == tiles_per_seq - 1)
    def _():
        sfin_ref[0] = s_scr[...]


def _inproj_gla(h, norm_w, w_r, wgk, bgk, gn, layer, *, tm, n_seq, seq_len):
    m = h.shape[0]
    c = GLA_CHUNK
    tiles_per_seq = seq_len // tm
    row = lambda w: pl.BlockSpec((tm, w), lambda i: (i, 0))
    out_shape = (
        jax.ShapeDtypeStruct((m, GLA_VAL_DIM), BF16),
        jax.ShapeDtypeStruct((m, POOL_WIDTH), F32),
        jax.ShapeDtypeStruct((m, SG_WIDTH), BF16),
        jax.ShapeDtypeStruct((m, SG_WIDTH), BF16),
        jax.ShapeDtypeStruct((m, N_BRANCHES * D_MODEL), BF16),
        jax.ShapeDtypeStruct((n_seq, GLA_HEADS, GLA_HEAD_K, GLA_HEAD_V), F32),
    )
    code, sgn = _score_codes(c, True), _level_signs(c)
    return pl.pallas_call(
        functools.partial(_inproj_gla_body, tiles_per_seq=tiles_per_seq),
        grid=(m // tm,),
        in_specs=[row(D_MODEL), _resident((1, D_MODEL), layer), _resident((D_MODEL, IN_COLS_R), layer),
                  _resident((LANES, GLA_KEY_DIM), layer), _resident((1, GLA_KEY_DIM), layer),
                  _resident(code.shape), _resident(sgn.shape), _resident((1, GLA_HEAD_V), layer)],
        out_specs=[row(GLA_VAL_DIM), row(POOL_WIDTH), row(SG_WIDTH), row(SG_WIDTH), row(N_BRANCHES * D_MODEL),
                   pl.BlockSpec((1, GLA_HEADS, GLA_HEAD_K, GLA_HEAD_V), lambda i: (i // tiles_per_seq, 0, 0, 0))],
        out_shape=out_shape,
        scratch_shapes=[pltpu.VMEM((GLA_HEADS, GLA_HEAD_K, GLA_HEAD_V), F32),
                        pltpu.VMEM((tm, GLA_KEY_DIM), F32), pltpu.VMEM((tm, GLA_KEY_DIM), F32),
                        pltpu.VMEM((tm, GLA_VAL_DIM), BF16), pltpu.VMEM((tm, GLA_VAL_DIM), BF16),
                        pltpu.VMEM((tm, GLA_KEY_DIM), F32)],
        compiler_params=pltpu.CompilerParams(
            dimension_semantics=("arbitrary",), vmem_limit_bytes=VMEM_LIMIT_BYTES),
        name="inproj_gla",
    )(h, norm_w, w_r, wgk, bgk, code, sgn, gn)


def _gla_sample_body(qk_ref, vg_ref, b_ref, s0_ref, code_ref, gn_ref, *rest, seq_len, n_blk):
    og_ref, snew_ref, q_scr, k_scr = rest[-4:]
    nb = SAMPLE_SEQS_PER_STEP
    c = nb * seq_len

    @pl.when(pl.program_id(0) >= n_blk)
    def _():
        snew_ref[...] = jnp.zeros_like(snew_ref)

    @pl.when(pl.program_id(0) < n_blk)
    def _():
        q_scr[...] = qk_ref[:, :GLA_KEY_DIM].astype(F32) * (GLA_HEAD_K ** -0.5)
        k_scr[...] = qk_ref[:, GLA_KEY_DIM:].astype(F32)
        gn = gn_ref[...]
        lane_seq = lax.broadcasted_iota(jnp.int32, (GLA_HEAD_K, c), 1) // seq_len
        last_rows = [(n + 1) * seq_len - 1 for n in range(nb)]
        for h in range(GLA_HEADS):
            ks, kv = h * GLA_HEAD_K, h * GLA_HEAD_V
            k = k_scr[:, ks:ks + GLA_HEAD_K]
            b = b_ref[:, ks:ks + GLA_HEAD_K]
            v_bf = vg_ref[:, kv:kv + GLA_HEAD_V]
            gate = vg_ref[:, GLA_VAL_DIM + kv:GLA_VAL_DIM + kv + GLA_HEAD_V].astype(F32)
            qi = q_scr[:, ks:ks + GLA_HEAD_K] * jnp.exp2(b)
            inter = [
                _dot(qi[n * seq_len:(n + 1) * seq_len, :].astype(BF16), s0_ref[n, h].astype(BF16))
                for n in range(nb)
            ]
            att = _run(_gla_scores_steps(q_scr, k_scr, b_ref, code_ref, 0, ks, c))
            o = jnp.concatenate(inter, axis=0) + _dot(att.astype(BF16), v_bf)
            og_ref[:, kv:kv + GLA_HEAD_V] = (_rms(o, gn) * gate).astype(BF16)

            b_last_rows = _rows_bcast(b_ref, last_rows, ks, 1)
            kd_t = (k * jnp.exp2(b_last_rows - b)).T
            b_t = b.T
            for n in range(nb):
                col = last_rows[n]
                decay = jnp.exp2(b_t[:, col:col + 1])
                kd_n = jnp.where(lane_seq == n, kd_t, 0.0).astype(BF16)
                snew_ref[n, h] = decay * s0_ref[n, h] + _dot(kd_n, v_bf)


def _gla_sample(qk, vg, b2, state_all, gn, layer, prev_new_state, *, seq_len):
    assert seq_len == GLA_BLOCK
    nb = SAMPLE_SEQS_PER_STEP
    depth, n_seq = state_all.shape[:2]
    n_blk = n_seq // nb
    c = nb * seq_len
    first = prev_new_state is None
    assert first == (layer == 0)
    blk = (lambda i: jnp.minimum(i, n_blk - 1)) if first else (lambda i: i)
    row = lambda w: pl.BlockSpec((c, w), lambda i: (blk(i), 0))
    st_shape = (None, nb, GLA_HEADS, GLA_HEAD_K, GLA_HEAD_V)
    st_in = pl.BlockSpec(st_shape, lambda i: (layer, blk(i), 0, 0, 0))
    st_out = pl.BlockSpec(st_shape, lambda i: (layer + i // n_blk, i % n_blk, 0, 0, 0))
    in_specs = [row(2 * GLA_KEY_DIM), row(2 * GLA_VAL_DIM), row(GLA_KEY_DIM), st_in, _resident((c, c)),
                _resident((1, GLA_HEAD_V), layer)]
    args = [qk, vg, b2, state_all, _score_codes(c, False), gn]
    aliases = {}
    if not first:
        in_specs.append(pl.BlockSpec(memory_space=pl.ANY))
        args.append(prev_new_state)
        aliases = {len(args) - 1: 1}
    return pl.pallas_call(
        functools.partial(_gla_sample_body, seq_len=seq_len, n_blk=n_blk),
        grid=(n_blk * depth if first else n_blk,),
        in_specs=in_specs,
        out_specs=[row(GLA_VAL_DIM), st_out],
        out_shape=(jax.ShapeDtypeStruct((n_seq * seq_len, GLA_VAL_DIM), BF16),
                   jax.ShapeDtypeStruct(state_all.shape, F32)),
        scratch_shapes=[pltpu.VMEM((c, GLA_KEY_DIM), F32), pltpu.VMEM((c, GLA_KEY_DIM), F32)],
        input_output_aliases=aliases,
        compiler_params=pltpu.CompilerParams(
            dimension_semantics=("arbitrary",), vmem_limit_bytes=VMEM_LIMIT_BYTES),
        name="gla_sample",
    )(*args)


def _window_sums(full):
    outs = []
    s = full
    w = 1
    for gi, win in enumerate(POOL_WINDOWS):
        while w < win:
            s = s + pltpu.roll(s, w, 0)
            w *= 2
        outs.append(s[:, gi * POOL_GROUP_DIM:(gi + 1) * POOL_GROUP_DIM])
    return jnp.concatenate(outs, axis=1)


def _pool_counts(pos, n_prev):
    grp = lax.broadcasted_iota(jnp.int32, pos.shape, pos.ndim - 1) // POOL_GROUP_DIM
    win = jnp.zeros(pos.shape, jnp.int32)
    for gi, w in enumerate(POOL_WINDOWS):
        win = jnp.where(grp == gi, w, win)
    return jnp.minimum(win, pos + 1 + n_prev).astype(F32)


POOL_PREV_ROWS = 2 * SUBLANES


def _pool_prompt_tile(x_ref, prev_ref, tile_in_seq):
    tp = x_ref.shape[0]
    x = x_ref[...]
    prev = jnp.where(tile_in_seq == 0, 0.0, prev_ref[...])
    sums = _window_sums(jnp.concatenate([prev, x], axis=0))[POOL_PREV_ROWS:, :]
    pos = _row_iota(tp, POOL_WIDTH) + tile_in_seq * tp
    return (sums / _pool_counts(pos, 0) - x).astype(BF16)


def _pool_sample_body(x_ref, hist_ref, o_ref, nh_ref, *, n_prev):
    nb, t, _ = x_ref.shape
    hr = hist_ref.shape[1]
    x = x_ref[...]
    full = jnp.concatenate([hist_ref[...], x], axis=1)
    sums = _window_sums(full.reshape(nb * (hr + t), POOL_WIDTH)).reshape(nb, hr + t, POOL_WIDTH)
    pos = lax.broadcasted_iota(jnp.int32, (nb, t, POOL_WIDTH), 1)
    o_ref[...] = (sums[:, hr:, :] / _pool_counts(pos, n_prev) - x).astype(BF16)
    nh_ref[...] = full[:, t:, :]


def _pool_sample(xp3, hist16, *, n_prev, nb):
    n, t, _ = xp3.shape
    hr = hist16.shape[1]
    blk = lambda r: pl.BlockSpec((nb, r, POOL_WIDTH), lambda i: (i, 0, 0))
    return pl.pallas_call(
        functools.partial(_pool_sample_body, n_prev=n_prev),
        grid=(n // nb,),
        in_specs=[blk(t), blk(hr)],
        out_specs=[blk(t), blk(hr)],
        out_shape=(jax.ShapeDtypeStruct((n, t, POOL_WIDTH), BF16),
                   jax.ShapeDtypeStruct((n, hr, POOL_WIDTH), F32)),
        compiler_params=pltpu.CompilerParams(dimension_semantics=("arbitrary",)),
        name="pool_sample",
    )(xp3, hist16)


def _merge_ffn_body(h_ref, og_ref, u_ref, vv_ref, gt_ref, *rest, final_norm, pool_tiles_per_seq):
    if pool_tiles_per_seq is None:
        pl_ref, rest = rest[0], rest[1:]
        pooled = lambda g: pl_ref[:, g * POOL_GROUP_DIM:(g + 1) * POOL_GROUP_DIM]
    else:
        xp_ref, prev_ref, rest = rest[0], rest[1], rest[2:]
        pooled_all = _pool_prompt_tile(xp_ref, prev_ref, pl.program_id(0) % pool_tiles_per_seq)
        pooled = lambda g: pooled_all[:, g * POOL_GROUP_DIM:(g + 1) * POOL_GROUP_DIM]
    (wmix_ref, pscale_ref, wsg_ref, bsg_ref, wa_ref, wb_ref, wc_ref, wo_ref,
     nf_ref, w1_ref, w2_ref, nfin_ref, o_ref) = rest
    tm = h_ref.shape[0]
    br = gt_ref[:, 0:D_MODEL].astype(F32) * _dot(og_ref[...], wa_ref[...])
    mixed = jnp.concatenate([_dot(pooled(g), wmix_ref[g]) for g in range(POOL_GROUPS)], axis=1)
    pool_out = (mixed * pscale_ref[...]).astype(BF16)
    br = br + gt_ref[:, D_MODEL:2 * D_MODEL].astype(F32) * _dot(pool_out, wb_ref[...])
    rows = []
    for c in range(tm // SG_CHUNK):
        r0 = c * SG_CHUNK
        vv = vv_ref[r0:r0 + SG_CHUNK, :].astype(BF16)
        sg = jnp.concatenate(
            [_dot(wsg_ref[g], vv[:, g * SG_GROUP_DIM:(g + 1) * SG_GROUP_DIM]) for g in range(SG_GROUPS)],
            axis=1) + bsg_ref[...]
        rows.append((u_ref[r0:r0 + SG_CHUNK, :].astype(F32) * sg).astype(BF16))
    sg_out = jnp.concatenate(rows, axis=0)
    br = br + gt_ref[:, 2 * D_MODEL:3 * D_MODEL].astype(F32) * _dot(sg_out, wc_ref[...])
    h = h_ref[...] + _dot(br.astype(BF16), wo_ref[...])
    a = jnp.maximum(_dot(_rms(h, nf_ref[...]).astype(BF16), w1_ref[...]), 0.0)
    h = h + _dot((a * a).astype(BF16), w2_ref[...])
    if final_norm:
        h = _rms(h, nfin_ref[...])
    o_ref[...] = h


def _merge_ffn(h, og, pool_in, u, vv, gates, layer_wts, sg_wts, nfin, layer, *, tm, final_norm, pool_seq_len=None):
    m = h.shape[0]
    row = lambda w: pl.BlockSpec((tm, w), lambda i: (i, 0))
    (wmix, pscale, wa, wb, wc, wo, nf, w1, w2) = layer_wts
    wsg, bsg = sg_wts
    lw = lambda w: _resident(w.shape[1:], layer)
    if pool_seq_len is None:
        pool_specs, pool_args, tiles_per_seq = [row(POOL_WIDTH)], [pool_in], None
    else:
        ratio = tm // POOL_PREV_ROWS
        pool_specs = [row(POOL_WIDTH),
                      pl.BlockSpec((POOL_PREV_ROWS, POOL_WIDTH), lambda i: (jnp.maximum(i * ratio - 1, 0), 0))]
        pool_args, tiles_per_seq = [pool_in, pool_in], pool_seq_len // tm
    return pl.pallas_call(
        functools.partial(_merge_ffn_body, final_norm=final_norm, pool_tiles_per_seq=tiles_per_seq),
        grid=(m // tm,),
        in_specs=[row(D_MODEL), row(GLA_VAL_DIM), row(SG_WIDTH), row(SG_WIDTH), row(N_BRANCHES * D_MODEL)]
        + pool_specs
        + [lw(wmix), lw(pscale), lw(wsg), lw(bsg), lw(wa), lw(wb), lw(wc), lw(wo), lw(nf), lw(w1), lw(w2),
           _resident(nfin.shape)],
        out_specs=row(D_MODEL),
        out_shape=jax.ShapeDtypeStruct((m, D_MODEL), F32),
        compiler_params=pltpu.CompilerParams(
            dimension_semantics=("arbitrary",), vmem_limit_bytes=VMEM_LIMIT_BYTES),
        name="merge_ffn",
    )(h, og, u, vv, gates, *pool_args, wmix, pscale, wsg, bsg, wa, wb, wc, wo, nf, w1, w2, nfin)


def _reorder_w_in_body(wt_ref, o_ref):
    glr0 = 2 * GLA_KEY_DIM + 2 * GLA_VAL_DIM
    glr1 = glr0 + GLA_GATE_RANK
    tk = wt_ref.shape[1]
    o_ref[:, :glr0] = wt_ref[:glr0, :].T.astype(BF16)
    o_ref[:, glr0:C_GLR[0]] = wt_ref[glr1:, :].T.astype(BF16)
    glr = jnp.concatenate([wt_ref[glr0:glr1, :], jnp.zeros((LANES - GLA_GATE_RANK, tk), F32)], axis=0)
    o_ref[:, C_GLR[0]:] = glr.T.astype(BF16)


def _reorder_w_in(w):
    depth, rows, cols = w.shape
    tk = 256
    return pl.pallas_call(
        _reorder_w_in_body,
        grid=(depth, rows // tk),
        in_specs=[pl.BlockSpec((None, cols, tk), lambda l, i: (l, 0, i))],
        out_specs=pl.BlockSpec((None, tk, IN_COLS_R), lambda l, i: (l, i, 0)),
        out_shape=jax.ShapeDtypeStruct((depth, rows, IN_COLS_R), BF16),
        compiler_params=pltpu.CompilerParams(dimension_semantics=("arbitrary", "arbitrary")),
        name="reorder_w_in",
    )(jnp.swapaxes(w, 1, 2))


def _sg_operands(w_spatial, b_spatial, length):
    reps = SG_CHUNK // length
    t = jnp.arange(SG_CHUNK)
    w = jnp.tile(w_spatial[:, :, :length, :length], (1, 1, reps, reps))
    keep = ((t[:, None] // length) == (t[None, :] // length)) & ((t[None, :] % length) <= (t[:, None] % length))
    w = jnp.where(keep, w, 0.0).astype(BF16)
    bias = jnp.repeat(jnp.swapaxes(jnp.tile(b_spatial[:, :, :length], (1, 1, reps)), 1, 2), SG_GROUP_DIM, axis=2)
    return w, bias.astype(F32)


def kernel(x_prompt, x_sample, state_gla, state_pool, norm_mix, w_in, w_gk2, b_gk, gla_norm, w_pool_mix,
           pool_scale, w_spatial, b_spatial, w_br_a, w_br_b, w_br_c, w_out, norm_ffn, w_ff1, w_ff2,
           norm_final):
    depth = w_in.shape[0]
    n_p, t_p, _ = x_prompt.shape
    n_s, t_s, _ = x_sample.shape
    hp = x_prompt.reshape(n_p * t_p, D_MODEL)
    hs = x_sample.reshape(n_s * t_s, D_MODEL)
    n_prev_s = POOL_HIST
    rows = lambda a: a.reshape(depth, 1, -1).astype(F32)
    bf = lambda a: a.astype(BF16)

    w_r = _reorder_w_in(w_in)
    wgk = bf(jnp.pad(w_gk2, ((0, 0), (0, LANES - GLA_GATE_RANK), (0, 0))))
    bgk, gn, nmix = rows(b_gk), rows(gla_norm), rows(norm_mix)
    layer_wts = (bf(w_pool_mix), rows(pool_scale), bf(w_br_a), bf(w_br_b), bf(w_br_c), bf(w_out),
                 rows(norm_ffn), bf(w_ff1), bf(w_ff2))
    sg_p = _sg_operands(w_spatial, b_spatial, min(t_p, SG_CHUNK))
    sg_s = _sg_operands(w_spatial, b_spatial, min(t_s, SG_CHUNK))
    nfin = norm_final.reshape(1, -1).astype(F32)
    hist16 = jnp.pad(state_pool, ((0, 0), (0, 0), (1, 0), (0, 0)))

    gla_p, pool_p, pool_s, sgv_s = [], [], [], []
    gla_s = None
    for l in range(depth):
        final = l == depth - 1
        og, xp, u, vv, gates, s_fin = _inproj_gla(hp, nmix, w_r, wgk, bgk, gn, l, tm=TM_PROJ, n_seq=n_p,
                                                  seq_len=t_p)
        hp = _merge_ffn(hp, og, xp, u, vv, gates, layer_wts, sg_p, nfin, l, tm=TM_MERGE, final_norm=final,
                        pool_seq_len=t_p)
        gla_p.append(s_fin)
        pool_p.append(xp.reshape(n_p, t_p, POOL_WIDTH)[:, t_p - POOL_HIST:, :])

        qk, vg, xp, u, vv, gates, b2 = _inproj(hs, nmix, w_r, wgk, bgk, l, tm=TM_PROJ, seg=t_s, vv_dtype=F32)
        og, gla_s = _gla_sample(qk, vg, b2, state_gla, gn, l, gla_s, seq_len=t_s)
        pooled, new_hist = _pool_sample(xp.reshape(n_s, t_s, POOL_WIDTH), hist16[l], n_prev=n_prev_s, nb=16)
        hs = _merge_ffn(hs, og, pooled.reshape(n_s * t_s, POOL_WIDTH), u, vv, gates, layer_wts, sg_s, nfin, l,
                        tm=TM_MERGE, final_norm=final)
        pool_s.append(new_hist[:, 1:, :])
        sgv_s.append(vv.reshape(n_s, t_s, SG_WIDTH))

    return (hp.reshape(n_p, t_p, D_MODEL), hs.reshape(n_s, t_s, D_MODEL), jnp.stack(gla_p), gla_s,
            jnp.stack(pool_p), jnp.stack(pool_s), jnp.stack(sgv_s))
```

```python
import functools
import math

import jax
import jax.numpy as jnp
import numpy as np
from jax import lax
from jax.experimental import pallas as pl
from jax.experimental.pallas import tpu as pltpu

F32 = jnp.float32
BF16 = jnp.bfloat16

D_MODEL = 1024
GLA_HEADS = 4
GLA_KEY_DIM = 512
GLA_VAL_DIM = 1024
GLA_HEAD_K = 128
GLA_HEAD_V = 256
GLA_GATE_RANK = 16
GLA_GATE_NORMALIZER = 16.0
POOL_WIDTH = 512
POOL_WINDOWS = (2, 4, 8, 16)
POOL_GROUPS = 4
POOL_GROUP_DIM = 128
POOL_HIST = 15
SG_WIDTH = 512
SG_GROUPS = 4
SG_GROUP_DIM = 128
SG_CHUNK = 128
N_BRANCHES = 3
D_FF = 4096
EPS = 1e-6

SUBLANES = 8
LANES = 128
VMEM_LIMIT_BYTES = 56 * 1024 * 1024

C_QK = (0, 1024)
C_V = (1024, 2048)
C_GOUT = (2048, 3072)
C_XP = (3072, 3584)
C_U = (3584, 4096)
C_VV = (4096, 4608)
C_GATES = (4608, 7680)
C_GLR = (7680, 7808)
IN_COLS_R = 7808

TM_PROJ = 512
TM_MERGE = 512
PROJ_SLICE = 256
GLA_CHUNK = 128
GLA_BLOCK = SUBLANES
SAMPLE_SEQS_PER_STEP = 16
LOG2_DECAY_SCALE = math.log2(math.e) / GLA_GATE_NORMALIZER


def _resident(shape, layer=None):
    nd = len(shape)
    if layer is None:
        return pl.BlockSpec(shape, lambda *_: (0,) * nd, pipeline_mode=pl.Buffered(1))
    return pl.BlockSpec((None,) + tuple(shape), lambda *_: (layer,) + (0,) * nd,
                        pipeline_mode=pl.Buffered(1))


def _rms(x, w):
    return x * lax.rsqrt(jnp.mean(x * x, axis=-1, keepdims=True) + EPS) * w


def _dot(a, b):
    return jnp.dot(a, b, preferred_element_type=F32)


def _dot_nt(a, b):
    return lax.dot_general(a, b, (((1,), (1,)), ((), ())), preferred_element_type=F32)


def _row_iota(c, w):
    return lax.broadcasted_iota(jnp.int32, (c, w), 0)


def _bcast_row(x, blk, r):
    c, w = x.shape
    x3 = x.reshape(c // blk, blk, w)
    return jnp.broadcast_to(x3[:, r:r + 1, :], x3.shape).reshape(c, w)


def _seg_cumsum(g, seg):
    c, w = g.shape
    tpos = _row_iota(c, w) % SUBLANES
    for sh in (1, 2, 4):
        g = g + jnp.where(tpos >= sh, pltpu.roll(g, sh, 0), 0.0)
    if seg > SUBLANES:
        nb = seg // SUBLANES
        x = g.reshape(c // seg, nb, SUBLANES, w)
        tot = _bcast_row(g, SUBLANES, SUBLANES - 1).reshape(c // seg, nb, SUBLANES, w)
        outs, carry = [x[:, 0]], tot[:, 0]
        for i in range(1, nb):
            outs.append(x[:, i] + carry)
            carry = carry + tot[:, i]
        g = jnp.stack(outs, axis=1).reshape(c, w)
    return g


def _log2_decay(x):
    y = jnp.exp2(jnp.abs(x) * -math.log2(math.e))
    return jnp.minimum(x, 0.0) * LOG2_DECAY_SCALE - jnp.log2(1.0 + y) * (math.log(2.0) * LOG2_DECAY_SCALE)


def _inproj_body(h_ref, nw_ref, w_ref, wgk_ref, bgk_ref,
                 qk_ref, vg_ref, xp_ref, u_ref, vv_ref, gt_ref, b2_ref, *, seg):
    xn = _rms(h_ref[...], nw_ref[...]).astype(BF16)

    def proj(lo, hi):
        return _dot(xn, w_ref[:, lo:hi])

    g2 = _log2_decay(_dot(proj(*C_GLR).astype(BF16), wgk_ref[...]) + bgk_ref[...])
    b2_ref[...] = _seg_cumsum(g2, seg)
    vg_ref[:, GLA_VAL_DIM:] = jax.nn.silu(proj(*C_GOUT)).astype(BF16)
    u_ref[...] = jax.nn.gelu(proj(*C_U)).astype(u_ref.dtype)
    vv_ref[...] = jax.nn.gelu(proj(*C_VV)).astype(vv_ref.dtype)
    for j in range(N_BRANCHES):
        lo = C_GATES[0] + j * D_MODEL
        gt_ref[:, j * D_MODEL:(j + 1) * D_MODEL] = jax.nn.sigmoid(proj(lo, lo + D_MODEL)).astype(BF16)
    xp_ref[...] = proj(*C_XP)
    vg_ref[:, :GLA_VAL_DIM] = proj(*C_V).astype(BF16)
    qk_ref[...] = proj(*C_QK).astype(BF16)


def _inproj(h, norm_w, w_r, wgk, bgk, layer, *, tm, seg, vv_dtype):
    m = h.shape[0]
    row = lambda w: pl.BlockSpec((tm, w), lambda i: (i, 0))
    out_shape = (
        jax.ShapeDtypeStruct((m, 2 * GLA_KEY_DIM), BF16),
        jax.ShapeDtypeStruct((m, 2 * GLA_VAL_DIM), BF16),
        jax.ShapeDtypeStruct((m, POOL_WIDTH), F32),
        jax.ShapeDtypeStruct((m, SG_WIDTH), BF16),
        jax.ShapeDtypeStruct((m, SG_WIDTH), vv_dtype),
        jax.ShapeDtypeStruct((m, N_BRANCHES * D_MODEL), BF16),
        jax.ShapeDtypeStruct((m, GLA_KEY_DIM), F32),
    )
    return pl.pallas_call(
        functools.partial(_inproj_body, seg=seg),
        grid=(m // tm,),
        in_specs=[row(D_MODEL), _resident((1, D_MODEL), layer), _resident((D_MODEL, IN_COLS_R), layer),
                  _resident((LANES, GLA_KEY_DIM), layer), _resident((1, GLA_KEY_DIM), layer)],
        out_specs=[row(2 * GLA_KEY_DIM), row(2 * GLA_VAL_DIM), row(POOL_WIDTH), row(SG_WIDTH), row(SG_WIDTH),
                   row(N_BRANCHES * D_MODEL), row(GLA_KEY_DIM)],
        out_shape=out_shape,
        compiler_params=pltpu.CompilerParams(
            dimension_semantics=("arbitrary",), vmem_limit_bytes=VMEM_LIMIT_BYTES),
        name="inproj",
    )(h, norm_w, w_r, wgk, bgk)


CODE_BLOCK = 1
CODE_LEVEL = CODE_BLOCK + GLA_BLOCK


def _score_codes(c, levels):
    t = np.arange(c)[:, None]
    s = np.arange(c)[None, :]
    code = np.zeros((c, c), np.int32)
    code = np.where((t // GLA_BLOCK == s // GLA_BLOCK) & (s <= t), CODE_BLOCK + s % GLA_BLOCK, code)
    if levels:
        lvl, j = 2 * GLA_BLOCK, 0
        while lvl <= c:
            hit = (t // lvl == s // lvl) & (t // (lvl // 2) != s // (lvl // 2)) & (s < t)
            code = np.where(hit, CODE_LEVEL + j, code)
            lvl, j = lvl * 2, j + 1
    return jnp.asarray(code, jnp.int32)


def _level_signs(c):
    t = np.arange(c)[:, None]
    out = []
    lvl = 2 * GLA_BLOCK
    while lvl <= c:
        out.append(np.broadcast_to(np.where(t % lvl >= lvl // 2, 1.0, -1.0), (c, GLA_HEAD_K)))
        lvl *= 2
    return jnp.asarray(np.stack(out), F32)


def _rows_bcast(ref, rows, lo, reps):
    parts = []
    for r in rows:
        tile = jnp.broadcast_to(ref[pl.ds(r, 1), lo:lo + GLA_HEAD_K], (SUBLANES, GLA_HEAD_K))
        parts.extend([tile] * reps)
    return jnp.concatenate(parts, axis=0)


def _run(steps):
    try:
        while True:
            next(steps)
    except StopIteration as stop:
        return stop.value


def _iter_stages(steps):
    yield from steps
    yield


def _gla_scores_steps(q_ref, k_ref, b_ref, code_ref, r0, ks, c, sgn_ref=None):
    tile = lambda ref: ref[r0:r0 + c, ks:ks + GLA_HEAD_K]
    att = jnp.zeros((c, c), F32)
    if sgn_ref is not None:
        lvl, j = 2 * GLA_BLOCK, 0
        while lvl <= c:
            half = lvl // 2
            bmid = _rows_bcast(b_ref, [r0 + i * lvl + half - 1 for i in range(c // lvl)], ks, lvl // SUBLANES)
            e = jnp.exp2((tile(b_ref) - bmid) * sgn_ref[j])
            firsts = [slice(i * lvl, i * lvl + half) for i in range(c // lvl)]
            seconds = [slice(i * lvl + half, (i + 1) * lvl) for i in range(c // lvl)]
            q, k = tile(q_ref), tile(k_ref)
            ql = jnp.concatenate([q[rs] * e[rs] for rs in seconds], axis=0).astype(BF16)
            zero = jnp.zeros((half, GLA_HEAD_K), F32)
            kl = jnp.concatenate([x for rs in firsts for x in (k[rs] * e[rs], zero)], axis=0).astype(BF16)
            att_l = _dot_nt(ql, kl)
            pieces = []
            for i, (fs, ss) in enumerate(zip(firsts, seconds)):
                hit = code_ref[ss, :] == CODE_LEVEL + j
                pieces += [att[fs], jnp.where(hit, att_l[i * half:(i + 1) * half], att[ss])]
            att = jnp.concatenate(pieces, axis=0)
            lvl, j = lvl * 2, j + 1
            yield
    nblk = c // GLA_BLOCK
    for r in range(GLA_BLOCK):
        rows = [r0 + i * GLA_BLOCK + r for i in range(nblk)]
        br = _rows_bcast(b_ref, rows, ks, 1)
        kr = _rows_bcast(k_ref, rows, ks, 1)
        a = jnp.sum(tile(q_ref) * kr * jnp.exp2(tile(b_ref) - br), axis=-1, keepdims=True)
        att = jnp.where(code_ref[...] == CODE_BLOCK + r, a, att)
        yield
    return att


def _inproj_gla_body(h_ref, nw_ref, w_ref, wgk_ref, bgk_ref, code_ref, sgn_ref, gn_ref,
                     og_ref, xp_ref, u_ref, vv_ref, gt_ref, sfin_ref,
                     s_scr, q_scr, k_scr, v_scr, g_scr, b_scr, *, tiles_per_seq):
    i = pl.program_id(0)
    tm = h_ref.shape[0]
    c = GLA_CHUNK

    @pl.when(i % tiles_per_seq == 0)
    def _():
        s_scr[...] = jnp.zeros_like(s_scr)

    xn = _rms(h_ref[...], nw_ref[...]).astype(BF16)

    def proj(lo, hi):
        return _dot(xn, w_ref[:, lo:hi])

    def slice_task(lo, dst_ref, dst_lo, act):
        def run():
            dst_ref[:, dst_lo:dst_lo + PROJ_SLICE] = act(proj(lo, lo + PROJ_SLICE)).astype(dst_ref.dtype)
        return run

    def slice_tasks(cols, dst_ref, act=lambda x: x):
        return [slice_task(lo, dst_ref, lo - cols[0], act) for lo in range(cols[0], cols[1], PROJ_SLICE)]

    q_scale = GLA_HEAD_K ** -0.5
    head_tasks = (slice_tasks(C_GOUT, g_scr, jax.nn.silu) + slice_tasks(C_V, v_scr)
                  + slice_tasks((C_QK[0], C_QK[0] + GLA_KEY_DIM), q_scr, lambda x: x * q_scale)
                  + slice_tasks((C_QK[0] + GLA_KEY_DIM, C_QK[1]), k_scr))
    tails = (slice_tasks(C_XP, xp_ref) + slice_tasks(C_GATES, gt_ref, jax.nn.sigmoid)
             + slice_tasks(C_U, u_ref, jax.nn.gelu) + slice_tasks(C_VV, vv_ref, jax.nn.gelu))

    def pump(n=1):
        for _ in range(min(n, len(tails))):
            tails.pop(0)()

    glr = proj(*C_GLR).astype(BF16)
    per_head = len(head_tasks) // GLA_HEADS
    for h in range(GLA_HEADS):
        ks = h * GLA_HEAD_K
        g2 = _log2_decay(_dot(glr, wgk_ref[:, ks:ks + GLA_HEAD_K]) + bgk_ref[:, ks:ks + GLA_HEAD_K])
        b_scr[:, ks:ks + GLA_HEAD_K] = _seg_cumsum(g2, c)
        for t in head_tasks[h * per_head:(h + 1) * per_head]:
            t()
        pump()

    def head_steps(r0, h):
        ks, kv = h * GLA_HEAD_K, h * GLA_HEAD_V
        tile = lambda ref: ref[r0:r0 + c, ks:ks + GLA_HEAD_K]
        v_bf = lambda: v_scr[r0:r0 + c, kv:kv + GLA_HEAD_V]
        s = s_scr[h]
        o = _dot((tile(q_scr) * jnp.exp2(tile(b_scr))).astype(BF16), s.astype(BF16))
        b_t = tile(b_scr).T
        b_last = b_t[:, c - 1:c]
        kd_t = (tile(k_scr).T * jnp.exp2(b_last - b_t)).astype(BF16)
        s_scr[h] = jnp.exp2(b_last) * s + _dot(kd_t, v_bf())
        yield
        att = yield from _gla_scores_steps(q_scr, k_scr, b_scr, code_ref, r0, ks, c, sgn_ref)
        o = o + _dot(att.astype(BF16), v_bf())
        yield
        gate = g_scr[r0:r0 + c, kv:kv + GLA_HEAD_V].astype(F32)
        og_ref[r0:r0 + c, kv:kv + GLA_HEAD_V] = (_rms(o, gn_ref[...]) * gate).astype(BF16)

    n_levels = int(math.log2(c // GLA_BLOCK))
    stages_total = (tm // c) * GLA_HEADS * (n_levels + GLA_BLOCK + 3)
    every = max(1, stages_total // len(tails))
    stage = 0
    for ch in range(tm // c):
        for h in range(GLA_HEADS):
            for _ in _iter_stages(head_steps(ch * c, h)):
                stage += 1
                if stage % every == 0:
                    pump()
    pump(len(tails))

    @pl.when(i % tiles_per_seq == tiles_per_seq - 1)
    def _():
        sfin_ref[0] = s_scr[...]


def _inproj_gla(h, norm_w, w_r, wgk, bgk, gn, layer, *, tm, n_seq, seq_len):
    m = h.shape[0]
    c = GLA_CHUNK
    tiles_per_seq = seq_len // tm
    row = lambda w: pl.BlockSpec((tm, w), lambda i: (i, 0))
    out_shape = (
        jax.ShapeDtypeStruct((m, GLA_VAL_DIM), BF16),
        jax.ShapeDtypeStruct((m, POOL_WIDTH), F32),
        jax.ShapeDtypeStruct((m, SG_WIDTH), BF16),
        jax.ShapeDtypeStruct((m, SG_WIDTH), BF16),
        jax.ShapeDtypeStruct((m, N_BRANCHES * D_MODEL), BF16),
        jax.ShapeDtypeStruct((n_seq, GLA_HEADS, GLA_HEAD_K, GLA_HEAD_V), F32),
    )
    code, sgn = _score_codes(c, True), _level_signs(c)
    return pl.pallas_call(
        functools.partial(_inproj_gla_body, tiles_per_seq=tiles_per_seq),
        grid=(m // tm,),
        in_specs=[row(D_MODEL), _resident((1, D_MODEL), layer), _resident((D_MODEL, IN_COLS_R), layer),
                  _resident((LANES, GLA_KEY_DIM), layer), _resident((1, GLA_KEY_DIM), layer),
                  _resident(code.shape), _resident(sgn.shape), _resident((1, GLA_HEAD_V), layer)],
        out_specs=[row(GLA_VAL_DIM), row(POOL_WIDTH), row(SG_WIDTH), row(SG_WIDTH), row(N_BRANCHES * D_MODEL),
                   pl.BlockSpec((1, GLA_HEADS, GLA_HEAD_K, GLA_HEAD_V), lambda i: (i // tiles_per_seq, 0, 0, 0))],
        out_shape=out_shape,
        scratch_shapes=[pltpu.VMEM((GLA_HEADS, GLA_HEAD_K, GLA_HEAD_V), F32),
                        pltpu.VMEM((tm, GLA_KEY_DIM), F32), pltpu.VMEM((tm, GLA_KEY_DIM), F32),
                        pltpu.VMEM((tm, GLA_VAL_DIM), BF16), pltpu.VMEM((tm, GLA_VAL_DIM), BF16),
                        pltpu.VMEM((tm, GLA_KEY_DIM), F32)],
        compiler_params=pltpu.CompilerParams(
            dimension_semantics=("arbitrary",), vmem_limit_bytes=VMEM_LIMIT_BYTES),
        name="inproj_gla",
    )(h, norm_w, w_r, wgk, bgk, code, sgn, gn)


def _gla_sample_body(qk_ref, vg_ref, b_ref, s0_ref, code_ref, gn_ref, *rest, seq_len, n_blk):
    og_ref, snew_ref, q_scr, k_scr = rest[-4:]
    nb = SAMPLE_SEQS_PER_STEP
    c = nb * seq_len

    @pl.when(pl.program_id(0) >= n_blk)
    def _():
        snew_ref[...] = jnp.zeros_like(snew_ref)

    @pl.when(pl.program_id(0) < n_blk)
    def _():
        q_scr[...] = qk_ref[:, :GLA_KEY_DIM].astype(F32) * (GLA_HEAD_K ** -0.5)
        k_scr[...] = qk_ref[:, GLA_KEY_DIM:].astype(F32)
        gn = gn_ref[...]
        lane_seq = lax.broadcasted_iota(jnp.int32, (GLA_HEAD_K, c), 1) // seq_len
        last_rows = [(n + 1) * seq_len - 1 for n in range(nb)]
        for h in range(GLA_HEADS):
            ks, kv = h * GLA_HEAD_K, h * GLA_HEAD_V
            k = k_scr[:, ks:ks + GLA_HEAD_K]
            b = b_ref[:, ks:ks + GLA_HEAD_K]
            v_bf = vg_ref[:, kv:kv + GLA_HEAD_V]
            gate = vg_ref[:, GLA_VAL_DIM + kv:GLA_VAL_DIM + kv + GLA_HEAD_V].astype(F32)
            qi = q_scr[:, ks:ks + GLA_HEAD_K] * jnp.exp2(b)
            inter = [
                _dot(qi[n * seq_len:(n + 1) * seq_len, :].astype(BF16), s0_ref[n, h].astype(BF16))
                for n in range(nb)
            ]
            att = _run(_gla_scores_steps(q_scr, k_scr, b_ref, code_ref, 0, ks, c))
            o = jnp.concatenate(inter, axis=0) + _dot(att.astype(BF16), v_bf)
            og_ref[:, kv:kv + GLA_HEAD_V] = (_rms(o, gn) * gate).astype(BF16)

            b_last_rows = _rows_bcast(b_ref, last_rows, ks, 1)
            kd_t = (k * jnp.exp2(b_last_rows - b)).T
            b_t = b.T
            for n in range(nb):
                col = last_rows[n]
                decay = jnp.exp2(b_t[:, col:col + 1])
                kd_n = jnp.where(lane_seq == n, kd_t, 0.0).astype(BF16)
                snew_ref[n, h] = decay * s0_ref[n, h] + _dot(kd_n, v_bf)


def _gla_sample(qk, vg, b2, state_all, gn, layer, prev_new_state, *, seq_len):
    assert seq_len == GLA_BLOCK
    nb = SAMPLE_SEQS_PER_STEP
    depth, n_seq = state_all.shape[:2]
    n_blk = n_seq // nb
    c = nb * seq_len
    first = prev_new_state is None
    assert first == (layer == 0)
    blk = (lambda i: jnp.minimum(i, n_blk - 1)) if first else (lambda i: i)
    row = lambda w: pl.BlockSpec((c, w), lambda i: (blk(i), 0))
    st_shape = (None, nb, GLA_HEADS, GLA_HEAD_K, GLA_HEAD_V)
    st_in = pl.BlockSpec(st_shape, lambda i: (layer, blk(i), 0, 0, 0))
    st_out = pl.BlockSpec(st_shape, lambda i: (layer + i // n_blk, i % n_blk, 0, 0, 0))
    in_specs = [row(2 * GLA_KEY_DIM), row(2 * GLA_VAL_DIM), row(GLA_KEY_DIM), st_in, _resident((c, c)),
                _resident((1, GLA_HEAD_V), layer)]
    args = [qk, vg, b2, state_all, _score_codes(c, False), gn]
    aliases = {}
    if not first:
        in_specs.append(pl.BlockSpec(memory_space=pl.ANY))
        args.append(prev_new_state)
        aliases = {len(args) - 1: 1}
    return pl.pallas_call(
        functools.partial(_gla_sample_body, seq_len=seq_len, n_blk=n_blk),
        grid=(n_blk * depth if first else n_blk,),
        in_specs=in_specs,
        out_specs=[row(GLA_VAL_DIM), st_out],
        out_shape=(jax.ShapeDtypeStruct((n_seq * seq_len, GLA_VAL_DIM), BF16),
                   jax.ShapeDtypeStruct(state_all.shape, F32)),
        scratch_shapes=[pltpu.VMEM((c, GLA_KEY_DIM), F32), pltpu.VMEM((c, GLA_KEY_DIM), F32)],
        input_output_aliases=aliases,
        compiler_params=pltpu.CompilerParams(
            dimension_semantics=("arbitrary",), vmem_limit_bytes=VMEM_LIMIT_BYTES),
        name="gla_sample",
    )(*args)


def _window_sums(full):
    outs = []
    s = full
    w = 1
    for gi, win in enumerate(POOL_WINDOWS):
        while w < win:
            s = s + pltpu.roll(s, w, 0)
            w *= 2
        outs.append(s[:, gi * POOL_GROUP_DIM:(gi + 1) * POOL_GROUP_DIM])
    return jnp.concatenate(outs, axis=1)


def _pool_counts(pos, n_prev):
    grp = lax.broadcasted_iota(jnp.int32, pos.shape, pos.ndim - 1) // POOL_GROUP_DIM
    win = jnp.zeros(pos.shape, jnp.int32)
    for gi, w in enumerate(POOL_WINDOWS):
        win = jnp.where(grp == gi, w, win)
    return jnp.minimum(win, pos + 1 + n_prev).astype(F32)


POOL_PREV_ROWS = 2 * SUBLANES


def _pool_prompt_tile(x_ref, prev_ref, tile_in_seq):
    tp = x_ref.shape[0]
    x = x_ref[...]
    prev = jnp.where(tile_in_seq == 0, 0.0, prev_ref[...])
    sums = _window_sums(jnp.concatenate([prev, x], axis=0))[POOL_PREV_ROWS:, :]
    pos = _row_iota(tp, POOL_WIDTH) + tile_in_seq * tp
    return (sums / _pool_counts(pos, 0) - x).astype(BF16)


def _pool_sample_body(x_ref, hist_ref, o_ref, nh_ref, *, n_prev):
    nb, t, _ = x_ref.shape
    hr = hist_ref.shape[1]
    x = x_ref[...]
    full = jnp.concatenate([hist_ref[...], x], axis=1)
    sums = _window_sums(full.reshape(nb * (hr + t), POOL_WIDTH)).reshape(nb, hr + t, POOL_WIDTH)
    pos = lax.broadcasted_iota(jnp.int32, (nb, t, POOL_WIDTH), 1)
    o_ref[...] = (sums[:, hr:, :] / _pool_counts(pos, n_prev) - x).astype(BF16)
    nh_ref[...] = full[:, t:, :]


def _pool_sample(xp3, hist16, *, n_prev, nb):
    n, t, _ = xp3.shape
    hr = hist16.shape[1]
    blk = lambda r: pl.BlockSpec((nb, r, POOL_WIDTH), lambda i: (i, 0, 0))
    return pl.pallas_call(
        functools.partial(_pool_sample_body, n_prev=n_prev),
        grid=(n // nb,),
        in_specs=[blk(t), blk(hr)],
        out_specs=[blk(t), blk(hr)],
        out_shape=(jax.ShapeDtypeStruct((n, t, POOL_WIDTH), BF16),
                   jax.ShapeDtypeStruct((n, hr, POOL_WIDTH), F32)),
        compiler_params=pltpu.CompilerParams(dimension_semantics=("arbitrary",)),
        name="pool_sample",
    )(xp3, hist16)


def _merge_ffn_body(h_ref, og_ref, u_ref, vv_ref, gt_ref, *rest, final_norm, pool_tiles_per_seq):
    if pool_tiles_per_seq is None:
        pl_ref, rest = rest[0], rest[1:]
        pooled = lambda g: pl_ref[:, g * POOL_GROUP_DIM:(g + 1) * POOL_GROUP_DIM]
    else:
        xp_ref, prev_ref, rest = rest[0], rest[1], rest[2:]
        pooled_all = _pool_prompt_tile(xp_ref, prev_ref, pl.program_id(0) % pool_tiles_per_seq)
        pooled = lambda g: pooled_all[:, g * POOL_GROUP_DIM:(g + 1) * POOL_GROUP_DIM]
    (wmix_ref, pscale_ref, wsg_ref, bsg_ref, wa_ref, wb_ref, wc_ref, wo_ref,
     nf_ref, w1_ref, w2_ref, nfin_ref, o_ref) = rest
    tm = h_ref.shape[0]
    br = gt_ref[:, 0:D_MODEL].astype(F32) * _dot(og_ref[...], wa_ref[...])
    mixed = jnp.concatenate([_dot(pooled(g), wmix_ref[g]) for g in range(POOL_GROUPS)], axis=1)
    pool_out = (mixed * pscale_ref[...]).astype(BF16)
    br = br + gt_ref[:, D_MODEL:2 * D_MODEL].astype(F32) * _dot(pool_out, wb_ref[...])
    rows = []
    for c in range(tm // SG_CHUNK):
        r0 = c * SG_CHUNK
        vv = vv_ref[r0:r0 + SG_CHUNK, :].astype(BF16)
        sg = jnp.concatenate(
            [_dot(wsg_ref[g], vv[:, g * SG_GROUP_DIM:(g + 1) * SG_GROUP_DIM]) for g in range(SG_GROUPS)],
            axis=1) + bsg_ref[...]
        rows.append((u_ref[r0:r0 + SG_CHUNK, :].astype(F32) * sg).astype(BF16))
    sg_out = jnp.concatenate(rows, axis=0)
    br = br + gt_ref[:, 2 * D_MODEL:3 * D_MODEL].astype(F32) * _dot(sg_out, wc_ref[...])
    h = h_ref[...] + _dot(br.astype(BF16), wo_ref[...])
    a = jnp.maximum(_dot(_rms(h, nf_ref[...]).astype(BF16), w1_ref[...]), 0.0)
    h = h + _dot((a * a).astype(BF16), w2_ref[...])
    if final_norm:
        h = _rms(h, nfin_ref[...])
    o_ref[...] = h


def _merge_ffn(h, og, pool_in, u, vv, gates, layer_wts, sg_wts, nfin, layer, *, tm, final_norm, pool_seq_len=None):
    m = h.shape[0]
    row = lambda w: pl.BlockSpec((tm, w), lambda i: (i, 0))
    (wmix, pscale, wa, wb, wc, wo, nf, w1, w2) = layer_wts
    wsg, bsg = sg_wts
    lw = lambda w: _resident(w.shape[1:], layer)
    if pool_seq_len is None:
        pool_specs, pool_args, tiles_per_seq = [row(POOL_WIDTH)], [pool_in], None
    else:
        ratio = tm // POOL_PREV_ROWS
        pool_specs = [row(POOL_WIDTH),
                      pl.BlockSpec((POOL_PREV_ROWS, POOL_WIDTH), lambda i: (jnp.maximum(i * ratio - 1, 0), 0))]
        pool_args, tiles_per_seq = [pool_in, pool_in], pool_seq_len // tm
    return pl.pallas_call(
        functools.partial(_merge_ffn_body, final_norm=final_norm, pool_tiles_per_seq=tiles_per_seq),
        grid=(m // tm,),
        in_specs=[row(D_MODEL), row(GLA_VAL_DIM), row(SG_WIDTH), row(SG_WIDTH), row(N_BRANCHES * D_MODEL)]
        + pool_specs
        + [lw(wmix), lw(pscale), lw(wsg), lw(bsg), lw(wa), lw(wb), lw(wc), lw(wo), lw(nf), lw(w1), lw(w2),
           _resident(nfin.shape)],
        out_specs=row(D_MODEL),
        out_shape=jax.ShapeDtypeStruct((m, D_MODEL), F32),
        compiler_params=pltpu.CompilerParams(
            dimension_semantics=("arbitrary",), vmem_limit_bytes=VMEM_LIMIT_BYTES),
        name="merge_ffn",
    )(h, og, u, vv, gates, *pool_args, wmix, pscale, wsg, bsg, wa, wb, wc, wo, nf, w1, w2, nfin)


def _reorder_w_in_body(wt_ref, o_ref):
    glr0 = 2 * GLA_KEY_DIM + 2 * GLA_VAL_DIM
    glr1 = glr0 + GLA_GATE_RANK
    tk = wt_ref.shape[1]
    o_ref[:, :glr0] = wt_ref[:glr0, :].T.astype(BF16)
    o_ref[:, glr0:C_GLR[0]] = wt_ref[glr1:, :].T.astype(BF16)
    glr = jnp.concatenate([wt_ref[glr0:glr1, :], jnp.zeros((LANES - GLA_GATE_RANK, tk), F32)], axis=0)
    o_ref[:, C_GLR[0]:] = glr.T.astype(BF16)


def _reorder_w_in(w):
    depth, rows, cols = w.shape
    tk = 256
    return pl.pallas_call(
        _reorder_w_in_body,
        grid=(depth, rows // tk),
        in_specs=[pl.BlockSpec((None, cols, tk), lambda l, i: (l, 0, i))],
        out_specs=pl.BlockSpec((None, tk, IN_COLS_R), lambda l, i: (l, i, 0)),
        out_shape=jax.ShapeDtypeStruct((depth, rows, IN_COLS_R), BF16),
        compiler_params=pltpu.CompilerParams(dimension_semantics=("arbitrary", "arbitrary")),
        name="reorder_w_in",
    )(jnp.swapaxes(w, 1, 2))


def _sg_operands(w_spatial, b_spatial, length):
    reps = SG_CHUNK // length
    t = jnp.arange(SG_CHUNK)
    w = jnp.tile(w_spatial[:, :, :length, :length], (1, 1, reps, reps))
    keep = ((t[:, None] // length) == (t[None, :] // length)) & ((t[None, :] % length) <= (t[:, None] % length))
    w = jnp.where(keep, w, 0.0).astype(BF16)
    bias = jnp.repeat(jnp.swapaxes(jnp.tile(b_spatial[:, :, :length], (1, 1, reps)), 1, 2), SG_GROUP_DIM, axis=2)
    return w, bias.astype(F32)


def kernel(x_prompt, x_sample, state_gla, state_pool, norm_mix, w_in, w_gk2, b_gk, gla_norm, w_pool_mix,
           pool_scale, w_spatial, b_spatial, w_br_a, w_br_b, w_br_c, w_out, norm_ffn, w_ff1, w_ff2,
           norm_final):
    depth = w_in.shape[0]
    n_p, t_p, _ = x_prompt.shape
    n_s, t_s, _ = x_sample.shape
    hp = x_prompt.reshape(n_p * t_p, D_MODEL)
    hs = x_sample.reshape(n_s * t_s, D_MODEL)
    n_prev_s = POOL_HIST
    rows = lambda a: a.reshape(depth, 1, -1).astype(F32)
    bf = lambda a: a.astype(BF16)

    w_r = _reorder_w_in(w_in)
    wgk = bf(jnp.pad(w_gk2, ((0, 0), (0, LANES - GLA_GATE_RANK), (0, 0))))
    bgk, gn, nmix = rows(b_gk), rows(gla_norm), rows(norm_mix)
    layer_wts = (bf(w_pool_mix), rows(pool_scale), bf(w_br_a), bf(w_br_b), bf(w_br_c), bf(w_out),
                 rows(norm_ffn), bf(w_ff1), bf(w_ff2))
    sg_p = _sg_operands(w_spatial, b_spatial, min(t_p, SG_CHUNK))
    sg_s = _sg_operands(w_spatial, b_spatial, min(t_s, SG_CHUNK))
    nfin = norm_final.reshape(1, -1).astype(F32)
    hist16 = jnp.pad(state_pool, ((0, 0), (0, 0), (1, 0), (0, 0)))

    gla_p, pool_p, pool_s, sgv_s = [], [], [], []
    gla_s = None
    for l in range(depth):
        final = l == depth - 1
        og, xp, u, vv, gates, s_fin = _inproj_gla(hp, nmix, w_r, wgk, bgk, gn, l, tm=TM_PROJ, n_seq=n_p,
                                                  seq_len=t_p)
        hp = _merge_ffn(hp, og, xp, u, vv, gates, layer_wts, sg_p, nfin, l, tm=TM_MERGE, final_norm=final,
                        pool_seq_len=t_p)
        gla_p.append(s_fin)
        pool_p.append(xp.reshape(n_p, t_p, POOL_WIDTH)[:, t_p - POOL_HIST:, :])

        qk, vg, xp, u, vv, gates, b2 = _inproj(hs, nmix, w_r, wgk, bgk, l, tm=TM_PROJ, seg=t_s, vv_dtype=F32)
        og, gla_s = _gla_sample(qk, vg, b2, state_gla, gn, l, gla_s, seq_len=t_s)
        pooled, new_hist = _pool_sample(xp.reshape(n_s, t_s, POOL_WIDTH), hist16[l], n_prev=n_prev_s, nb=16)
        hs = _merge_ffn(hs, og, pooled.reshape(n_s * t_s, POOL_WIDTH), u, vv, gates, layer_wts, sg_s, nfin, l,
                        tm=TM_MERGE, final_norm=final)
        pool_s.append(new_hist[:, 1:, :])
        sgv_s.append(vv.reshape(n_s, t_s, SG_WIDTH))

    return (hp.reshape(n_p, t_p, D_MODEL), hs.reshape(n_s, t_s, D_MODEL), jnp.stack(gla_p), gla_s,
            jnp.stack(pool_p), jnp.stack(pool_s), jnp.stack(sgv_s))
```
